```python
import jax, jax.numpy as jnp
from jax import lax
import numpy as np

D_MODEL = 2048
BATCH = 32
SEQ = 256
DEPTH = 1
DEC_BATCH = 2
DEC_SEQ = 4096
PAST_LEN = 512

GRID_W = 64
NA_HEADS = 8
NA_HEAD_DIM = 128
NA_KR = 8
NA_KC = 16
MLA_HEADS = 8
Q_LORA = 512
KV_LORA = 256
QK_NOPE = 128
QK_ROPE = 64
V_DIM = 128
ROPE_AXIS = QK_ROPE // 2
ROPE_THETA = 10000.0
NA_WIDTH = NA_HEADS * NA_HEAD_DIM
MLA_WIDTH = MLA_HEADS * V_DIM
MIX_WIDTH = NA_WIDTH + MLA_WIDTH
IN_SPLITS = (NA_WIDTH, NA_WIDTH, NA_WIDTH, Q_LORA, KV_LORA, QK_ROPE)
IN_COLS = sum(IN_SPLITS)
IN_OFFSETS = [int(o) for o in np.cumsum(IN_SPLITS)[:-1]]
N_EXPERTS = 32
TOP_K = 4
D_FF = D_MODEL
SWIGLU_ALPHA = 1.702
SWIGLU_LIMIT = 7.0
MOE_BLOCK = 256
Q_BLOCK = 128
EPS = 1e-6

kernel_name = 'hybrid_na_mla_moe_dit_step'


def rms_norm(x, g):
    xf = x.astype(jnp.float32)
    y = xf * lax.rsqrt(jnp.mean(xf * xf, axis=-1, keepdims=True) + EPS)
    return (y * g.astype(jnp.float32)).astype(x.dtype)


def modulation(c, w_mod, b_mod):
    m = jax.nn.silu(c) @ w_mod + b_mod
    return jnp.split(m[:, None, :], 6, axis=-1)


def rope_tables(t, dtype):
    pos = jnp.arange(t)
    rows = (pos // GRID_W).astype(jnp.float32)
    cols = (pos % GRID_W).astype(jnp.float32)
    inv = ROPE_THETA ** (-(jnp.arange(ROPE_AXIS // 2, dtype=jnp.float32) * 2.0 / ROPE_AXIS))
    ar = rows[:, None] * inv
    ac = cols[:, None] * inv
    ang = jnp.concatenate([ar, ar, ac, ac], axis=-1)
    return jnp.cos(ang).astype(dtype), jnp.sin(ang).astype(dtype)


def axial_rope(x, cos, sin):
    def rot(z):
        z1, z2 = jnp.split(z, 2, axis=-1)
        return jnp.concatenate([-z2, z1], axis=-1)
    xr = jnp.concatenate([rot(x[..., :ROPE_AXIS]), rot(x[..., ROPE_AXIS:])], axis=-1)
    return x * cos + xr * sin


def block_attn(q, k, v):
    b, nq, h, dq = q.shape
    dv = v.shape[-1]
    nblk = nq // Q_BLOCK
    scale = dq ** -0.5
    qb = q.reshape(b, nblk, Q_BLOCK, h, dq).swapaxes(0, 1)

    def one(qi):
        s = jnp.einsum('bqhd,bkhd->bhqk', qi, k, preferred_element_type=jnp.float32) * scale
        p = jax.nn.softmax(s, axis=-1).astype(v.dtype)
        return jnp.einsum('bhqk,bkhd->bqhd', p, v)

    o = lax.map(one, qb)
    return o.swapaxes(0, 1).reshape(b, nq, h, dv)


def neighborhood_attn(q, k, v, k_ctx, v_ctx, rpb):
    b, t, h, d = q.shape
    rows = t // GRID_W
    kr = min(NA_KR, rows)
    kc = NA_KC
    scale = d ** -0.5
    qg = q.reshape(b, rows, GRID_W, h, d)
    kg = k.reshape(b, rows, GRID_W, h, d)
    vg = v.reshape(b, rows, GRID_W, h, d)
    col = jnp.arange(GRID_W)
    col_idx = jnp.clip(col - kc // 2, 0, GRID_W - kc)[:, None] + jnp.arange(kc)[None, :]
    col_off = col_idx - col[:, None]
    bias_col = rpb[:, :, col_off + NA_KC - 1]
    row_ids = jnp.arange(rows)
    row_start = jnp.clip(row_ids - kr // 2, 0, rows - kr)

    def one_row(args):
        qr, r, rs = args
        kb = lax.dynamic_slice_in_dim(kg, rs, kr, axis=1)
        vb = lax.dynamic_slice_in_dim(vg, rs, kr, axis=1)
        kw = kb[:, :, col_idx]
        vw = vb[:, :, col_idx]
        row_off = rs + jnp.arange(kr) - r
        bias = bias_col[:, row_off + NA_KR - 1].transpose(0, 2, 1, 3)
        s_loc = jnp.einsum('bqhd,bkqjhd->bhqkj', qr, kw, preferred_element_type=jnp.float32) * scale + bias[None]
        s_ctx = jnp.einsum('bqhd,bphd->bhqp', qr, k_ctx, preferred_element_type=jnp.float32) * scale
        s = jnp.concatenate([s_loc.reshape(b, h, GRID_W, kr * kc), s_ctx], axis=-1)
        p = jax.nn.softmax(s, axis=-1).astype(v.dtype)
        p_loc = p[..., :kr * kc].reshape(b, h, GRID_W, kr, kc)
        p_ctx = p[..., kr * kc:]
        return jnp.einsum('bhqkj,bkqjhd->bqhd', p_loc, vw) + jnp.einsum('bhqp,bphd->bqhd', p_ctx, v_ctx)

    o = lax.map(one_row, (qg.swapaxes(0, 1), row_ids, row_start))
    return o.swapaxes(0, 1).reshape(b, t, h, d)


def moe_ffn(h, w_router, b_router, w_gate_up, b_gate_up, w_down, b_down):
    n, d = h.shape
    logits = jnp.einsum('nd,de->ne', h, w_router, preferred_element_type=jnp.float32) + b_router.astype(jnp.float32)
    top_logit, top_idx = lax.top_k(logits, TOP_K)
    gates = jax.nn.softmax(top_logit, axis=-1)
    nk = n * TOP_K
    flat_e = top_idx.reshape(nk)
    order = jnp.argsort(flat_e)
    sorted_e = flat_e[order]
    counts = jnp.bincount(flat_e, length=N_EXPERTS)
    padded = (counts + MOE_BLOCK - 1) // MOE_BLOCK * MOE_BLOCK
    pad_end = jnp.cumsum(padded)
    pad_start = pad_end - padded
    grp_start = jnp.cumsum(counts) - counts
    dest = pad_start[sorted_e] + jnp.arange(nk) - grp_start[sorted_e]
    n_blocks = (nk + MOE_BLOCK - 1) // MOE_BLOCK + N_EXPERTS
    cap = n_blocks * MOE_BLOCK
    row_token = jnp.zeros((cap,), jnp.int32).at[dest].set((order // TOP_K).astype(jnp.int32))
    row_gate = jnp.zeros((cap,), jnp.float32).at[dest].set(gates.reshape(nk)[order])
    rows = h[row_token].reshape(n_blocks, MOE_BLOCK, d)
    block_expert = jnp.minimum(jnp.searchsorted(pad_end, jnp.arange(n_blocks) * MOE_BLOCK, side='right'), N_EXPERTS - 1)

    def expert_block(args):
        xb, e = args
        gu = xb @ w_gate_up[e] + b_gate_up[e]
        g = jnp.minimum(gu[..., 0::2], SWIGLU_LIMIT)
        u = jnp.clip(gu[..., 1::2], -SWIGLU_LIMIT, SWIGLU_LIMIT)
        act = g * jax.nn.sigmoid(g * SWIGLU_ALPHA) * (u + 1)
        return act @ w_down[e] + b_down[e]

    out = lax.map(expert_block, (rows, block_expert)).reshape(cap, d)
    out = (out.astype(jnp.float32) * row_gate[:, None]).astype(h.dtype)
    return jax.ops.segment_sum(out, row_token, num_segments=n)


def attention_inputs(h, w_in, g_q_a, w_q_b, g_kv_a):
    b, t, _ = h.shape
    na_q, na_k, na_v, q_a, kv_a, k_rope = jnp.split(h @ w_in, IN_OFFSETS, axis=-1)
    heads = lambda z: z.reshape(b, t, NA_HEADS, NA_HEAD_DIM)
    q = (rms_norm(q_a, g_q_a) @ w_q_b).reshape(b, t, MLA_HEADS, QK_NOPE + QK_ROPE)
    c_kv = rms_norm(kv_a, g_kv_a)
    return heads(na_q), heads(na_k), heads(na_v), q[..., :QK_NOPE], q[..., QK_NOPE:], c_kv, k_rope


def mla_expand(c_kv, w_kv_b):
    b, t, _ = c_kv.shape
    kv = (c_kv @ w_kv_b).reshape(b, t, MLA_HEADS, QK_NOPE + V_DIM)
    return kv[..., :QK_NOPE], kv[..., QK_NOPE:]


def mla_keys(k_nope, k_rope):
    kr = jnp.broadcast_to(k_rope[:, :, None, :], k_nope.shape[:3] + (QK_ROPE,))
    return jnp.concatenate([k_nope, kr], axis=-1)


def ffn_sublayer(x, shift, scale, gate, g_ffn, w_router, b_router, w_gate_up, b_gate_up, w_down, b_down):
    b, t, d = x.shape
    h = rms_norm(x, g_ffn) * (1 + scale) + shift
    y = moe_ffn(h.reshape(b * t, d), w_router, b_router, w_gate_up, b_gate_up, w_down, b_down)
    return x + gate * y.reshape(b, t, d)


def setup_inputs(seed: int = 0) -> dict:
    key = jax.random.key(seed)
    ks = jax.random.split(key, 26)
    f32 = jnp.float32
    nrm = lambda k, shape, s: (jax.random.normal(k, shape, f32) * s).astype(f32)
    D = D_MODEL
    return {
        'x_prompt': nrm(ks[0], (BATCH, SEQ, D), 1.0),
        'x_sample': nrm(ks[1], (DEC_BATCH, DEC_SEQ, D), 1.0),
        'cache_na_k': nrm(ks[2], (DEC_BATCH, DEPTH, PAST_LEN, NA_HEADS, NA_HEAD_DIM), 1.0),
        'cache_na_v': nrm(ks[3], (DEC_BATCH, DEPTH, PAST_LEN, NA_HEADS, NA_HEAD_DIM), 1.0),
        'cache_mla_ckv': nrm(ks[4], (DEC_BATCH, DEPTH, PAST_LEN, KV_LORA), 1.0),
        'cache_mla_krope': nrm(ks[5], (DEC_BATCH, DEPTH, PAST_LEN, QK_ROPE), 1.0),
        'c': nrm(ks[6], (DEC_BATCH, D), 1.0),
        'c_ctx': nrm(ks[7], (D,), 1.0),
        'g_attn': 1.0 + nrm(ks[8], (DEPTH, D), 0.02),
        'g_ffn': 1.0 + nrm(ks[9], (DEPTH, D), 0.02),
        'g_final': 1.0 + nrm(ks[10], (D,), 0.02),
        'w_mod': nrm(ks[11], (DEPTH, D, 6 * D), D ** -0.5),
        'b_mod': nrm(ks[12], (DEPTH, 6 * D), 0.02),
        'w_in': nrm(ks[13], (DEPTH, D, IN_COLS), D ** -0.5),
        'w_out': nrm(ks[14], (DEPTH, MIX_WIDTH, D), MIX_WIDTH ** -0.5),
        'na_rpb': nrm(ks[15], (DEPTH, NA_HEADS, 2 * NA_KR - 1, 2 * NA_KC - 1), 0.1),
        'g_q_a': 1.0 + nrm(ks[16], (DEPTH, Q_LORA), 0.02),
        'w_q_b': nrm(ks[17], (DEPTH, Q_LORA, MLA_HEADS * (QK_NOPE + QK_ROPE)), Q_LORA ** -0.5),
        'g_kv_a': 1.0 + nrm(ks[18], (DEPTH, KV_LORA), 0.02),
        'w_kv_b': nrm(ks[19], (DEPTH, KV_LORA, MLA_HEADS * (QK_NOPE + V_DIM)), KV_LORA ** -0.5),
        'w_router': nrm(ks[20], (DEPTH, D, N_EXPERTS), D ** -0.5),
        'b_router': nrm(ks[21], (DEPTH, N_EXPERTS), 0.01),
        'w_gate_up': nrm(ks[22], (DEPTH, N_EXPERTS, D, 2 * D_FF), D ** -0.5),
        'b_gate_up': nrm(ks[23], (DEPTH, N_EXPERTS, 2 * D_FF), 0.02),
        'w_down': nrm(ks[24], (DEPTH, N_EXPERTS, D_FF, D), D_FF ** -0.5),
        'b_down': nrm(ks[25], (DEPTH, N_EXPERTS, D), 0.02),
    }


def reference(x_prompt, x_sample, cache_na_k, cache_na_v, cache_mla_ckv, cache_mla_krope, c, c_ctx,
              g_attn, g_ffn, g_final, w_mod, b_mod, w_in, w_out, na_rpb, g_q_a, w_q_b, g_kv_a, w_kv_b,
              w_router, b_router, w_gate_up, b_gate_up, w_down, b_down):
    bp, sp, d = x_prompt.shape
    bd, td, _ = x_sample.shape
    cos, sin = rope_tables(td, x_sample.dtype)
    xp = x_prompt
    xs = x_sample
    st_na_k, st_na_v, st_ckv, st_krope = [], [], [], []
    for l in range(DEPTH):
        moe_w = (w_router[l], b_router[l], w_gate_up[l], b_gate_up[l], w_down[l], b_down[l])
        sa, sca, ga, sf, scf, gf = modulation(c_ctx[None, :], w_mod[l], b_mod[l])
        h = rms_norm(xp, g_attn[l]) * (1 + sca) + sa
        na_q, na_k, na_v, q_nope, q_rope, c_kv, k_rope = attention_inputs(h, w_in[l], g_q_a[l], w_q_b[l], g_kv_a[l])
        o_na = block_attn(na_q, na_k, na_v)
        k_nope, v_mla = mla_expand(c_kv, w_kv_b[l])
        o_mla = block_attn(jnp.concatenate([q_nope, q_rope], axis=-1), mla_keys(k_nope, k_rope), v_mla)
        o = jnp.concatenate([o_na.reshape(bp, sp, NA_WIDTH), o_mla.reshape(bp, sp, MLA_WIDTH)], axis=-1) @ w_out[l]
        xp = xp + ga * o
        xp = ffn_sublayer(xp, sf, scf, gf, g_ffn[l], *moe_w)
        st_na_k.append(na_k)
        st_na_v.append(na_v)
        st_ckv.append(c_kv)
        st_krope.append(k_rope)
        sa, sca, ga, sf, scf, gf = modulation(c, w_mod[l], b_mod[l])
        h = rms_norm(xs, g_attn[l]) * (1 + sca) + sa
        na_q, na_k, na_v, q_nope, q_rope, c_kv, k_rope = attention_inputs(h, w_in[l], g_q_a[l], w_q_b[l], g_kv_a[l])
        o_na = neighborhood_attn(na_q, na_k, na_v, cache_na_k[:, l], cache_na_v[:, l], na_rpb[l])
        q_rope = axial_rope(q_rope, cos[:, None, :], sin[:, None, :])
        k_rope = axial_rope(k_rope, cos, sin)
        k_nope, v_mla = mla_expand(c_kv, w_kv_b[l])
        ck_nope, cv_mla = mla_expand(cache_mla_ckv[:, l], w_kv_b[l])
        k_all = jnp.concatenate([mla_keys(k_nope, k_rope), mla_keys(ck_nope, cache_mla_krope[:, l])], axis=1)
        v_all = jnp.concatenate([v_mla, cv_mla], axis=1)
        o_mla = block_attn(jnp.concatenate([q_nope, q_rope], axis=-1), k_all, v_all)
        o = jnp.concatenate([o_na.reshape(bd, td, NA_WIDTH), o_mla.reshape(bd, td, MLA_WIDTH)], axis=-1) @ w_out[l]
        xs = xs + ga * o
        xs = ffn_sublayer(xs, sf, scf, gf, g_ffn[l], *moe_w)
    y_prompt = rms_norm(xp, g_final)
    y_sample = rms_norm(xs, g_final)
    new_na_k = jnp.stack(st_na_k, axis=1)
    new_na_v = jnp.stack(st_na_v, axis=1)
    new_mla_ckv = jnp.stack(st_ckv, axis=1)
    new_mla_krope = jnp.stack(st_krope, axis=1)
    return (y_prompt, y_sample, new_na_k, new_na_v, new_mla_ckv, new_mla_krope)
```

```python
import functools

import numpy as np
import jax
import jax.numpy as jnp
from jax import lax
from jax.experimental import pallas as pl
from jax.experimental.pallas import tpu as pltpu

F32 = jnp.float32
BF16 = jnp.bfloat16
U32 = jnp.uint32

D_MODEL = 2048
GRID_W = 64
NA_HEADS = 8
NA_HEAD_DIM = 128
NA_KR = 8
NA_KC = 16
MLA_HEADS = 8
Q_LORA = 512
KV_LORA = 256
QK_NOPE = 128
QK_ROPE = 64
V_DIM = 128
ROPE_AXIS = QK_ROPE // 2
ROPE_THETA = 10000.0
NA_WIDTH = NA_HEADS * NA_HEAD_DIM
MLA_WIDTH = MLA_HEADS * V_DIM
IN_COLS = 3 * NA_WIDTH + Q_LORA + KV_LORA + QK_ROPE
N_EXPERTS = 32
TOP_K = 4
D_FF = D_MODEL
SWIGLU_ALPHA = 1.702
SWIGLU_LIMIT = 7.0
EPS = 1e-6

LANE = 128
Q_PAD = 2 * LANE
KR_OFF = 3 * NA_WIDTH + Q_LORA + KV_LORA
IN_COLS_PAD = KR_OFF + LANE
VMEM_LIMIT = 56 * 1024 * 1024
NEG_BIG = -1e30

NA_SCALE = NA_HEAD_DIM ** -0.5
MLA_SCALE = (QK_NOPE + QK_ROPE) ** -0.5

ROW_TILE = 256
NA_Q_ROWS = 4
NA_WIN_ROWS = 12
MLA_TQ = 512
MLA_TK = 1152
MOE_CHUNK = 1024
MOE_SUB = 256
MOE_TF = 512
DISPATCH_ROWS = 256
COMBINE_TOKENS = 128
PACK_ROWS = D_MODEL // (2 * LANE)
OUT_ROWS = D_MODEL // LANE


def _cparams(sem):
    return pltpu.CompilerParams(dimension_semantics=sem, vmem_limit_bytes=VMEM_LIMIT)


def _const_spec(shape):
    nd = len(shape)
    return pl.BlockSpec(shape, lambda *a: (0,) * nd, pipeline_mode=pl.Buffered(1))


def _rms(x, g):
    return x * lax.rsqrt(jnp.mean(x * x, axis=-1, keepdims=True) + EPS) * g


def _dot(a, b):
    return jnp.dot(a, b, preferred_element_type=F32)


def _dot_nt(a, b):
    return lax.dot_general(a, b, (((1,), (1,)), ((), ())), preferred_element_type=F32)


def _mod_kernel(c_ref, w_ref, b_ref, o_ref):
    c = c_ref[...]
    s = c / (1.0 + jnp.exp(-c))
    o_ref[...] = _dot(s.astype(BF16), w_ref[...].astype(BF16)) + b_ref[...]


def _modulation(c8, w_mod, b_mod):
    n = w_mod.shape[1]
    tn = 1024
    return pl.pallas_call(
        _mod_kernel,
        grid=(n // tn,),
        in_specs=[pl.BlockSpec((8, D_MODEL), lambda j: (0, 0)),
                  pl.BlockSpec((D_MODEL, tn), lambda j: (0, j)),
                  pl.BlockSpec((1, tn), lambda j: (0, j))],
        out_specs=pl.BlockSpec((8, tn), lambda j: (0, j)),
        out_shape=jax.ShapeDtypeStruct((8, n), F32),
        compiler_params=_cparams(("arbitrary",)),
        name="modulation",
    )(c8, w_mod, b_mod)


def _pre_kernel(*refs, rope):
    if rope:
        (x_ref, mod_ref, g_ref, win_ref, gq_ref, wqb_ref, gkv_ref, cosq_ref, sinq_ref, csk_ref,
         naq_ref, nak_ref, nav_ref, q_ref, ckv_ref, krp_ref) = refs
    else:
        (x_ref, mod_ref, g_ref, win_ref, gq_ref, wqb_ref, gkv_ref,
         naq_ref, nak_ref, nav_ref, q_ref, ckv_ref, kr_ref, krp_ref) = refs
    x = x_ref[...]
    sa = mod_ref[0, 0:1, :]
    sca = mod_ref[0, 1:2, :]
    h = (_rms(x, g_ref[...]) * (1.0 + sca) + sa).astype(BF16)
    proj = _dot(h, win_ref[...])
    naq_ref[...] = proj[:, 0:NA_WIDTH].astype(naq_ref.dtype)
    nak_ref[...] = proj[:, NA_WIDTH:2 * NA_WIDTH].astype(nak_ref.dtype)
    nav_ref[...] = proj[:, 2 * NA_WIDTH:3 * NA_WIDTH].astype(nav_ref.dtype)
    q_a = proj[:, 3 * NA_WIDTH:3 * NA_WIDTH + Q_LORA]
    kv_a = proj[:, 3 * NA_WIDTH + Q_LORA:KR_OFF]
    krx = proj[:, KR_OFF:IN_COLS_PAD]
    qan = _rms(q_a, gq_ref[...]).astype(BF16)
    qq = _dot(qan, wqb_ref[...])
    ckv_ref[...] = _rms(kv_a, gkv_ref[...]).astype(ckv_ref.dtype)
    if rope:
        width = MLA_HEADS * Q_PAD
        cosq = cosq_ref[...]
        sinq = sinq_ref[...]
        for hd in range(MLA_HEADS):
            a = qq[:, hd * Q_PAD:(hd + 1) * Q_PAD]
            b = qq[:, width + hd * Q_PAD:width + (hd + 1) * Q_PAD]
            q_ref[:, hd * Q_PAD:(hd + 1) * Q_PAD] = (a * cosq + b * sinq).astype(BF16)
        y = krx * csk_ref[...]
        y = y + pltpu.roll(y, QK_ROPE, 1)
        lane = lax.broadcasted_iota(jnp.int32, y.shape, 1)
        krp_ref[...] = jnp.where(lane < QK_ROPE, y, 0.0).astype(BF16)
    else:
        q_ref[...] = qq.astype(BF16)
        kr_ref[...] = krx[:, 0:QK_ROPE]
        krp_ref[...] = krx.astype(BF16)


def _pre_attention(x, mod, g_attn, w_in, g_q_a, w_qb, g_kv_a, rope_tabs, *, rope, mod_row):
    n = x.shape[0]
    tm = ROW_TILE
    row = lambda i: (i, 0)
    in_specs = [pl.BlockSpec((tm, D_MODEL), row),
                pl.BlockSpec((1, 6, D_MODEL), lambda i: (mod_row(i), 0, 0)),
                _const_spec((1, D_MODEL)),
                _const_spec(w_in.shape),
                _const_spec((1, Q_LORA)),
                _const_spec(w_qb.shape),
                _const_spec((1, KV_LORA))]
    args = [x, mod, g_attn, w_in, g_q_a, w_qb, g_kv_a]
    qw = MLA_HEADS * Q_PAD
    if rope:
        tiles_per_seq = rope_tabs[0].shape[0] // tm
        pos = lambda i: (i % tiles_per_seq, 0)
        in_specs += [pl.BlockSpec((tm, Q_PAD), pos), pl.BlockSpec((tm, Q_PAD), pos),
                     pl.BlockSpec((tm, LANE), pos)]
        args += list(rope_tabs)
        out_shape = [jax.ShapeDtypeStruct((n, NA_WIDTH), BF16)] * 3 + [
            jax.ShapeDtypeStruct((n, qw), BF16),
            jax.ShapeDtypeStruct((n, KV_LORA), BF16),
            jax.ShapeDtypeStruct((n, LANE), BF16)]
        out_specs = [pl.BlockSpec((tm, NA_WIDTH), row)] * 3 + [
            pl.BlockSpec((tm, qw), row), pl.BlockSpec((tm, KV_LORA), row), pl.BlockSpec((tm, LANE), row)]
    else:
        out_shape = [jax.ShapeDtypeStruct((n, NA_WIDTH), BF16),
                     jax.ShapeDtypeStruct((n, NA_WIDTH), F32),
                     jax.ShapeDtypeStruct((n, NA_WIDTH), F32),
                     jax.ShapeDtypeStruct((n, qw), BF16),
                     jax.ShapeDtypeStruct((n, KV_LORA), F32),
                     jax.ShapeDtypeStruct((n, QK_ROPE), F32),
                     jax.ShapeDtypeStruct((n, LANE), BF16)]
        out_specs = [pl.BlockSpec((tm, NA_WIDTH), row)] * 3 + [
            pl.BlockSpec((tm, qw), row), pl.BlockSpec((tm, KV_LORA), row),
            pl.BlockSpec((tm, QK_ROPE), row), pl.BlockSpec((tm, LANE), row)]
    return pl.pallas_call(
        functools.partial(_pre_kernel, rope=rope),
        grid=(n // tm,),
        in_specs=in_specs,
        out_specs=out_specs,
        out_shape=out_shape,
        compiler_params=_cparams(("arbitrary",)),
        name="pre_attention_rope" if rope else "pre_attention",
    )(*args)


def _softmax_pv(s, v):
    m = jnp.max(s, axis=-1, keepdims=True)
    p = jnp.exp(s - m)
    l = jnp.sum(p, axis=-1, keepdims=True)
    return _dot(p.astype(BF16), v) / l


def _prompt_attn_kernel(naq_ref, nak_ref, nav_ref, q_ref, ckv_ref, krp_ref, wkvb_ref, ona_ref, omla_ref):
    kv = _dot(ckv_ref[...].astype(BF16), wkvb_ref[...])
    krp = krp_ref[...]
    for hd in range(NA_HEADS):
        sl = slice(hd * NA_HEAD_DIM, (hd + 1) * NA_HEAD_DIM)
        s = _dot_nt(naq_ref[:, sl], nak_ref[:, sl].astype(BF16)) * NA_SCALE
        ona_ref[:, sl] = _softmax_pv(s, nav_ref[:, sl].astype(BF16)).astype(BF16)
    kvw = QK_NOPE + V_DIM
    for hd in range(MLA_HEADS):
        kf = jnp.concatenate([kv[:, hd * kvw:hd * kvw + QK_NOPE].astype(BF16), krp], axis=-1)
        s = _dot_nt(q_ref[:, hd * Q_PAD:(hd + 1) * Q_PAD], kf) * MLA_SCALE
        v = kv[:, hd * kvw + QK_NOPE:(hd + 1) * kvw].astype(BF16)
        omla_ref[:, hd * V_DIM:(hd + 1) * V_DIM] = _softmax_pv(s, v).astype(BF16)


def _prompt_attention(naq, nak, nav, q, ckv, krp, w_kvb, seq):
    n = naq.shape[0]
    row = lambda b: (b, 0)
    return pl.pallas_call(
        _prompt_attn_kernel,
        grid=(n // seq,),
        in_specs=[pl.BlockSpec((seq, NA_WIDTH), row)] * 3 + [
            pl.BlockSpec((seq, MLA_HEADS * Q_PAD), row),
            pl.BlockSpec((seq, KV_LORA), row),
            pl.BlockSpec((seq, LANE), row),
            _const_spec(w_kvb.shape)],
        out_specs=[pl.BlockSpec((seq, NA_WIDTH), row), pl.BlockSpec((seq, MLA_WIDTH), row)],
        out_shape=[jax.ShapeDtypeStruct((n, NA_WIDTH), BF16), jax.ShapeDtypeStruct((n, MLA_WIDTH), BF16)],
        compiler_params=_cparams(("arbitrary",)),
        name="prompt_attention",
    )(naq, nak, nav, q, ckv, krp, w_kvb)


def _na_kernel(q_ref, k_ref, v_ref, kc_ref, vc_ref, bias_ref, o_ref, *, rows):
    rt = pl.program_id(2)
    ws = jnp.clip(rt * NA_Q_ROWS - NA_KR // 2, 0, rows - NA_WIN_ROWS)
    start = pl.multiple_of(ws * GRID_W, GRID_W)
    nwin = NA_WIN_ROWS * GRID_W
    q = q_ref[...]
    s_loc = _dot_nt(q, k_ref[pl.ds(start, nwin), :]) * NA_SCALE + bias_ref[0, 0]
    s_ctx = _dot_nt(q, kc_ref[0].astype(BF16)) * NA_SCALE
    m = jnp.maximum(jnp.max(s_loc, axis=-1, keepdims=True), jnp.max(s_ctx, axis=-1, keepdims=True))
    p_loc = jnp.exp(s_loc - m)
    p_ctx = jnp.exp(s_ctx - m)
    l = jnp.sum(p_loc, axis=-1, keepdims=True) + jnp.sum(p_ctx, axis=-1, keepdims=True)
    o = _dot(p_loc.astype(BF16), v_ref[pl.ds(start, nwin), :]) + _dot(p_ctx.astype(BF16), vc_ref[0].astype(BF16))
    o_ref[...] = (o / l).astype(BF16)


def _na_bias_table(rpb, rows):
    a = np.arange(NA_Q_ROWS)
    b = np.arange(NA_WIN_ROWS)
    col = np.arange(GRID_W)
    cs = np.clip(col - NA_KC // 2, 0, GRID_W - NA_KC)
    valid_col = (col[None, :] >= cs[:, None]) & (col[None, :] < cs[:, None] + NA_KC)
    dc = np.clip(col[None, :] - col[:, None] + NA_KC - 1, 0, 2 * NA_KC - 2)
    tabs = []
    for r0 in (0, NA_Q_ROWS, rows - NA_Q_ROWS):
        ws = int(np.clip(r0 - NA_KR // 2, 0, rows - NA_WIN_ROWS))
        r = r0 + a
        rs = np.clip(r - NA_KR // 2, 0, rows - NA_KR)
        kr = ws + b
        valid_row = (kr[None, :] >= rs[:, None]) & (kr[None, :] < rs[:, None] + NA_KR)
        dr = np.clip(kr[None, :] - r[:, None] + NA_KR - 1, 0, 2 * NA_KR - 2)
        valid = valid_row[:, None, :, None] & valid_col[None, :, None, :]
        dri = np.broadcast_to(dr[:, None, :, None], valid.shape)
        dci = np.broadcast_to(dc[None, :, None, :], valid.shape)
        t = jnp.where(valid[None], rpb[:, dri, dci], NEG_BIG)
        tabs.append(t.reshape(rpb.shape[0], NA_Q_ROWS * GRID_W, NA_WIN_ROWS * GRID_W))
    return jnp.stack(tabs)


def _sample_na(naq, nak, nav, kc, vc, bias, batch, seq):
    rows = seq // GRID_W
    tq = NA_Q_ROWS * GRID_W
    nt = seq // tq
    last = nt - 1

    def pat(rt):
        return jnp.where(rt == 0, 0, jnp.where(rt == last, 2, 1))

    return pl.pallas_call(
        functools.partial(_na_kernel, rows=rows),
        grid=(batch, NA_HEADS, nt),
        in_specs=[pl.BlockSpec((tq, NA_HEAD_DIM), lambda b, h, r: (b * nt + r, h)),
                  pl.BlockSpec((seq, NA_HEAD_DIM), lambda b, h, r: (b, h)),
                  pl.BlockSpec((seq, NA_HEAD_DIM), lambda b, h, r: (b, h)),
                  pl.BlockSpec((1, kc.shape[1], NA_HEAD_DIM), lambda b, h, r: (b, 0, h)),
                  pl.BlockSpec((1, vc.shape[1], NA_HEAD_DIM), lambda b, h, r: (b, 0, h)),
                  pl.BlockSpec((1, 1, tq, NA_WIN_ROWS * GRID_W), lambda b, h, r: (pat(r), h, 0, 0))],
        out_specs=pl.BlockSpec((tq, NA_HEAD_DIM), lambda b, h, r: (b * nt + r, h)),
        out_shape=jax.ShapeDtypeStruct((batch * seq, NA_WIDTH), BF16),
        compiler_params=_cparams(("arbitrary", "arbitrary", "arbitrary")),
        name="sample_neighbourhood_attention",
    )(naq, nak, nav, kc, vc, bias)


def _kv_expand_kernel(ckv_ref, krp_ref, w_ref, kf_ref, v_ref):
    kv = _dot(ckv_ref[0], w_ref[...])
    krp = krp_ref[0]
    kvw = QK_NOPE + V_DIM
    for hd in range(MLA_HEADS):
        kf_ref[0, hd, :, 0:QK_NOPE] = kv[:, hd * kvw:hd * kvw + QK_NOPE].astype(BF16)
        kf_ref[0, hd, :, QK_NOPE:Q_PAD] = krp
        v_ref[0, hd] = kv[:, hd * kvw + QK_NOPE:(hd + 1) * kvw].astype(BF16)


def _kv_expand(ckv, krp, w_kvb):
    batch, nkeys, _ = ckv.shape
    tm = 512
    return pl.pallas_call(
        _kv_expand_kernel,
        grid=(batch, nkeys // tm),
        in_specs=[pl.BlockSpec((1, tm, KV_LORA), lambda b, t: (b, t, 0)),
                  pl.BlockSpec((1, tm, LANE), lambda b, t: (b, t, 0)),
                  _const_spec(w_kvb.shape)],
        out_specs=[pl.BlockSpec((1, MLA_HEADS, tm, Q_PAD), lambda b, t: (b, 0, t, 0)),
                   pl.BlockSpec((1, MLA_HEADS, tm, V_DIM), lambda b, t: (b, 0, t, 0))],
        out_shape=[jax.ShapeDtypeStruct((batch, MLA_HEADS, nkeys, Q_PAD), BF16),
                   jax.ShapeDtypeStruct((batch, MLA_HEADS, nkeys, V_DIM), BF16)],
        compiler_params=_cparams(("arbitrary", "arbitrary")),
        name="latent_kv_expand",
    )(ckv, krp, w_kvb)


def _mla_kernel(q_ref, kf_ref, v_ref, o_ref, *, nkeys):
    q = q_ref[...]
    tq = q.shape[0]
    m = jnp.full((tq, 1), NEG_BIG, F32)
    l = jnp.zeros((tq, 1), F32)
    acc = jnp.zeros((tq, V_DIM), F32)
    for c in range(nkeys // MLA_TK):
        ks = slice(c * MLA_TK, (c + 1) * MLA_TK)
        s = _dot_nt(q, kf_ref[0, 0, ks, :]) * MLA_SCALE
        m_new = jnp.maximum(m, jnp.max(s, axis=-1, keepdims=True))
        alpha = jnp.exp(m - m_new)
        p = jnp.exp(s - m_new)
        l = alpha * l + jnp.sum(p, axis=-1, keepdims=True)
        acc = alpha * acc + _dot(p.astype(BF16), v_ref[0, 0, ks, :])
        m = m_new
    o_ref[...] = (acc / l).astype(BF16)


def _sample_mla(q, kf, v, batch, seq):
    nkeys = kf.shape[2]
    nt = seq // MLA_TQ
    return pl.pallas_call(
        functools.partial(_mla_kernel, nkeys=nkeys),
        grid=(batch, MLA_HEADS, nt),
        in_specs=[pl.BlockSpec((MLA_TQ, Q_PAD), lambda b, h, t: (b * nt + t, h)),
                  pl.BlockSpec((1, 1, nkeys, Q_PAD), lambda b, h, t: (b, h, 0, 0)),
                  pl.BlockSpec((1, 1, nkeys, V_DIM), lambda b, h, t: (b, h, 0, 0))],
        out_specs=pl.BlockSpec((MLA_TQ, V_DIM), lambda b, h, t: (b * nt + t, h)),
        out_shape=jax.ShapeDtypeStruct((batch * seq, MLA_WIDTH), BF16),
        compiler_params=_cparams(("arbitrary", "arbitrary", "arbitrary")),
        name="sample_latent_attention",
    )(q, kf, v)


def _post_kernel(xp_ref, xs_ref, onp_ref, omp_ref, ons_ref, oms_ref, mod_ref, wout_ref, gffn_ref, wr_ref, br_ref,
                 x1_ref, h2p_ref, idx_ref, gate_ref, *, n_prompt_tiles):
    is_prompt = pl.program_id(0) < n_prompt_tiles
    x = jnp.where(is_prompt, xp_ref[...], xs_ref[...])
    ona = jnp.where(is_prompt, onp_ref[...], ons_ref[...])
    omla = jnp.where(is_prompt, omp_ref[...], oms_ref[...])
    o = _dot(ona, wout_ref[0:NA_WIDTH, :]) + _dot(omla, wout_ref[NA_WIDTH:NA_WIDTH + MLA_WIDTH, :])
    ga = mod_ref[0, 2:3, :]
    sf = mod_ref[0, 3:4, :]
    scf = mod_ref[0, 4:5, :]
    x1 = x + ga * o
    x1_ref[...] = x1
    h2 = _rms(x1, gffn_ref[...]) * (1.0 + scf) + sf
    logits = jnp.dot(h2, wr_ref[...], precision=lax.Precision.HIGHEST, preferred_element_type=F32) + br_ref[...]
    tm = x.shape[0]
    lane_e = lax.broadcasted_iota(jnp.int32, logits.shape, 1).astype(F32)
    lane_o = lax.broadcasted_iota(jnp.int32, (tm, LANE), 1)
    idx_out = jnp.zeros((tm, LANE), F32)
    gate_out = jnp.zeros((tm, LANE), F32)
    top0 = None
    denom = jnp.zeros((tm, 1), F32)
    cur = logits
    for k in range(TOP_K):
        mx = jnp.max(cur, axis=-1, keepdims=True)
        ix = jnp.min(jnp.where(cur == mx, lane_e, float(N_EXPERTS)), axis=-1, keepdims=True)
        cur = jnp.where(lane_e == ix, -jnp.inf, cur)
        if k == 0:
            top0 = mx
        e = jnp.exp(mx - top0)
        denom = denom + e
        idx_out = jnp.where(lane_o == k, ix, idx_out)
        gate_out = jnp.where(lane_o == k, e, gate_out)
    idx_ref[...] = idx_out.astype(jnp.int32)
    gate_ref[...] = gate_out / denom
    bits = pltpu.bitcast(h2.astype(BF16).astype(F32), U32)
    for s in range(PACK_ROWS):
        lo = bits[:, s * LANE:(s + 1) * LANE] >> 16
        hi = bits[:, (s + PACK_ROWS) * LANE:(s + PACK_ROWS + 1) * LANE] & jnp.uint32(0xFFFF0000)
        h2p_ref[pl.ds(s, tm, stride=PACK_ROWS), :] = hi | lo


def _post_attention(xp, xs, onp, omp, ons, oms, mod, w_out, g_ffn, w_router, b_router, *, mod_row):
    tm = ROW_TILE
    npt = xp.shape[0] // tm
    nst = xs.shape[0] // tm
    n = xp.shape[0] + xs.shape[0]
    pidx = lambda i: (jnp.minimum(i, npt - 1), 0)
    sidx = lambda i: (jnp.maximum(i - npt, 0), 0)
    row = lambda i: (i, 0)
    return pl.pallas_call(
        functools.partial(_post_kernel, n_prompt_tiles=npt),
        grid=(npt + nst,),
        in_specs=[pl.BlockSpec((tm, D_MODEL), pidx), pl.BlockSpec((tm, D_MODEL), sidx),
                  pl.BlockSpec((tm, NA_WIDTH), pidx), pl.BlockSpec((tm, MLA_WIDTH), pidx),
                  pl.BlockSpec((tm, NA_WIDTH), sidx), pl.BlockSpec((tm, MLA_WIDTH), sidx),
                  pl.BlockSpec((1, 6, D_MODEL), lambda i: (mod_row(i), 0, 0)),
                  _const_spec(w_out.shape), _const_spec((1, D_MODEL)),
                  _const_spec(w_router.shape), _const_spec((1, N_EXPERTS))],
        out_specs=[pl.BlockSpec((tm, D_MODEL), row),
                   pl.BlockSpec((tm * PACK_ROWS, LANE), row),
                   pl.BlockSpec((tm, LANE), row),
                   pl.BlockSpec((tm, LANE), row)],
        out_shape=[jax.ShapeDtypeStruct((n, D_MODEL), F32),
                   jax.ShapeDtypeStruct((n * PACK_ROWS, LANE), U32),
                   jax.ShapeDtypeStruct((n, LANE), jnp.int32),
                   jax.ShapeDtypeStruct((n, LANE), F32)],
        compiler_params=_cparams(("arbitrary",)),
        name="post_attention_router",
    )(xp, xs, onp, omp, ons, oms, mod, w_out, g_ffn, w_router, b_router)


def _deinterleave_kernel(w_ref, p_ref, o_ref):
    p = p_ref[...]
    grp = p.shape[0]
    for c in range(w_ref.shape[2] // grp):
        w = w_ref[0, :, c * grp:(c + 1) * grp].astype(BF16)
        o_ref[0, :, c * grp:(c + 1) * grp] = _dot(w, p).astype(BF16)


def _deinterleave_gate_up(w_gate_up):
    ne, d, n2 = w_gate_up.shape
    grp = 2 * LANE
    dst = np.arange(grp)
    src = np.where(dst < LANE, 2 * dst, 2 * (dst - LANE) + 1)
    perm = np.zeros((grp, grp), np.float32)
    perm[src, dst] = 1.0
    tk = 512
    return pl.pallas_call(
        _deinterleave_kernel,
        grid=(ne, d // tk),
        in_specs=[pl.BlockSpec((1, tk, n2), lambda e, k: (e, k, 0)), _const_spec((grp, grp))],
        out_specs=pl.BlockSpec((1, tk, n2), lambda e, k: (e, k, 0)),
        out_shape=jax.ShapeDtypeStruct((ne, d, n2), BF16),
        compiler_params=_cparams(("arbitrary", "arbitrary")),
        name="deinterleave_gate_up",
    )(w_gate_up, jnp.asarray(perm, BF16))


def _dispatch_kernel(act_ref, idx_ref, h_ref, xs_ref, sem, pend_ref):
    i = pl.program_id(0)
    batch_rows = DISPATCH_ROWS * PACK_ROWS

    def wait_batch():
        pltpu.make_async_copy(h_ref.at[pl.ds(0, batch_rows)], xs_ref.at[pl.ds(0, batch_rows)], sem).wait()

    @pl.when(i == 0)
    def _():
        pend_ref[0] = 0

    @pl.when(act_ref[i] > 0)
    def _():
        def body(r, carry):
            tok = idx_ref[0, 0, r]
            src = pl.multiple_of(tok * PACK_ROWS, PACK_ROWS)
            dst = pl.multiple_of((i * DISPATCH_ROWS + r) * PACK_ROWS, PACK_ROWS)
            pltpu.make_async_copy(h_ref.at[pl.ds(src, PACK_ROWS)], xs_ref.at[pl.ds(dst, PACK_ROWS)], sem).start()
            return carry

        lax.fori_loop(0, DISPATCH_ROWS, body, 0)

        @pl.when(pend_ref[0] > 0)
        def _():
            wait_batch()

        pend_ref[0] = 1

    @pl.when(jnp.logical_and(i == pl.num_programs(0) - 1, pend_ref[0] > 0))
    def _():
        wait_batch()
        pend_ref[0] = 0


def _dispatch(blk_active, row_token, h2p, cap):
    nblk = cap // DISPATCH_ROWS
    return pl.pallas_call(
        _dispatch_kernel,
        grid_spec=pltpu.PrefetchScalarGridSpec(
            num_scalar_prefetch=1,
            grid=(nblk,),
            in_specs=[pl.BlockSpec((1, 1, DISPATCH_ROWS), lambda i, a: (i, 0, 0), memory_space=pltpu.SMEM),
                      pl.BlockSpec(memory_space=pl.ANY)],
            out_specs=pl.BlockSpec(memory_space=pl.ANY),
            scratch_shapes=[pltpu.SemaphoreType.DMA(()), pltpu.SMEM((1,), jnp.int32)]),
        out_shape=jax.ShapeDtypeStruct((cap * PACK_ROWS, LANE), U32),
        compiler_params=_cparams(("arbitrary",)),
        name="moe_dispatch",
    )(blk_active, row_token.reshape(nblk, 1, DISPATCH_ROWS), h2p)


def _moe_kernel(ce_ref, nv_ref, xb_ref, ob_ref, x_ref, wgu_ref, bgu_ref, wd_ref, bd_ref, out_ref, xbf_ref):
    c = pl.program_id(0)
    j = pl.program_id(1)
    nv = nv_ref[c]
    nsub = MOE_CHUNK // MOE_SUB

    @pl.when(j == 0)
    def _():
        bd = bd_ref[0]
        for s in range(OUT_ROWS):
            out_ref[pl.ds(s, MOE_CHUNK, stride=OUT_ROWS), :] = jnp.broadcast_to(
                bd[:, s * LANE:(s + 1) * LANE], (MOE_CHUNK, LANE))
        for sb in range(nsub):
            @pl.when(sb * MOE_SUB < nv)
            def _():
                for s in range(PACK_ROWS):
                    w = x_ref[pl.ds(sb * MOE_SUB * PACK_ROWS + s, MOE_SUB, stride=PACK_ROWS), :]
                    lo = pltpu.bitcast(w << 16, F32)
                    hi = pltpu.bitcast(w & jnp.uint32(0xFFFF0000), F32)
                    xbf_ref[sb * MOE_SUB:(sb + 1) * MOE_SUB, s * LANE:(s + 1) * LANE] = lo.astype(BF16)
                    xbf_ref[sb * MOE_SUB:(sb + 1) * MOE_SUB,
                            (s + PACK_ROWS) * LANE:(s + PACK_ROWS + 1) * LANE] = hi.astype(BF16)

    wd = wd_ref[0].astype(BF16)
    for sb in range(nsub):
        @pl.when(sb * MOE_SUB < nv)
        def _():
            x = xbf_ref[sb * MOE_SUB:(sb + 1) * MOE_SUB, :]
            gu = _dot(x, wgu_ref[0]) + bgu_ref[0]
            ng = MOE_TF // LANE
            g = jnp.concatenate([gu[:, 2 * b * LANE:(2 * b + 1) * LANE] for b in range(ng)], axis=-1)
            u = jnp.concatenate([gu[:, (2 * b + 1) * LANE:(2 * b + 2) * LANE] for b in range(ng)], axis=-1)
            g = jnp.minimum(g, SWIGLU_LIMIT)
            u = jnp.clip(u, -SWIGLU_LIMIT, SWIGLU_LIMIT)
            act = g * (1.0 / (1.0 + jnp.exp(-(g * SWIGLU_ALPHA)))) * (u + 1.0)
            part = _dot(act.astype(BF16), wd)
            for s in range(OUT_ROWS):
                rows = pl.ds(sb * MOE_SUB * OUT_ROWS + s, MOE_SUB, stride=OUT_ROWS)
                out_ref[rows, :] = out_ref[rows, :] + part[:, s * LANE:(s + 1) * LANE]


def _moe(chunk_expert, chunk_rows, x_blk, out_blk, xs, wgu, bgu, w_down, b_down, nch):
    nj = D_FF // MOE_TF

    def jj(c, j, nv):
        return jnp.where(nv[c] > 0, j, nj - 1)

    return pl.pallas_call(
        _moe_kernel,
        grid_spec=pltpu.PrefetchScalarGridSpec(
            num_scalar_prefetch=4,
            grid=(nch, nj),
            in_specs=[
                pl.BlockSpec((MOE_CHUNK * PACK_ROWS, LANE), lambda c, j, ce, nv, xb, ob: (xb[c], 0)),
                pl.BlockSpec((1, D_MODEL, 2 * MOE_TF), lambda c, j, ce, nv, xb, ob: (ce[c], 0, jj(c, j, nv))),
                pl.BlockSpec((1, 1, 2 * MOE_TF), lambda c, j, ce, nv, xb, ob: (ce[c], 0, jj(c, j, nv))),
                pl.BlockSpec((1, MOE_TF, D_MODEL), lambda c, j, ce, nv, xb, ob: (ce[c], jj(c, j, nv), 0)),
                pl.BlockSpec((1, 1, D_MODEL), lambda c, j, ce, nv, xb, ob: (ce[c], 0, 0))],
            out_specs=pl.BlockSpec((MOE_CHUNK * OUT_ROWS, LANE), lambda c, j, ce, nv, xb, ob: (ob[c], 0)),
            scratch_shapes=[pltpu.VMEM((MOE_CHUNK, D_MODEL), BF16)]),
        out_shape=jax.ShapeDtypeStruct(((nch + 1) * MOE_CHUNK * OUT_ROWS, LANE), F32),
        compiler_params=_cparams(("arbitrary", "arbitrary")),
        name="moe_experts",
    )(chunk_expert, chunk_rows, x_blk, out_blk, xs, wgu, bgu, w_down, b_down)


def _combine_kernel(idx0_ref, idx1_ref, gate_ref, x1_ref, mod_ref, gfin_ref, os_ref, y_ref, buf_ref, sem):
    i = pl.program_id(0)
    n = pl.num_programs(0)
    nrow = COMBINE_TOKENS * TOP_K
    slot_rows = nrow * OUT_ROWS

    def issue(idx_ref, slot):
        def body(r, carry):
            src = pl.multiple_of(idx_ref[0, 0, r] * OUT_ROWS, OUT_ROWS)
            dst = pl.multiple_of(slot * slot_rows + r * OUT_ROWS, OUT_ROWS)
            pltpu.make_async_copy(os_ref.at[pl.ds(src, OUT_ROWS)], buf_ref.at[pl.ds(dst, OUT_ROWS)],
                                  sem.at[slot]).start()
            return carry

        lax.fori_loop(0, nrow, body, 0)

    @pl.when(i == 0)
    def _():
        issue(idx0_ref, 0)

    @pl.when(i + 1 < n)
    def _():
        issue(idx1_ref, (i + 1) % 2)

    slot = i % 2
    base = pl.multiple_of(slot * slot_rows, slot_rows)
    pltpu.make_async_copy(os_ref.at[pl.ds(0, slot_rows)], buf_ref.at[pl.ds(base, slot_rows)], sem.at[slot]).wait()
    gate = gate_ref[...]
    pieces = []
    for s in range(OUT_ROWS):
        acc = None
        for k in range(TOP_K):
            rows = buf_ref[pl.ds(base + k * OUT_ROWS + s, COMBINE_TOKENS, stride=TOP_K * OUT_ROWS), :]
            term = gate[:, k:k + 1] * rows
            acc = term if acc is None else acc + term
        pieces.append(acc)
    y = jnp.concatenate(pieces, axis=-1)
    gf = mod_ref[0, 5:6, :]
    y_ref[...] = _rms(x1_ref[...] + gf * y, gfin_ref[...])


def _combine(dest, gates, x1, mod, g_final, out_sorted, *, tile0, ntiles, mod_row):
    tt = COMBINE_TOKENS
    nrow = tt * TOP_K
    total_tiles = dest.shape[0]
    return pl.pallas_call(
        _combine_kernel,
        grid=(ntiles,),
        in_specs=[pl.BlockSpec((1, 1, nrow), lambda i: (tile0 + i, 0, 0), memory_space=pltpu.SMEM),
                  pl.BlockSpec((1, 1, nrow), lambda i: (jnp.minimum(tile0 + i + 1, total_tiles - 1), 0, 0),
                               memory_space=pltpu.SMEM),
                  pl.BlockSpec((tt, LANE), lambda i: (tile0 + i, 0)),
                  pl.BlockSpec((tt, D_MODEL), lambda i: (tile0 + i, 0)),
                  pl.BlockSpec((1, 6, D_MODEL), lambda i: (mod_row(i), 0, 0)),
                  _const_spec((1, D_MODEL)),
                  pl.BlockSpec(memory_space=pl.ANY)],
        out_specs=pl.BlockSpec((tt, D_MODEL), lambda i: (i, 0)),
        out_shape=jax.ShapeDtypeStruct((ntiles * tt, D_MODEL), F32),
        scratch_shapes=[pltpu.VMEM((2 * nrow * OUT_ROWS, LANE), F32), pltpu.SemaphoreType.DMA((2,))],
        compiler_params=_cparams(("arbitrary",)),
        name="moe_combine_final_norm",
    )(dest, dest, gates, x1, mod, g_final, out_sorted)


def _routing(top_idx, nch):
    n = top_idx.shape[0]
    nk = n * TOP_K
    flat_e = top_idx.reshape(nk)
    order = jnp.argsort(flat_e, stable=True).astype(jnp.int32)
    inv = jnp.argsort(order).astype(jnp.int32)
    counts = jnp.sum((flat_e[:, None] == jnp.arange(N_EXPERTS, dtype=jnp.int32)[None, :]).astype(jnp.int32), axis=0)
    grp_start = jnp.cumsum(counts) - counts
    chunks_e = (counts + MOE_CHUNK - 1) // MOE_CHUNK
    chunk_end = jnp.cumsum(chunks_e)
    chunk_start = chunk_end - chunks_e
    slot_start = chunk_start * MOE_CHUNK
    dest = slot_start[flat_e] + inv - grp_start[flat_e]

    total = chunk_end[-1]
    cidx = jnp.arange(nch, dtype=jnp.int32)
    active = cidx < total
    ce = jnp.minimum(jnp.searchsorted(chunk_end, cidx, side='right'), N_EXPERTS - 1).astype(jnp.int32)
    nv = jnp.clip(counts[ce] - (cidx - chunk_start[ce]) * MOE_CHUNK, 0, MOE_CHUNK)
    nv = jnp.where(active, nv, 0).astype(jnp.int32)
    last = jnp.maximum(total - 1, 0)
    chunk_expert = jnp.where(active, ce, ce[last]).astype(jnp.int32)
    x_blk = jnp.where(active, cidx, last).astype(jnp.int32)
    out_blk = jnp.where(active, cidx, nch).astype(jnp.int32)

    cap = nch * MOE_CHUNK
    slot = jnp.arange(cap, dtype=jnp.int32)
    sc = slot // MOE_CHUNK
    local = slot - slot_start[ce[sc]]
    valid = jnp.logical_and(active[sc], local < counts[ce[sc]])
    src_flat = order[jnp.clip(grp_start[ce[sc]] + local, 0, nk - 1)]
    row_token = jnp.where(valid, src_flat // TOP_K, 0).astype(jnp.int32)
    blk = jnp.arange(cap // DISPATCH_ROWS, dtype=jnp.int32)
    per = MOE_CHUNK // DISPATCH_ROWS
    blk_active = ((blk % per) * DISPATCH_ROWS < nv[blk // per]).astype(jnp.int32)
    return dest.astype(jnp.int32), row_token, blk_active, chunk_expert, nv, x_blk, out_blk


def _rope_tables(t):
    pos = jnp.arange(t)
    rows = (pos // GRID_W).astype(F32)
    cols = (pos % GRID_W).astype(F32)
    inv = ROPE_THETA ** (-(jnp.arange(ROPE_AXIS // 2, dtype=F32) * 2.0 / ROPE_AXIS))
    ar = rows[:, None] * inv
    ac = cols[:, None] * inv
    ang = jnp.concatenate([ar, ar, ac, ac], axis=-1)
    return jnp.cos(ang), jnp.sin(ang)


def _rot_cols(w):
    half = ROPE_AXIS // 2
    src = np.concatenate([np.arange(half, ROPE_AXIS), np.arange(0, half),
                          np.arange(ROPE_AXIS + half, 2 * ROPE_AXIS), np.arange(ROPE_AXIS, ROPE_AXIS + half)])
    sign = np.concatenate([-np.ones(half), np.ones(half), -np.ones(half), np.ones(half)]).astype(np.float32)
    return w[..., src] * sign


def kernel(x_prompt, x_sample, cache_na_k, cache_na_v, cache_mla_ckv, cache_mla_krope, c, c_ctx, g_attn, g_ffn, g_final, w_mod, b_mod, w_in, w_out, na_rpb, g_q_a, w_q_b, g_kv_a, w_kv_b, w_router, b_router, w_gate_up, b_gate_up, w_down, b_down):
    bp, sp, d = x_prompt.shape
    bd, td, _ = x_sample.shape
    assert d == D_MODEL and w_mod.shape[0] == 1, "one trunk layer of width D_MODEL"
    n_p = bp * sp
    n_s = bd * td
    xp = x_prompt.reshape(n_p, d)
    xs = x_sample.reshape(n_s, d)

    c8 = jnp.zeros((8, d), F32).at[0].set(c_ctx).at[1:1 + bd].set(c)
    mod = _modulation(c8, w_mod[0], b_mod[0].reshape(1, -1)).reshape(8, 6, d)

    w_in0 = w_in[0]
    w_kr = w_in0[:, KR_OFF:KR_OFF + QK_ROPE]
    w_in_p = jnp.concatenate([w_in0, jnp.zeros((d, LANE - QK_ROPE), F32)], axis=1).astype(BF16)
    w_in_s = jnp.concatenate([w_in0, _rot_cols(w_kr)], axis=1).astype(BF16)
    wq = w_q_b[0].reshape(Q_LORA, MLA_HEADS, QK_NOPE + QK_ROPE)
    zpad = jnp.zeros((Q_LORA, MLA_HEADS, Q_PAD - QK_NOPE - QK_ROPE), F32)
    wq_pad = jnp.concatenate([wq, zpad], axis=-1).reshape(Q_LORA, MLA_HEADS * Q_PAD)
    wq_rot = jnp.concatenate([jnp.zeros((Q_LORA, MLA_HEADS, QK_NOPE), F32), _rot_cols(wq[..., QK_NOPE:]), zpad],
                             axis=-1).reshape(Q_LORA, MLA_HEADS * Q_PAD)
    wqb_p = wq_pad.astype(BF16)
    wqb_s = jnp.concatenate([wq_pad, wq_rot], axis=1).astype(BF16)
    w_kvb = w_kv_b[0].astype(BF16)
    cos, sin = _rope_tables(td)
    cosq = jnp.concatenate([jnp.ones((td, QK_NOPE), F32), cos, jnp.ones((td, Q_PAD - QK_NOPE - QK_ROPE), F32)], axis=1)
    sinq = jnp.concatenate([jnp.zeros((td, QK_NOPE), F32), sin, jnp.zeros((td, Q_PAD - QK_NOPE - QK_ROPE), F32)], axis=1)
    csk = jnp.concatenate([cos, sin], axis=1)

    tiles_per_seq = td // ROW_TILE
    g_attn2 = g_attn[0].reshape(1, d)
    gq2 = g_q_a[0].reshape(1, Q_LORA)
    gkv2 = g_kv_a[0].reshape(1, KV_LORA)

    naq_p, nak_p, nav_p, q_p, ckv_p, kr_p, krp_p = _pre_attention(
        xp, mod, g_attn2, w_in_p, gq2, wqb_p, gkv2, None, rope=False, mod_row=lambda i: 0)
    ona_p, omla_p = _prompt_attention(naq_p, nak_p, nav_p, q_p, ckv_p, krp_p, w_kvb, sp)

    naq_s, nak_s, nav_s, q_s, ckv_s, krp_s = _pre_attention(
        xs, mod, g_attn2, w_in_s, gq2, wqb_s, gkv2, (cosq, sinq, csk), rope=True,
        mod_row=lambda i: 1 + i // tiles_per_seq)
    past = cache_na_k.shape[2]
    kc = cache_na_k[:, 0].reshape(bd, past, NA_WIDTH)
    vc = cache_na_v[:, 0].reshape(bd, past, NA_WIDTH)
    bias = _na_bias_table(na_rpb[0], td // GRID_W)
    ona_s = _sample_na(naq_s, nak_s, nav_s, kc, vc, bias, bd, td)
    ckv_all = jnp.concatenate([ckv_s.reshape(bd, td, KV_LORA), cache_mla_ckv[:, 0].astype(BF16)], axis=1)
    krp_c = jnp.concatenate([cache_mla_krope[:, 0], jnp.zeros((bd, past, LANE - QK_ROPE), F32)], axis=-1).astype(BF16)
    krp_all = jnp.concatenate([krp_s.reshape(bd, td, LANE), krp_c], axis=1)
    kf, vv = _kv_expand(ckv_all, krp_all, w_kvb)
    omla_s = _sample_mla(q_s, kf, vv, bd, td)

    npt = n_p // ROW_TILE
    x1, h2p, idx128, gate128 = _post_attention(
        xp, xs, ona_p, omla_p, ona_s, omla_s, mod, w_out[0].astype(BF16), g_ffn[0].reshape(1, d),
        w_router[0], b_router[0].reshape(1, N_EXPERTS),
        mod_row=lambda i: jnp.where(i < npt, 0, 1 + jnp.maximum(i - npt, 0) // tiles_per_seq))

    n = n_p + n_s
    nch = n * TOP_K // MOE_CHUNK + N_EXPERTS
    dest, row_token, blk_active, chunk_expert, chunk_rows, x_blk, out_blk = _routing(idx128[:, :TOP_K], nch)
    x_sorted = _dispatch(blk_active, row_token, h2p, nch * MOE_CHUNK)
    wgu = _deinterleave_gate_up(w_gate_up[0])
    ng = 2 * D_FF // (2 * LANE)
    bgu = b_gate_up[0].reshape(N_EXPERTS, ng, LANE, 2).transpose(0, 1, 3, 2).reshape(N_EXPERTS, 1, 2 * D_FF)
    out_sorted = _moe(chunk_expert, chunk_rows, x_blk, out_blk, x_sorted, wgu, bgu, w_down[0],
                      b_down[0].reshape(N_EXPERTS, 1, d), nch)

    tt = COMBINE_TOKENS
    dest3 = dest.reshape(n // tt, 1, tt * TOP_K)
    gfin = g_final.reshape(1, d)
    ctiles_seq = td // tt
    y_p = _combine(dest3, gate128, x1, mod, gfin, out_sorted, tile0=0, ntiles=n_p // tt, mod_row=lambda i: 0)
    y_s = _combine(dest3, gate128, x1, mod, gfin, out_sorted, tile0=n_p // tt, ntiles=n_s // tt,
                   mod_row=lambda i: 1 + i // ctiles_seq)

    return (y_p.reshape(bp, sp, d), y_s.reshape(bd, td, d),
            nak_p.reshape(bp, 1, sp, NA_HEADS, NA_HEAD_DIM), nav_p.reshape(bp, 1, sp, NA_HEADS, NA_HEAD_DIM),
            ckv_p.reshape(bp, 1, sp, KV_LORA), kr_p.reshape(bp, 1, sp, QK_ROPE))
```

```python
import functools

import numpy as np
import jax
import jax.numpy as jnp
from jax import lax
from jax.experimental import pallas as pl
from jax.experimental.pallas import tpu as pltpu

F32 = jnp.float32
BF16 = jnp.bfloat16
U32 = jnp.uint32

D_MODEL = 2048
GRID_W = 64
NA_HEADS = 8
NA_HEAD_DIM = 128
NA_KR = 8
NA_KC = 16
MLA_HEADS = 8
Q_LORA = 512
KV_LORA = 256
QK_NOPE = 128
QK_ROPE = 64
V_DIM = 128
ROPE_AXIS = QK_ROPE // 2
ROPE_THETA = 10000.0
NA_WIDTH = NA_HEADS * NA_HEAD_DIM
MLA_WIDTH = MLA_HEADS * V_DIM
IN_COLS = 3 * NA_WIDTH + Q_LORA + KV_LORA + QK_ROPE
N_EXPERTS = 32
TOP_K = 4
D_FF = D_MODEL
SWIGLU_ALPHA = 1.702
SWIGLU_LIMIT = 7.0
EPS = 1e-6

LANE = 128
Q_PAD = 2 * LANE
KR_OFF = 3 * NA_WIDTH + Q_LORA + KV_LORA
IN_COLS_PAD = KR_OFF + LANE
VMEM_LIMIT = 56 * 1024 * 1024
NEG_BIG = -1e30

NA_SCALE = NA_HEAD_DIM ** -0.5
MLA_SCALE = (QK_NOPE + QK_ROPE) ** -0.5

ROW_TILE = 256
NA_Q_ROWS = 4
NA_WIN_ROWS = 12
MLA_TQ = 512
MLA_TK = 1152
MOE_CHUNK = 1024
MOE_SUB = 256
MOE_TF = 512
DISPATCH_ROWS = 256
COMBINE_TOKENS = 128
DMA_UNROLL = 8
PACK_ROWS = D_MODEL // (2 * LANE)
OUT_ROWS = D_MODEL // LANE


def _cparams(sem):
    return pltpu.CompilerParams(dimension_semantics=sem, vmem_limit_bytes=VMEM_LIMIT)


def _const_spec(shape):
    nd = len(shape)
    return pl.BlockSpec(shape, lambda *a: (0,) * nd, pipeline_mode=pl.Buffered(1))


def _rms(x, g):
    return x * lax.rsqrt(jnp.mean(x * x, axis=-1, keepdims=True) + EPS) * g


def _dot(a, b):
    return jnp.dot(a, b, preferred_element_type=F32)


def _dot_nt(a, b):
    return lax.dot_general(a, b, (((1,), (1,)), ((), ())), preferred_element_type=F32)


def _mod_kernel(c_ref, w_ref, b_ref, o_ref):
    c = c_ref[...]
    s = c / (1.0 + jnp.exp(-c))
    o_ref[...] = _dot(s.astype(BF16), w_ref[...].astype(BF16)) + b_ref[...]


def _modulation(c8, w_mod, b_mod):
    n = w_mod.shape[1]
    tn = 1024
    return pl.pallas_call(
        _mod_kernel,
        grid=(n // tn,),
        in_specs=[pl.BlockSpec((8, D_MODEL), lambda j: (0, 0)),
                  pl.BlockSpec((D_MODEL, tn), lambda j: (0, j)),
                  pl.BlockSpec((1, tn), lambda j: (0, j))],
        out_specs=pl.BlockSpec((8, tn), lambda j: (0, j)),
        out_shape=jax.ShapeDtypeStruct((8, n), F32),
        compiler_params=_cparams(("arbitrary",)),
        name="modulation",
    )(c8, w_mod, b_mod)


def _pre_kernel(*refs, rope):
    if rope:
        (x_ref, mod_ref, g_ref, win_ref, gq_ref, wqb_ref, gkv_ref, cosq_ref, sinq_ref, csk_ref,
         naq_ref, nak_ref, nav_ref, q_ref, ckv_ref, krp_ref) = refs
    else:
        (x_ref, mod_ref, g_ref, win_ref, gq_ref, wqb_ref, gkv_ref,
         naq_ref, nak_ref, nav_ref, q_ref, ckv_ref, kr_ref, krp_ref) = refs
    x = x_ref[...]
    sa = mod_ref[0, 0:1, :]
    sca = mod_ref[0, 1:2, :]
    h = (_rms(x, g_ref[...]) * (1.0 + sca) + sa).astype(BF16)
    proj = _dot(h, win_ref[...])
    naq_ref[...] = proj[:, 0:NA_WIDTH].astype(naq_ref.dtype)
    nak_ref[...] = proj[:, NA_WIDTH:2 * NA_WIDTH].astype(nak_ref.dtype)
    nav_ref[...] = proj[:, 2 * NA_WIDTH:3 * NA_WIDTH].astype(nav_ref.dtype)
    q_a = proj[:, 3 * NA_WIDTH:3 * NA_WIDTH + Q_LORA]
    kv_a = proj[:, 3 * NA_WIDTH + Q_LORA:KR_OFF]
    krx = proj[:, KR_OFF:IN_COLS_PAD]
    qan = _rms(q_a, gq_ref[...]).astype(BF16)
    qq = _dot(qan, wqb_ref[...])
    ckv_ref[...] = _rms(kv_a, gkv_ref[...]).astype(ckv_ref.dtype)
    if rope:
        width = MLA_HEADS * Q_PAD
        cosq = cosq_ref[...]
        sinq = sinq_ref[...]
        for hd in range(MLA_HEADS):
            a = qq[:, hd * Q_PAD:(hd + 1) * Q_PAD]
            b = qq[:, width + hd * Q_PAD:width + (hd + 1) * Q_PAD]
            q_ref[:, hd * Q_PAD:(hd + 1) * Q_PAD] = (a * cosq + b * sinq).astype(BF16)
        y = krx * csk_ref[...]
        y = y + pltpu.roll(y, QK_ROPE, 1)
        lane = lax.broadcasted_iota(jnp.int32, y.shape, 1)
        krp_ref[...] = jnp.where(lane < QK_ROPE, y, 0.0).astype(BF16)
    else:
        q_ref[...] = qq.astype(BF16)
        kr_ref[...] = krx[:, 0:QK_ROPE]
        krp_ref[...] = krx.astype(BF16)


def _pre_attention(x, mod, g_attn, w_in, g_q_a, w_qb, g_kv_a, rope_tabs, *, rope, mod_row):
    n = x.shape[0]
    tm = ROW_TILE
    row = lambda i: (i, 0)
    in_specs = [pl.BlockSpec((tm, D_MODEL), row),
                pl.BlockSpec((1, 6, D_MODEL), lambda i: (mod_row(i), 0, 0)),
                _const_spec((1, D_MODEL)),
                _const_spec(w_in.shape),
                _const_spec((1, Q_LORA)),
                _const_spec(w_qb.shape),
                _const_spec((1, KV_LORA))]
    args = [x, mod, g_attn, w_in, g_q_a, w_qb, g_kv_a]
    qw = MLA_HEADS * Q_PAD
    if rope:
        tiles_per_seq = rope_tabs[0].shape[0] // tm
        pos = lambda i: (i % tiles_per_seq, 0)
        in_specs += [pl.BlockSpec((tm, Q_PAD), pos), pl.BlockSpec((tm, Q_PAD), pos),
                     pl.BlockSpec((tm, LANE), pos)]
        args += list(rope_tabs)
        out_shape = [jax.ShapeDtypeStruct((n, NA_WIDTH), BF16)] * 3 + [
            jax.ShapeDtypeStruct((n, qw), BF16),
            jax.ShapeDtypeStruct((n, KV_LORA), BF16),
            jax.ShapeDtypeStruct((n, LANE), BF16)]
        out_specs = [pl.BlockSpec((tm, NA_WIDTH), row)] * 3 + [
            pl.BlockSpec((tm, qw), row), pl.BlockSpec((tm, KV_LORA), row), pl.BlockSpec((tm, LANE), row)]
    else:
        out_shape = [jax.ShapeDtypeStruct((n, NA_WIDTH), BF16),
                     jax.ShapeDtypeStruct((n, NA_WIDTH), F32),
                     jax.ShapeDtypeStruct((n, NA_WIDTH), F32),
                     jax.ShapeDtypeStruct((n, qw), BF16),
                     jax.ShapeDtypeStruct((n, KV_LORA), F32),
                     jax.ShapeDtypeStruct((n, QK_ROPE), F32),
                     jax.ShapeDtypeStruct((n, LANE), BF16)]
        out_specs = [pl.BlockSpec((tm, NA_WIDTH), row)] * 3 + [
            pl.BlockSpec((tm, qw), row), pl.BlockSpec((tm, KV_LORA), row),
            pl.BlockSpec((tm, QK_ROPE), row), pl.BlockSpec((tm, LANE), row)]
    return pl.pallas_call(
        functools.partial(_pre_kernel, rope=rope),
        grid=(n // tm,),
        in_specs=in_specs,
        out_specs=out_specs,
        out_shape=out_shape,
        compiler_params=_cparams(("arbitrary",)),
        name="pre_attention_rope" if rope else "pre_attention",
    )(*args)


def _softmax_pv(s, v):
    m = jnp.max(s, axis=-1, keepdims=True)
    p = jnp.exp(s - m)
    l = jnp.sum(p, axis=-1, keepdims=True)
    return _dot(p.astype(BF16), v) / l


def _prompt_attn_kernel(naq_ref, nak_ref, nav_ref, q_ref, ckv_ref, krp_ref, wkvb_ref, ona_ref, omla_ref):
    kv = _dot(ckv_ref[...].astype(BF16), wkvb_ref[...])
    krp = krp_ref[...]
    for hd in range(NA_HEADS):
        sl = slice(hd * NA_HEAD_DIM, (hd + 1) * NA_HEAD_DIM)
        s = _dot_nt(naq_ref[:, sl], nak_ref[:, sl].astype(BF16)) * NA_SCALE
        ona_ref[:, sl] = _softmax_pv(s, nav_ref[:, sl].astype(BF16)).astype(BF16)
    kvw = QK_NOPE + V_DIM
    for hd in range(MLA_HEADS):
        kf = jnp.concatenate([kv[:, hd * kvw:hd * kvw + QK_NOPE].astype(BF16), krp], axis=-1)
        s = _dot_nt(q_ref[:, hd * Q_PAD:(hd + 1) * Q_PAD], kf) * MLA_SCALE
        v = kv[:, hd * kvw + QK_NOPE:(hd + 1) * kvw].astype(BF16)
        omla_ref[:, hd * V_DIM:(hd + 1) * V_DIM] = _softmax_pv(s, v).astype(BF16)


def _prompt_attention(naq, nak, nav, q, ckv, krp, w_kvb, seq):
    n = naq.shape[0]
    row = lambda b: (b, 0)
    return pl.pallas_call(
        _prompt_attn_kernel,
        grid=(n // seq,),
        in_specs=[pl.BlockSpec((seq, NA_WIDTH), row)] * 3 + [
            pl.BlockSpec((seq, MLA_HEADS * Q_PAD), row),
            pl.BlockSpec((seq, KV_LORA), row),
            pl.BlockSpec((seq, LANE), row),
            _const_spec(w_kvb.shape)],
        out_specs=[pl.BlockSpec((seq, NA_WIDTH), row), pl.BlockSpec((seq, MLA_WIDTH), row)],
        out_shape=[jax.ShapeDtypeStruct((n, NA_WIDTH), BF16), jax.ShapeDtypeStruct((n, MLA_WIDTH), BF16)],
        compiler_params=_cparams(("arbitrary",)),
        name="prompt_attention",
    )(naq, nak, nav, q, ckv, krp, w_kvb)


def _na_kernel(q_ref, k_ref, v_ref, kc_ref, vc_ref, bias_ref, o_ref, *, rows):
    rt = pl.program_id(2)
    ws = jnp.clip(rt * NA_Q_ROWS - NA_KR // 2, 0, rows - NA_WIN_ROWS)
    start = pl.multiple_of(ws * GRID_W, GRID_W)
    nwin = NA_WIN_ROWS * GRID_W
    q = q_ref[...]
    s_loc = _dot_nt(q, k_ref[pl.ds(start, nwin), :]) * NA_SCALE + bias_ref[0, 0]
    s_ctx = _dot_nt(q, kc_ref[0].astype(BF16)) * NA_SCALE
    m = jnp.maximum(jnp.max(s_loc, axis=-1, keepdims=True), jnp.max(s_ctx, axis=-1, keepdims=True))
    p_loc = jnp.exp(s_loc - m)
    p_ctx = jnp.exp(s_ctx - m)
    l = jnp.sum(p_loc, axis=-1, keepdims=True) + jnp.sum(p_ctx, axis=-1, keepdims=True)
    o = _dot(p_loc.astype(BF16), v_ref[pl.ds(start, nwin), :]) + _dot(p_ctx.astype(BF16), vc_ref[0].astype(BF16))
    o_ref[...] = (o / l).astype(BF16)


def _na_bias_table(rpb, rows):
    nh = rpb.shape[0]
    a = np.arange(NA_Q_ROWS)
    b = np.arange(NA_WIN_ROWS)
    col = np.arange(GRID_W)
    cs = np.clip(col - NA_KC // 2, 0, GRID_W - NA_KC)
    valid_col = (col[None, :] >= cs[:, None]) & (col[None, :] < cs[:, None] + NA_KC)
    padw = GRID_W - NA_KC
    rp = jnp.pad(rpb, ((0, 0), (0, 0), (padw, padw)))
    toep = jnp.stack([rp[:, :, GRID_W - 1 - qc:2 * GRID_W - 1 - qc] for qc in range(GRID_W)], axis=2)
    toep = jnp.where(valid_col[None, None], toep, NEG_BIG)
    masked = jnp.full((nh, GRID_W, GRID_W), NEG_BIG, F32)
    tabs = []
    for r0 in (0, NA_Q_ROWS, rows - NA_Q_ROWS):
        ws = int(np.clip(r0 - NA_KR // 2, 0, rows - NA_WIN_ROWS))
        r = r0 + a
        rs = np.clip(r - NA_KR // 2, 0, rows - NA_KR)
        kr = ws + b
        valid_row = (kr[None, :] >= rs[:, None]) & (kr[None, :] < rs[:, None] + NA_KR)
        dr = kr[None, :] - r[:, None] + NA_KR - 1
        tile_rows = []
        for ai in range(NA_Q_ROWS):
            blocks = [toep[:, int(dr[ai, bi])] if valid_row[ai, bi] else masked for bi in range(NA_WIN_ROWS)]
            tile_rows.append(jnp.concatenate(blocks, axis=-1))
        tabs.append(jnp.concatenate(tile_rows, axis=1))
    return jnp.stack(tabs)


def _sample_na(naq, nak, nav, kc, vc, bias, batch, seq):
    rows = seq // GRID_W
    tq = NA_Q_ROWS * GRID_W
    nt = seq // tq
    last = nt - 1

    def pat(rt):
        return jnp.where(rt == 0, 0, jnp.where(rt == last, 2, 1))

    return pl.pallas_call(
        functools.partial(_na_kernel, rows=rows),
        grid=(batch, NA_HEADS, nt),
        in_specs=[pl.BlockSpec((tq, NA_HEAD_DIM), lambda b, h, r: (b * nt + r, h)),
                  pl.BlockSpec((seq, NA_HEAD_DIM), lambda b, h, r: (b, h)),
                  pl.BlockSpec((seq, NA_HEAD_DIM), lambda b, h, r: (b, h)),
                  pl.BlockSpec((1, kc.shape[1], NA_HEAD_DIM), lambda b, h, r: (b, 0, h)),
                  pl.BlockSpec((1, vc.shape[1], NA_HEAD_DIM), lambda b, h, r: (b, 0, h)),
                  pl.BlockSpec((1, 1, tq, NA_WIN_ROWS * GRID_W), lambda b, h, r: (pat(r), h, 0, 0))],
        out_specs=pl.BlockSpec((tq, NA_HEAD_DIM), lambda b, h, r: (b * nt + r, h)),
        out_shape=jax.ShapeDtypeStruct((batch * seq, NA_WIDTH), BF16),
        compiler_params=_cparams(("arbitrary", "arbitrary", "arbitrary")),
        name="sample_neighbourhood_attention",
    )(naq, nak, nav, kc, vc, bias)


def _kv_expand_kernel(ckv_ref, krp_ref, w_ref, kf_ref, v_ref):
    kv = _dot(ckv_ref[0], w_ref[...])
    krp = krp_ref[0]
    kvw = QK_NOPE + V_DIM
    for hd in range(MLA_HEADS):
        kf_ref[0, hd, :, 0:QK_NOPE] = kv[:, hd * kvw:hd * kvw + QK_NOPE].astype(BF16)
        kf_ref[0, hd, :, QK_NOPE:Q_PAD] = krp
        v_ref[0, hd] = kv[:, hd * kvw + QK_NOPE:(hd + 1) * kvw].astype(BF16)


def _kv_expand(ckv, krp, w_kvb):
    batch, nkeys, _ = ckv.shape
    tm = 512
    return pl.pallas_call(
        _kv_expand_kernel,
        grid=(batch, nkeys // tm),
        in_specs=[pl.BlockSpec((1, tm, KV_LORA), lambda b, t: (b, t, 0)),
                  pl.BlockSpec((1, tm, LANE), lambda b, t: (b, t, 0)),
                  _const_spec(w_kvb.shape)],
        out_specs=[pl.BlockSpec((1, MLA_HEADS, tm, Q_PAD), lambda b, t: (b, 0, t, 0)),
                   pl.BlockSpec((1, MLA_HEADS, tm, V_DIM), lambda b, t: (b, 0, t, 0))],
        out_shape=[jax.ShapeDtypeStruct((batch, MLA_HEADS, nkeys, Q_PAD), BF16),
                   jax.ShapeDtypeStruct((batch, MLA_HEADS, nkeys, V_DIM), BF16)],
        compiler_params=_cparams(("arbitrary", "arbitrary")),
        name="latent_kv_expand",
    )(ckv, krp, w_kvb)


def _mla_kernel(q_ref, kf_ref, v_ref, o_ref, *, nkeys):
    q = q_ref[...]
    tq = q.shape[0]
    m = jnp.full((tq, 1), NEG_BIG, F32)
    l = jnp.zeros((tq, 1), F32)
    acc = jnp.zeros((tq, V_DIM), F32)
    for c in range(nkeys // MLA_TK):
        ks = slice(c * MLA_TK, (c + 1) * MLA_TK)
        s = _dot_nt(q, kf_ref[0, 0, ks, :]) * MLA_SCALE
        m_new = jnp.maximum(m, jnp.max(s, axis=-1, keepdims=True))
        alpha = jnp.exp(m - m_new)
        p = jnp.exp(s - m_new)
        l = alpha * l + jnp.sum(p, axis=-1, keepdims=True)
        acc = alpha * acc + _dot(p.astype(BF16), v_ref[0, 0, ks, :])
        m = m_new
    o_ref[...] = (acc / l).astype(BF16)


def _sample_mla(q, kf, v, batch, seq):
    nkeys = kf.shape[2]
    nt = seq // MLA_TQ
    return pl.pallas_call(
        functools.partial(_mla_kernel, nkeys=nkeys),
        grid=(batch, MLA_HEADS, nt),
        in_specs=[pl.BlockSpec((MLA_TQ, Q_PAD), lambda b, h, t: (b * nt + t, h)),
                  pl.BlockSpec((1, 1, nkeys, Q_PAD), lambda b, h, t: (b, h, 0, 0)),
                  pl.BlockSpec((1, 1, nkeys, V_DIM), lambda b, h, t: (b, h, 0, 0))],
        out_specs=pl.BlockSpec((MLA_TQ, V_DIM), lambda b, h, t: (b * nt + t, h)),
        out_shape=jax.ShapeDtypeStruct((batch * seq, MLA_WIDTH), BF16),
        compiler_params=_cparams(("arbitrary", "arbitrary", "arbitrary")),
        name="sample_latent_attention",
    )(q, kf, v)


def _post_kernel(xp_ref, xs_ref, onp_ref, omp_ref, ons_ref, oms_ref, mod_ref, wout_ref, gffn_ref, wr_ref, br_ref,
                 x1_ref, h2p_ref, idx_ref, gate_ref, *, n_prompt_tiles):
    is_prompt = pl.program_id(0) < n_prompt_tiles
    x = jnp.where(is_prompt, xp_ref[...], xs_ref[...])
    ona = jnp.where(is_prompt, onp_ref[...], ons_ref[...])
    omla = jnp.where(is_prompt, omp_ref[...], oms_ref[...])
    o = _dot(ona, wout_ref[0:NA_WIDTH, :]) + _dot(omla, wout_ref[NA_WIDTH:NA_WIDTH + MLA_WIDTH, :])
    ga = mod_ref[0, 2:3, :]
    sf = mod_ref[0, 3:4, :]
    scf = mod_ref[0, 4:5, :]
    x1 = x + ga * o
    x1_ref[...] = x1
    h2 = _rms(x1, gffn_ref[...]) * (1.0 + scf) + sf
    logits = jnp.dot(h2, wr_ref[...], precision=lax.Precision.HIGHEST, preferred_element_type=F32) + br_ref[...]
    tm = x.shape[0]
    lane_e = lax.broadcasted_iota(jnp.int32, logits.shape, 1).astype(F32)
    lane_o = lax.broadcasted_iota(jnp.int32, (tm, LANE), 1)
    idx_out = jnp.zeros((tm, LANE), F32)
    gate_out = jnp.zeros((tm, LANE), F32)
    top0 = None
    denom = jnp.zeros((tm, 1), F32)
    cur = logits
    for k in range(TOP_K):
        mx = jnp.max(cur, axis=-1, keepdims=True)
        ix = jnp.min(jnp.where(cur == mx, lane_e, float(N_EXPERTS)), axis=-1, keepdims=True)
        cur = jnp.where(lane_e == ix, -jnp.inf, cur)
        if k == 0:
            top0 = mx
        e = jnp.exp(mx - top0)
        denom = denom + e
        idx_out = jnp.where(lane_o == k, ix, idx_out)
        gate_out = jnp.where(lane_o == k, e, gate_out)
    idx_ref[...] = idx_out.astype(jnp.int32)
    gate_ref[...] = gate_out / denom
    bits = pltpu.bitcast(h2.astype(BF16).astype(F32), U32)
    for s in range(PACK_ROWS):
        lo = bits[:, s * LANE:(s + 1) * LANE] >> 16
        hi = bits[:, (s + PACK_ROWS) * LANE:(s + PACK_ROWS + 1) * LANE] & jnp.uint32(0xFFFF0000)
        h2p_ref[pl.ds(s, tm, stride=PACK_ROWS), :] = hi | lo


def _post_attention(xp, xs, onp, omp, ons, oms, mod, w_out, g_ffn, w_router, b_router, *, mod_row):
    tm = ROW_TILE
    npt = xp.shape[0] // tm
    nst = xs.shape[0] // tm
    n = xp.shape[0] + xs.shape[0]
    pidx = lambda i: (jnp.minimum(i, npt - 1), 0)
    sidx = lambda i: (jnp.maximum(i - npt, 0), 0)
    row = lambda i: (i, 0)
    return pl.pallas_call(
        functools.partial(_post_kernel, n_prompt_tiles=npt),
        grid=(npt + nst,),
        in_specs=[pl.BlockSpec((tm, D_MODEL), pidx), pl.BlockSpec((tm, D_MODEL), sidx),
                  pl.BlockSpec((tm, NA_WIDTH), pidx), pl.BlockSpec((tm, MLA_WIDTH), pidx),
                  pl.BlockSpec((tm, NA_WIDTH), sidx), pl.BlockSpec((tm, MLA_WIDTH), sidx),
                  pl.BlockSpec((1, 6, D_MODEL), lambda i: (mod_row(i), 0, 0)),
                  _const_spec(w_out.shape), _const_spec((1, D_MODEL)),
                  _const_spec(w_router.shape), _const_spec((1, N_EXPERTS))],
        out_specs=[pl.BlockSpec((tm, D_MODEL), row),
                   pl.BlockSpec((tm * PACK_ROWS, LANE), row),
                   pl.BlockSpec((tm, LANE), row),
                   pl.BlockSpec((tm, LANE), row)],
        out_shape=[jax.ShapeDtypeStruct((n, D_MODEL), F32),
                   jax.ShapeDtypeStruct((n * PACK_ROWS, LANE), U32),
                   jax.ShapeDtypeStruct((n, LANE), jnp.int32),
                   jax.ShapeDtypeStruct((n, LANE), F32)],
        compiler_params=_cparams(("arbitrary",)),
        name="post_attention_router",
    )(xp, xs, onp, omp, ons, oms, mod, w_out, g_ffn, w_router, b_router)


def _deinterleave_kernel(w_ref, p_ref, o_ref):
    p = p_ref[...]
    grp = p.shape[0]
    for c in range(w_ref.shape[2] // grp):
        w = w_ref[0, :, c * grp:(c + 1) * grp].astype(BF16)
        o_ref[0, :, c * grp:(c + 1) * grp] = _dot(w, p).astype(BF16)


def _deinterleave_gate_up(w_gate_up):
    ne, d, n2 = w_gate_up.shape
    grp = 2 * LANE
    dst = np.arange(grp)
    src = np.where(dst < LANE, 2 * dst, 2 * (dst - LANE) + 1)
    perm = np.zeros((grp, grp), np.float32)
    perm[src, dst] = 1.0
    tk = 512
    return pl.pallas_call(
        _deinterleave_kernel,
        grid=(ne, d // tk),
        in_specs=[pl.BlockSpec((1, tk, n2), lambda e, k: (e, k, 0)), _const_spec((grp, grp))],
        out_specs=pl.BlockSpec((1, tk, n2), lambda e, k: (e, k, 0)),
        out_shape=jax.ShapeDtypeStruct((ne, d, n2), BF16),
        compiler_params=_cparams(("arbitrary", "arbitrary")),
        name="deinterleave_gate_up",
    )(w_gate_up, jnp.asarray(perm, BF16))


def _dispatch_kernel(act_ref, idx0_ref, idx1_ref, h_ref, o_ref, buf_ref, sem):
    i = pl.program_id(0)
    n = pl.num_programs(0)
    blk_rows = DISPATCH_ROWS * PACK_ROWS

    def issue(idx_ref, slot):
        def body(r0, carry):
            for u in range(DMA_UNROLL):
                r = r0 * DMA_UNROLL + u
                src = pl.multiple_of(idx_ref[0, 0, r] * PACK_ROWS, PACK_ROWS)
                dst = pl.multiple_of(slot * blk_rows + r * PACK_ROWS, PACK_ROWS)
                pltpu.make_async_copy(h_ref.at[pl.ds(src, PACK_ROWS)], buf_ref.at[pl.ds(dst, PACK_ROWS)],
                                      sem.at[slot]).start()
            return carry

        lax.fori_loop(0, DISPATCH_ROWS // DMA_UNROLL, body, 0)

    @pl.when(jnp.logical_and(i == 0, act_ref[0] > 0))
    def _():
        issue(idx0_ref, 0)

    nxt = jnp.minimum(i + 1, n - 1)

    @pl.when(jnp.logical_and(i + 1 < n, act_ref[nxt] > 0))
    def _():
        issue(idx1_ref, (i + 1) % 2)

    @pl.when(act_ref[i] > 0)
    def _():
        slot = i % 2
        base = pl.multiple_of(slot * blk_rows, blk_rows)
        pltpu.make_async_copy(h_ref.at[pl.ds(0, blk_rows)], buf_ref.at[pl.ds(base, blk_rows)], sem.at[slot]).wait()
        o_ref[...] = buf_ref[pl.ds(base, blk_rows), :]


def _dispatch(blk_active, row_token, h2p, cap):
    nblk = cap // DISPATCH_ROWS
    blk_rows = DISPATCH_ROWS * PACK_ROWS
    idx = row_token.reshape(nblk, 1, DISPATCH_ROWS)
    return pl.pallas_call(
        _dispatch_kernel,
        grid_spec=pltpu.PrefetchScalarGridSpec(
            num_scalar_prefetch=1,
            grid=(nblk,),
            in_specs=[pl.BlockSpec((1, 1, DISPATCH_ROWS), lambda i, a: (i, 0, 0), memory_space=pltpu.SMEM),
                      pl.BlockSpec((1, 1, DISPATCH_ROWS), lambda i, a: (jnp.minimum(i + 1, nblk - 1), 0, 0),
                                   memory_space=pltpu.SMEM),
                      pl.BlockSpec(memory_space=pl.ANY)],
            out_specs=pl.BlockSpec((blk_rows, LANE), lambda i, a: (i, 0)),
            scratch_shapes=[pltpu.VMEM((2 * blk_rows, LANE), U32), pltpu.SemaphoreType.DMA((2,))]),
        out_shape=jax.ShapeDtypeStruct((cap * PACK_ROWS, LANE), U32),
        compiler_params=_cparams(("arbitrary",)),
        name="moe_dispatch",
    )(blk_active, idx, idx, h2p)


def _moe_kernel(ce_ref, nv_ref, xb_ref, ob_ref, x_ref, wgu_ref, bgu_ref, wd_ref, bd_ref, out_ref, xbf_ref, acc_ref):
    c = pl.program_id(0)
    j = pl.program_id(1)
    nv = nv_ref[c]
    nsub = MOE_CHUNK // MOE_SUB

    def unpack(sb):
        for s in range(PACK_ROWS):
            w = x_ref[pl.ds(sb * MOE_SUB * PACK_ROWS + s, MOE_SUB, stride=PACK_ROWS), :]
            lo = pltpu.bitcast(w << 16, F32)
            hi = pltpu.bitcast(w & jnp.uint32(0xFFFF0000), F32)
            xbf_ref[sb * MOE_SUB:(sb + 1) * MOE_SUB, s * LANE:(s + 1) * LANE] = lo.astype(BF16)
            xbf_ref[sb * MOE_SUB:(sb + 1) * MOE_SUB,
                    (s + PACK_ROWS) * LANE:(s + PACK_ROWS + 1) * LANE] = hi.astype(BF16)

    def for_active_sub_blocks(fn):
        @pl.when(nv == MOE_CHUNK)
        def _():
            for sb in range(nsub):
                fn(sb)

        @pl.when(nv < MOE_CHUNK)
        def _():
            for sb in range(nsub):
                @pl.when(sb * MOE_SUB < nv)
                def _():
                    fn(sb)

    @pl.when(jnp.logical_and(j == 0, nv > 0))
    def _():
        acc_ref[...] = jnp.broadcast_to(bd_ref[0], (MOE_CHUNK, D_MODEL))
        for_active_sub_blocks(unpack)

    wd = wd_ref[0].astype(BF16)

    def sub_block(sb):
        x = xbf_ref[sb * MOE_SUB:(sb + 1) * MOE_SUB, :]
        gu = _dot(x, wgu_ref[0]) + bgu_ref[0]
        ng = MOE_TF // LANE
        g = jnp.concatenate([gu[:, 2 * b * LANE:(2 * b + 1) * LANE] for b in range(ng)], axis=-1)
        u = jnp.concatenate([gu[:, (2 * b + 1) * LANE:(2 * b + 2) * LANE] for b in range(ng)], axis=-1)
        g = jnp.minimum(g, SWIGLU_LIMIT)
        u = jnp.clip(u, -SWIGLU_LIMIT, SWIGLU_LIMIT)
        act = g * (1.0 / (1.0 + jnp.exp(-(g * SWIGLU_ALPHA)))) * (u + 1.0)
        acc_ref[sb * MOE_SUB:(sb + 1) * MOE_SUB, :] += _dot(act.astype(BF16), wd)

    for_active_sub_blocks(sub_block)

    @pl.when(jnp.logical_and(j == pl.num_programs(1) - 1, nv > 0))
    def _():
        for s in range(OUT_ROWS):
            out_ref[pl.ds(s, MOE_CHUNK, stride=OUT_ROWS), :] = acc_ref[:, s * LANE:(s + 1) * LANE]


def _moe(chunk_expert, chunk_rows, x_blk, out_blk, xs, wgu, bgu, w_down, b_down, nch):
    nj = D_FF // MOE_TF

    def jj(c, j, nv):
        return jnp.where(nv[c] > 0, j, nj - 1)

    return pl.pallas_call(
        _moe_kernel,
        grid_spec=pltpu.PrefetchScalarGridSpec(
            num_scalar_prefetch=4,
            grid=(nch, nj),
            in_specs=[
                pl.BlockSpec((MOE_CHUNK * PACK_ROWS, LANE), lambda c, j, ce, nv, xb, ob: (xb[c], 0)),
                pl.BlockSpec((1, D_MODEL, 2 * MOE_TF), lambda c, j, ce, nv, xb, ob: (ce[c], 0, jj(c, j, nv))),
                pl.BlockSpec((1, 1, 2 * MOE_TF), lambda c, j, ce, nv, xb, ob: (ce[c], 0, jj(c, j, nv))),
                pl.BlockSpec((1, MOE_TF, D_MODEL), lambda c, j, ce, nv, xb, ob: (ce[c], jj(c, j, nv), 0)),
                pl.BlockSpec((1, 1, D_MODEL), lambda c, j, ce, nv, xb, ob: (ce[c], 0, 0))],
            out_specs=pl.BlockSpec((MOE_CHUNK * OUT_ROWS, LANE), lambda c, j, ce, nv, xb, ob: (ob[c], 0)),
            scratch_shapes=[pltpu.VMEM((MOE_CHUNK, D_MODEL), BF16), pltpu.VMEM((MOE_CHUNK, D_MODEL), F32)]),
        out_shape=jax.ShapeDtypeStruct(((nch + 1) * MOE_CHUNK * OUT_ROWS, LANE), F32),
        compiler_params=_cparams(("arbitrary", "arbitrary")),
        name="moe_experts",
    )(chunk_expert, chunk_rows, x_blk, out_blk, xs, wgu, bgu, w_down, b_down)


def _combine_kernel(idx0_ref, idx1_ref, gate_ref, x1_ref, mod_ref, gfin_ref, os_ref, y_ref, buf_ref, sem):
    i = pl.program_id(0)
    n = pl.num_programs(0)
    nrow = COMBINE_TOKENS * TOP_K
    slot_rows = nrow * OUT_ROWS

    def issue(idx_ref, slot):
        def body(r0, carry):
            for u in range(DMA_UNROLL):
                r = r0 * DMA_UNROLL + u
                src = pl.multiple_of(idx_ref[0, 0, r] * OUT_ROWS, OUT_ROWS)
                dst = pl.multiple_of(slot * slot_rows + r * OUT_ROWS, OUT_ROWS)
                pltpu.make_async_copy(os_ref.at[pl.ds(src, OUT_ROWS)], buf_ref.at[pl.ds(dst, OUT_ROWS)],
                                      sem.at[slot]).start()
            return carry

        lax.fori_loop(0, nrow // DMA_UNROLL, body, 0)

    @pl.when(i == 0)
    def _():
        issue(idx0_ref, 0)

    @pl.when(i + 1 < n)
    def _():
        issue(idx1_ref, (i + 1) % 2)

    slot = i % 2
    base = pl.multiple_of(slot * slot_rows, slot_rows)
    pltpu.make_async_copy(os_ref.at[pl.ds(0, slot_rows)], buf_ref.at[pl.ds(base, slot_rows)], sem.at[slot]).wait()
    gate = gate_ref[...]
    pieces = []
    for s in range(OUT_ROWS):
        acc = None
        for k in range(TOP_K):
            rows = buf_ref[pl.ds(base + k * OUT_ROWS + s, COMBINE_TOKENS, stride=TOP_K * OUT_ROWS), :]
            term = gate[:, k:k + 1] * rows
            acc = term if acc is None else acc + term
        pieces.append(acc)
    y = jnp.concatenate(pieces, axis=-1)
    gf = mod_ref[0, 5:6, :]
    y_ref[...] = _rms(x1_ref[...] + gf * y, gfin_ref[...])


def _combine(dest, gates, x1, mod, g_final, out_sorted, *, tile0, ntiles, mod_row):
    tt = COMBINE_TOKENS
    nrow = tt * TOP_K
    total_tiles = dest.shape[0]
    return pl.pallas_call(
        _combine_kernel,
        grid=(ntiles,),
        in_specs=[pl.BlockSpec((1, 1, nrow), lambda i: (tile0 + i, 0, 0), memory_space=pltpu.SMEM),
                  pl.BlockSpec((1, 1, nrow), lambda i: (jnp.minimum(tile0 + i + 1, total_tiles - 1), 0, 0),
                               memory_space=pltpu.SMEM),
                  pl.BlockSpec((tt, LANE), lambda i: (tile0 + i, 0)),
                  pl.BlockSpec((tt, D_MODEL), lambda i: (tile0 + i, 0)),
                  pl.BlockSpec((1, 6, D_MODEL), lambda i: (mod_row(i), 0, 0)),
                  _const_spec((1, D_MODEL)),
                  pl.BlockSpec(memory_space=pl.ANY)],
        out_specs=pl.BlockSpec((tt, D_MODEL), lambda i: (i, 0)),
        out_shape=jax.ShapeDtypeStruct((ntiles * tt, D_MODEL), F32),
        scratch_shapes=[pltpu.VMEM((2 * nrow * OUT_ROWS, LANE), F32), pltpu.SemaphoreType.DMA((2,))],
        compiler_params=_cparams(("arbitrary",)),
        name="moe_combine_final_norm",
    )(dest, dest, gates, x1, mod, g_final, out_sorted)


def _routing(top_idx, nch):
    n = top_idx.shape[0]
    nk = n * TOP_K
    flat_e = top_idx.reshape(nk)
    order = jnp.argsort(flat_e, stable=True).astype(jnp.int32)
    inv = jnp.argsort(order).astype(jnp.int32)
    onehot = flat_e[:, None] == jnp.arange(N_EXPERTS, dtype=jnp.int32)[None, :]
    counts = jnp.sum(onehot.astype(jnp.int32), axis=0)
    grp_start = jnp.cumsum(counts) - counts
    chunks_e = (counts + MOE_CHUNK - 1) // MOE_CHUNK
    chunk_end = jnp.cumsum(chunks_e)
    chunk_start = chunk_end - chunks_e
    slot_start = chunk_start * MOE_CHUNK
    dest = inv + jnp.sum(jnp.where(onehot, (slot_start - grp_start)[None, :], 0), axis=1)

    total = chunk_end[-1]
    cidx = jnp.arange(nch, dtype=jnp.int32)
    active = cidx < total
    ce = jnp.minimum(jnp.searchsorted(chunk_end, cidx, side='right'), N_EXPERTS - 1).astype(jnp.int32)
    nv = jnp.clip(counts[ce] - (cidx - chunk_start[ce]) * MOE_CHUNK, 0, MOE_CHUNK)
    nv = jnp.where(active, nv, 0).astype(jnp.int32)
    last = jnp.maximum(total - 1, 0)
    chunk_expert = jnp.where(active, ce, ce[last]).astype(jnp.int32)
    x_blk = jnp.where(active, cidx, last).astype(jnp.int32)
    out_blk = jnp.where(active, cidx, nch).astype(jnp.int32)

    per = MOE_CHUNK // DISPATCH_ROWS
    sub = jnp.tile(jnp.arange(per, dtype=jnp.int32) * DISPATCH_ROWS, nch)
    blk_valid = jnp.clip(jnp.repeat(nv, per) - sub, 0, DISPATCH_ROWS)
    blk_start = jnp.repeat(grp_start[ce] + (cidx - chunk_start[ce]) * MOE_CHUNK, per) + sub
    blk_start = jnp.clip(jnp.where(blk_valid > 0, blk_start, 0), 0, nk)
    order_tok = jnp.concatenate([order // TOP_K, jnp.zeros((DISPATCH_ROWS,), jnp.int32)])
    windows = jax.vmap(lambda s: lax.dynamic_slice(order_tok, (s,), (DISPATCH_ROWS,)))(blk_start)
    row_token = jnp.where(jnp.arange(DISPATCH_ROWS, dtype=jnp.int32)[None, :] < blk_valid[:, None], windows, 0)
    blk_active = (blk_valid > 0).astype(jnp.int32)
    return dest.astype(jnp.int32), row_token.astype(jnp.int32), blk_active, chunk_expert, nv, x_blk, out_blk


def _rope_tables(t):
    pos = jnp.arange(t)
    rows = (pos // GRID_W).astype(F32)
    cols = (pos % GRID_W).astype(F32)
    inv = ROPE_THETA ** (-(jnp.arange(ROPE_AXIS // 2, dtype=F32) * 2.0 / ROPE_AXIS))
    ar = rows[:, None] * inv
    ac = cols[:, None] * inv
    ang = jnp.concatenate([ar, ar, ac, ac], axis=-1)
    return jnp.cos(ang), jnp.sin(ang)


def _rot_cols(w):
    half = ROPE_AXIS // 2
    src = np.concatenate([np.arange(half, ROPE_AXIS), np.arange(0, half),
                          np.arange(ROPE_AXIS + half, 2 * ROPE_AXIS), np.arange(ROPE_AXIS, ROPE_AXIS + half)])
    sign = np.concatenate([-np.ones(half), np.ones(half), -np.ones(half), np.ones(half)]).astype(np.float32)
    return w[..., src] * sign


def kernel(x_prompt, x_sample, cache_na_k, cache_na_v, cache_mla_ckv, cache_mla_krope, c, c_ctx, g_attn, g_ffn, g_final, w_mod, b_mod, w_in, w_out, na_rpb, g_q_a, w_q_b, g_kv_a, w_kv_b, w_router, b_router, w_gate_up, b_gate_up, w_down, b_down):
    bp, sp, d = x_prompt.shape
    bd, td, _ = x_sample.shape
    assert d == D_MODEL and w_mod.shape[0] == 1, "one trunk layer of width D_MODEL"
    n_p = bp * sp
    n_s = bd * td
    xp = x_prompt.reshape(n_p, d)
    xs = x_sample.reshape(n_s, d)

    c8 = jnp.zeros((8, d), F32).at[0].set(c_ctx).at[1:1 + bd].set(c)
    mod = _modulation(c8, w_mod[0], b_mod[0].reshape(1, -1)).reshape(8, 6, d)

    w_in0 = w_in[0]
    w_kr = w_in0[:, KR_OFF:KR_OFF + QK_ROPE]
    w_in_p = jnp.concatenate([w_in0, jnp.zeros((d, LANE - QK_ROPE), F32)], axis=1).astype(BF16)
    w_in_s = jnp.concatenate([w_in0, _rot_cols(w_kr)], axis=1).astype(BF16)
    wq = w_q_b[0].reshape(Q_LORA, MLA_HEADS, QK_NOPE + QK_ROPE)
    zpad = jnp.zeros((Q_LORA, MLA_HEADS, Q_PAD - QK_NOPE - QK_ROPE), F32)
    wq_pad = jnp.concatenate([wq, zpad], axis=-1).reshape(Q_LORA, MLA_HEADS * Q_PAD)
    wq_rot = jnp.concatenate([jnp.zeros((Q_LORA, MLA_HEADS, QK_NOPE), F32), _rot_cols(wq[..., QK_NOPE:]), zpad],
                             axis=-1).reshape(Q_LORA, MLA_HEADS * Q_PAD)
    wqb_p = wq_pad.astype(BF16)
    wqb_s = jnp.concatenate([wq_pad, wq_rot], axis=1).astype(BF16)
    w_kvb = w_kv_b[0].astype(BF16)
    cos, sin = _rope_tables(td)
    cosq = jnp.concatenate([jnp.ones((td, QK_NOPE), F32), cos, jnp.ones((td, Q_PAD - QK_NOPE - QK_ROPE), F32)], axis=1)
    sinq = jnp.concatenate([jnp.zeros((td, QK_NOPE), F32), sin, jnp.zeros((td, Q_PAD - QK_NOPE - QK_ROPE), F32)], axis=1)
    csk = jnp.concatenate([cos, sin], axis=1)

    tiles_per_seq = td // ROW_TILE
    g_attn2 = g_attn[0].reshape(1, d)
    gq2 = g_q_a[0].reshape(1, Q_LORA)
    gkv2 = g_kv_a[0].reshape(1, KV_LORA)

    naq_p, nak_p, nav_p, q_p, ckv_p, kr_p, krp_p = _pre_attention(
        xp, mod, g_attn2, w_in_p, gq2, wqb_p, gkv2, None, rope=False, mod_row=lambda i: 0)
    ona_p, omla_p = _prompt_attention(naq_p, nak_p, nav_p, q_p, ckv_p, krp_p, w_kvb, sp)

    naq_s, nak_s, nav_s, q_s, ckv_s, krp_s = _pre_attention(
        xs, mod, g_attn2, w_in_s, gq2, wqb_s, gkv2, (cosq, sinq, csk), rope=True,
        mod_row=lambda i: 1 + i // tiles_per_seq)
    past = cache_na_k.shape[2]
    kc = cache_na_k[:, 0].reshape(bd, past, NA_WIDTH)
    vc = cache_na_v[:, 0].reshape(bd, past, NA_WIDTH)
    bias = _na_bias_table(na_rpb[0], td // GRID_W)
    ona_s = _sample_na(naq_s, nak_s, nav_s, kc, vc, bias, bd, td)
    ckv_all = jnp.concatenate([ckv_s.reshape(bd, td, KV_LORA), cache_mla_ckv[:, 0].astype(BF16)], axis=1)
    krp_c = jnp.concatenate([cache_mla_krope[:, 0], jnp.zeros((bd, past, LANE - QK_ROPE), F32)], axis=-1).astype(BF16)
    krp_all = jnp.concatenate([krp_s.reshape(bd, td, LANE), krp_c], axis=1)
    kf, vv = _kv_expand(ckv_all, krp_all, w_kvb)
    omla_s = _sample_mla(q_s, kf, vv, bd, td)

    npt = n_p // ROW_TILE
    x1, h2p, idx128, gate128 = _post_attention(
        xp, xs, ona_p, omla_p, ona_s, omla_s, mod, w_out[0].astype(BF16), g_ffn[0].reshape(1, d),
        w_router[0], b_router[0].reshape(1, N_EXPERTS),
        mod_row=lambda i: jnp.where(i < npt, 0, 1 + jnp.maximum(i - npt, 0) // tiles_per_seq))

    n = n_p + n_s
    nch = n * TOP_K // MOE_CHUNK + N_EXPERTS
    dest, row_token, blk_active, chunk_expert, chunk_rows, x_blk, out_blk = _routing(idx128[:, :TOP_K], nch)
    x_sorted = _dispatch(blk_active, row_token, h2p, nch * MOE_CHUNK)
    wgu = _deinterleave_gate_up(w_gate_up[0])
    ng = 2 * D_FF // (2 * LANE)
    bgu = b_gate_up[0].reshape(N_EXPERTS, ng, LANE, 2).transpose(0, 1, 3, 2).reshape(N_EXPERTS, 1, 2 * D_FF)
    out_sorted = _moe(chunk_expert, chunk_rows, x_blk, out_blk, x_sorted, wgu, bgu, w_down[0],
                      b_down[0].reshape(N_EXPERTS, 1, d), nch)

    tt = COMBINE_TOKENS
    dest3 = dest.reshape(n // tt, 1, tt * TOP_K)
    gfin = g_final.reshape(1, d)
    ctiles_seq = td // tt
    y_p = _combine(dest3, gate128, x1, mod, gfin, out_sorted, tile0=0, ntiles=n_p // tt, mod_row=lambda i: 0)
    y_s = _combine(dest3, gate128, x1, mod, gfin, out_sorted, tile0=n_p // tt, ntiles=n_s // tt,
                   mod_row=lambda i: 1 + i // ctiles_seq)

    return (y_p.reshape(bp, sp, d), y_s.reshape(bd, td, d),
            nak_p.reshape(bp, 1, sp, NA_HEADS, NA_HEAD_DIM), nav_p.reshape(bp, 1, sp, NA_HEADS, NA_HEAD_DIM),
            ckv_p.reshape(bp, 1, sp, KV_LORA), kr_p.reshape(bp, 1, sp, QK_ROPE))
```

```python
import functools

import numpy as np
import jax
import jax.numpy as jnp
from jax import lax
from jax.experimental import pallas as pl
from jax.experimental.pallas import tpu as pltpu

F32 = jnp.float32
BF16 = jnp.bfloat16
U32 = jnp.uint32

D_MODEL = 2048
GRID_W = 64
NA_HEADS = 8
NA_HEAD_DIM = 128
NA_KR = 8
NA_KC = 16
MLA_HEADS = 8
Q_LORA = 512
KV_LORA = 256
QK_NOPE = 128
QK_ROPE = 64
V_DIM = 128
ROPE_AXIS = QK_ROPE // 2
ROPE_THETA = 10000.0
NA_WIDTH = NA_HEADS * NA_HEAD_DIM
MLA_WIDTH = MLA_HEADS * V_DIM
IN_COLS = 3 * NA_WIDTH + Q_LORA + KV_LORA + QK_ROPE
N_EXPERTS = 32
TOP_K = 4
D_FF = D_MODEL
SWIGLU_ALPHA = 1.702
SWIGLU_LIMIT = 7.0
EPS = 1e-6

LANE = 128
Q_PAD = 2 * LANE
KR_OFF = 3 * NA_WIDTH + Q_LORA + KV_LORA
IN_COLS_PAD = KR_OFF + LANE
VMEM_LIMIT = 56 * 1024 * 1024
NEG_BIG = -1e30

LOG2E = 1.4426950408889634
NA_QSCALE = NA_HEAD_DIM ** -0.5 * LOG2E
MLA_QSCALE = (QK_NOPE + QK_ROPE) ** -0.5 * LOG2E

ROW_TILE = 256
POST_TILE = 2 * ROW_TILE
NA_Q_ROWS = 4
NA_WIN_ROWS = 12
MLA_TQ = 512
MLA_TK = 1152
MOE_CHUNK = 1024
MOE_SUB = 256
MOE_TF = 512
DISPATCH_ROWS = 256
COMBINE_TOKENS = 128
DMA_UNROLL = 8
PACK_ROWS = D_MODEL // (2 * LANE)
OUT_ROWS = D_MODEL // LANE


def _cparams(sem):
    return pltpu.CompilerParams(dimension_semantics=sem, vmem_limit_bytes=VMEM_LIMIT)


def _const_spec(shape):
    nd = len(shape)
    return pl.BlockSpec(shape, lambda *a: (0,) * nd, pipeline_mode=pl.Buffered(1))


def _rms(x, g):
    return x * lax.rsqrt(jnp.mean(x * x, axis=-1, keepdims=True) + EPS) * g


def _dot(a, b):
    return jnp.dot(a, b, preferred_element_type=F32)


def _dot_nt(a, b):
    return lax.dot_general(a, b, (((1,), (1,)), ((), ())), preferred_element_type=F32)


def _mod_kernel(c_ref, w_ref, b_ref, o_ref):
    c = c_ref[...]
    s = c / (1.0 + jnp.exp(-c))
    o_ref[...] = _dot(s.astype(BF16), w_ref[...].astype(BF16)) + b_ref[...]


def _modulation(c8, w_mod, b_mod):
    n = w_mod.shape[1]
    tn = 1024
    return pl.pallas_call(
        _mod_kernel,
        grid=(n // tn,),
        in_specs=[pl.BlockSpec((8, D_MODEL), lambda j: (0, 0)),
                  pl.BlockSpec((D_MODEL, tn), lambda j: (0, j)),
                  pl.BlockSpec((1, tn), lambda j: (0, j))],
        out_specs=pl.BlockSpec((8, tn), lambda j: (0, j)),
        out_shape=jax.ShapeDtypeStruct((8, n), F32),
        compiler_params=_cparams(("arbitrary",)),
        name="modulation",
    )(c8, w_mod, b_mod)


def _pre_kernel(*refs, rope):
    if rope:
        (x_ref, mod_ref, g_ref, win_ref, gq_ref, wqb_ref, gkv_ref, cosq_ref, sinq_ref, csk_ref,
         naq_ref, nak_ref, nav_ref, q_ref, ckv_ref, krp_ref) = refs
    else:
        (x_ref, mod_ref, g_ref, win_ref, gq_ref, wqb_ref, gkv_ref,
         naq_ref, nak_ref, nav_ref, q_ref, ckv_ref, kr_ref, krp_ref) = refs
    x = x_ref[...]
    sa = mod_ref[0, 0:1, :]
    sca = mod_ref[0, 1:2, :]
    h = (_rms(x, g_ref[...]) * (1.0 + sca) + sa).astype(BF16)
    proj = _dot(h, win_ref[...])
    naq_ref[...] = (proj[:, 0:NA_WIDTH] * NA_QSCALE).astype(naq_ref.dtype)
    nak_ref[...] = proj[:, NA_WIDTH:2 * NA_WIDTH].astype(nak_ref.dtype)
    nav_ref[...] = proj[:, 2 * NA_WIDTH:3 * NA_WIDTH].astype(nav_ref.dtype)
    q_a = proj[:, 3 * NA_WIDTH:3 * NA_WIDTH + Q_LORA]
    kv_a = proj[:, 3 * NA_WIDTH + Q_LORA:KR_OFF]
    krx = proj[:, KR_OFF:IN_COLS_PAD]
    qan = _rms(q_a, gq_ref[...]).astype(BF16)
    qq = _dot(qan, wqb_ref[...])
    ckv_ref[...] = _rms(kv_a, gkv_ref[...]).astype(ckv_ref.dtype)
    if rope:
        width = MLA_HEADS * Q_PAD
        cosq = cosq_ref[...]
        sinq = sinq_ref[...]
        for hd in range(MLA_HEADS):
            a = qq[:, hd * Q_PAD:(hd + 1) * Q_PAD]
            b = qq[:, width + hd * Q_PAD:width + (hd + 1) * Q_PAD]
            q_ref[:, hd * Q_PAD:(hd + 1) * Q_PAD] = (a * cosq + b * sinq).astype(BF16)
        y = krx * csk_ref[...]
        y = y + pltpu.roll(y, QK_ROPE, 1)
        lane = lax.broadcasted_iota(jnp.int32, y.shape, 1)
        krp_ref[...] = jnp.where(lane < QK_ROPE, y, 0.0).astype(BF16)
    else:
        q_ref[...] = (qq * MLA_QSCALE).astype(BF16)
        kr_ref[...] = krx[:, 0:QK_ROPE]
        krp_ref[...] = krx.astype(BF16)


def _pre_attention(x, mod, g_attn, w_in, g_q_a, w_qb, g_kv_a, rope_tabs, *, rope, mod_row):
    n = x.shape[0]
    tm = ROW_TILE
    row = lambda i: (i, 0)
    in_specs = [pl.BlockSpec((tm, D_MODEL), row),
                pl.BlockSpec((1, 6, D_MODEL), lambda i: (mod_row(i), 0, 0)),
                _const_spec((1, D_MODEL)),
                _const_spec(w_in.shape),
                _const_spec((1, Q_LORA)),
                _const_spec(w_qb.shape),
                _const_spec((1, KV_LORA))]
    args = [x, mod, g_attn, w_in, g_q_a, w_qb, g_kv_a]
    qw = MLA_HEADS * Q_PAD
    if rope:
        tiles_per_seq = rope_tabs[0].shape[0] // tm
        pos = lambda i: (i % tiles_per_seq, 0)
        in_specs += [pl.BlockSpec((tm, Q_PAD), pos), pl.BlockSpec((tm, Q_PAD), pos),
                     pl.BlockSpec((tm, LANE), pos)]
        args += list(rope_tabs)
        out_shape = [jax.ShapeDtypeStruct((n, NA_WIDTH), BF16)] * 3 + [
            jax.ShapeDtypeStruct((n, qw), BF16),
            jax.ShapeDtypeStruct((n, KV_LORA), BF16),
            jax.ShapeDtypeStruct((n, LANE), BF16)]
        out_specs = [pl.BlockSpec((tm, NA_WIDTH), row)] * 3 + [
            pl.BlockSpec((tm, qw), row), pl.BlockSpec((tm, KV_LORA), row), pl.BlockSpec((tm, LANE), row)]
    else:
        out_shape = [jax.ShapeDtypeStruct((n, NA_WIDTH), BF16),
                     jax.ShapeDtypeStruct((n, NA_WIDTH), F32),
                     jax.ShapeDtypeStruct((n, NA_WIDTH), F32),
                     jax.ShapeDtypeStruct((n, qw), BF16),
                     jax.ShapeDtypeStruct((n, KV_LORA), F32),
                     jax.ShapeDtypeStruct((n, QK_ROPE), F32),
                     jax.ShapeDtypeStruct((n, LANE), BF16)]
        out_specs = [pl.BlockSpec((tm, NA_WIDTH), row)] * 3 + [
            pl.BlockSpec((tm, qw), row), pl.BlockSpec((tm, KV_LORA), row),
            pl.BlockSpec((tm, QK_ROPE), row), pl.BlockSpec((tm, LANE), row)]
    return pl.pallas_call(
        functools.partial(_pre_kernel, rope=rope),
        grid=(n // tm,),
        in_specs=in_specs,
        out_specs=out_specs,
        out_shape=out_shape,
        compiler_params=_cparams(("arbitrary",)),
        name="pre_attention_rope" if rope else "pre_attention",
    )(*args)


def _softmax_pv(s, v):
    m = jnp.max(s, axis=-1, keepdims=True)
    p = jnp.exp2(s - m)
    l = jnp.sum(p, axis=-1, keepdims=True)
    return _dot(p.astype(BF16), v) / l


def _prompt_attn_kernel(naq_ref, nak_ref, nav_ref, q_ref, ckv_ref, krp_ref, wkvb_ref, ona_ref, omla_ref):
    kv = _dot(ckv_ref[...].astype(BF16), wkvb_ref[...])
    krp = krp_ref[...]
    for hd in range(NA_HEADS):
        sl = slice(hd * NA_HEAD_DIM, (hd + 1) * NA_HEAD_DIM)
        s = _dot_nt(naq_ref[:, sl], nak_ref[:, sl].astype(BF16))
        ona_ref[:, sl] = _softmax_pv(s, nav_ref[:, sl].astype(BF16)).astype(BF16)
    kvw = QK_NOPE + V_DIM
    for hd in range(MLA_HEADS):
        kf = jnp.concatenate([kv[:, hd * kvw:hd * kvw + QK_NOPE].astype(BF16), krp], axis=-1)
        s = _dot_nt(q_ref[:, hd * Q_PAD:(hd + 1) * Q_PAD], kf)
        v = kv[:, hd * kvw + QK_NOPE:(hd + 1) * kvw].astype(BF16)
        omla_ref[:, hd * V_DIM:(hd + 1) * V_DIM] = _softmax_pv(s, v).astype(BF16)


def _prompt_attention(naq, nak, nav, q, ckv, krp, w_kvb, seq):
    n = naq.shape[0]
    row = lambda b: (b, 0)
    return pl.pallas_call(
        _prompt_attn_kernel,
        grid=(n // seq,),
        in_specs=[pl.BlockSpec((seq, NA_WIDTH), row)] * 3 + [
            pl.BlockSpec((seq, MLA_HEADS * Q_PAD), row),
            pl.BlockSpec((seq, KV_LORA), row),
            pl.BlockSpec((seq, LANE), row),
            _const_spec(w_kvb.shape)],
        out_specs=[pl.BlockSpec((seq, NA_WIDTH), row), pl.BlockSpec((seq, MLA_WIDTH), row)],
        out_shape=[jax.ShapeDtypeStruct((n, NA_WIDTH), BF16), jax.ShapeDtypeStruct((n, MLA_WIDTH), BF16)],
        compiler_params=_cparams(("arbitrary",)),
        name="prompt_attention",
    )(naq, nak, nav, q, ckv, krp, w_kvb)


def _na_kernel(q_ref, k_ref, v_ref, kc_ref, vc_ref, bias_ref, o_ref, *, rows):
    rt = pl.program_id(2)
    ws = jnp.clip(rt * NA_Q_ROWS - NA_KR // 2, 0, rows - NA_WIN_ROWS)
    start = pl.multiple_of(ws * GRID_W, GRID_W)
    nwin = NA_WIN_ROWS * GRID_W
    q = q_ref[...]
    s_loc = _dot_nt(q, k_ref[pl.ds(start, nwin), :]) + bias_ref[0, 0]
    s_ctx = _dot_nt(q, kc_ref[0].astype(BF16))
    m = jnp.maximum(jnp.max(s_loc, axis=-1, keepdims=True), jnp.max(s_ctx, axis=-1, keepdims=True))
    p_loc = jnp.exp2(s_loc - m)
    p_ctx = jnp.exp2(s_ctx - m)
    l = jnp.sum(p_loc, axis=-1, keepdims=True) + jnp.sum(p_ctx, axis=-1, keepdims=True)
    o = _dot(p_loc.astype(BF16), v_ref[pl.ds(start, nwin), :]) + _dot(p_ctx.astype(BF16), vc_ref[0].astype(BF16))
    o_ref[...] = (o / l).astype(BF16)


def _na_bias_table(rpb, rows):
    nh = rpb.shape[0]
    a = np.arange(NA_Q_ROWS)
    b = np.arange(NA_WIN_ROWS)
    col = np.arange(GRID_W)
    cs = np.clip(col - NA_KC // 2, 0, GRID_W - NA_KC)
    valid_col = (col[None, :] >= cs[:, None]) & (col[None, :] < cs[:, None] + NA_KC)
    padw = GRID_W - NA_KC
    rp = jnp.pad(rpb, ((0, 0), (0, 0), (padw, padw)))
    toep = jnp.stack([rp[:, :, GRID_W - 1 - qc:2 * GRID_W - 1 - qc] for qc in range(GRID_W)], axis=2)
    toep = jnp.where(valid_col[None, None], toep, NEG_BIG)
    masked = jnp.full((nh, GRID_W, GRID_W), NEG_BIG, F32)
    tabs = []
    for r0 in (0, NA_Q_ROWS, rows - NA_Q_ROWS):
        ws = int(np.clip(r0 - NA_KR // 2, 0, rows - NA_WIN_ROWS))
        r = r0 + a
        rs = np.clip(r - NA_KR // 2, 0, rows - NA_KR)
        kr = ws + b
        valid_row = (kr[None, :] >= rs[:, None]) & (kr[None, :] < rs[:, None] + NA_KR)
        dr = kr[None, :] - r[:, None] + NA_KR - 1
        tile_rows = []
        for ai in range(NA_Q_ROWS):
            blocks = [toep[:, int(dr[ai, bi])] if valid_row[ai, bi] else masked for bi in range(NA_WIN_ROWS)]
            tile_rows.append(jnp.concatenate(blocks, axis=-1))
        tabs.append(jnp.concatenate(tile_rows, axis=1))
    return jnp.stack(tabs)


def _sample_na(naq, nak, nav, kc, vc, bias, batch, seq):
    rows = seq // GRID_W
    tq = NA_Q_ROWS * GRID_W
    nt = seq // tq
    last = nt - 1

    def pat(rt):
        return jnp.where(rt == 0, 0, jnp.where(rt == last, 2, 1))

    return pl.pallas_call(
        functools.partial(_na_kernel, rows=rows),
        grid=(batch, NA_HEADS, nt),
        in_specs=[pl.BlockSpec((tq, NA_HEAD_DIM), lambda b, h, r: (b * nt + r, h)),
                  pl.BlockSpec((seq, NA_HEAD_DIM), lambda b, h, r: (b, h)),
                  pl.BlockSpec((seq, NA_HEAD_DIM), lambda b, h, r: (b, h)),
                  pl.BlockSpec((1, kc.shape[1], NA_HEAD_DIM), lambda b, h, r: (b, 0, h)),
                  pl.BlockSpec((1, vc.shape[1], NA_HEAD_DIM), lambda b, h, r: (b, 0, h)),
                  pl.BlockSpec((1, 1, tq, NA_WIN_ROWS * GRID_W), lambda b, h, r: (pat(r), h, 0, 0))],
        out_specs=pl.BlockSpec((tq, NA_HEAD_DIM), lambda b, h, r: (b * nt + r, h)),
        out_shape=jax.ShapeDtypeStruct((batch * seq, NA_WIDTH), BF16),
        compiler_params=_cparams(("arbitrary", "arbitrary", "arbitrary")),
        name="sample_neighbourhood_attention",
    )(naq, nak, nav, kc, vc, bias)


def _kv_expand_kernel(ckv_ref, krp_ref, w_ref, kf_ref, v_ref):
    kv = _dot(ckv_ref[0], w_ref[...])
    krp = krp_ref[0]
    kvw = QK_NOPE + V_DIM
    for hd in range(MLA_HEADS):
        kf_ref[0, hd, :, 0:QK_NOPE] = kv[:, hd * kvw:hd * kvw + QK_NOPE].astype(BF16)
        kf_ref[0, hd, :, QK_NOPE:Q_PAD] = krp
        v_ref[0, hd] = kv[:, hd * kvw + QK_NOPE:(hd + 1) * kvw].astype(BF16)


def _kv_expand(ckv, krp, w_kvb):
    batch, nkeys, _ = ckv.shape
    tm = 512
    return pl.pallas_call(
        _kv_expand_kernel,
        grid=(batch, nkeys // tm),
        in_specs=[pl.BlockSpec((1, tm, KV_LORA), lambda b, t: (b, t, 0)),
                  pl.BlockSpec((1, tm, LANE), lambda b, t: (b, t, 0)),
                  _const_spec(w_kvb.shape)],
        out_specs=[pl.BlockSpec((1, MLA_HEADS, tm, Q_PAD), lambda b, t: (b, 0, t, 0)),
                   pl.BlockSpec((1, MLA_HEADS, tm, V_DIM), lambda b, t: (b, 0, t, 0))],
        out_shape=[jax.ShapeDtypeStruct((batch, MLA_HEADS, nkeys, Q_PAD), BF16),
                   jax.ShapeDtypeStruct((batch, MLA_HEADS, nkeys, V_DIM), BF16)],
        compiler_params=_cparams(("arbitrary", "arbitrary")),
        name="latent_kv_expand",
    )(ckv, krp, w_kvb)


def _mla_kernel(q_ref, kf_ref, v_ref, o_ref, *, nkeys):
    q = q_ref[...]
    tq = q.shape[0]
    m = jnp.full((tq, 1), NEG_BIG, F32)
    l = jnp.zeros((tq, 1), F32)
    acc = jnp.zeros((tq, V_DIM), F32)
    for c in range(nkeys // MLA_TK):
        ks = slice(c * MLA_TK, (c + 1) * MLA_TK)
        s = _dot_nt(q, kf_ref[0, 0, ks, :])
        m_new = jnp.maximum(m, jnp.max(s, axis=-1, keepdims=True))
        alpha = jnp.exp2(m - m_new)
        p = jnp.exp2(s - m_new)
        l = alpha * l + jnp.sum(p, axis=-1, keepdims=True)
        acc = alpha * acc + _dot(p.astype(BF16), v_ref[0, 0, ks, :])
        m = m_new
    o_ref[...] = (acc / l).astype(BF16)


def _sample_mla(q, kf, v, batch, seq):
    nkeys = kf.shape[2]
    nt = seq // MLA_TQ
    return pl.pallas_call(
        functools.partial(_mla_kernel, nkeys=nkeys),
        grid=(batch, MLA_HEADS, nt),
        in_specs=[pl.BlockSpec((MLA_TQ, Q_PAD), lambda b, h, t: (b * nt + t, h)),
                  pl.BlockSpec((1, 1, nkeys, Q_PAD), lambda b, h, t: (b, h, 0, 0)),
                  pl.BlockSpec((1, 1, nkeys, V_DIM), lambda b, h, t: (b, h, 0, 0))],
        out_specs=pl.BlockSpec((MLA_TQ, V_DIM), lambda b, h, t: (b * nt + t, h)),
        out_shape=jax.ShapeDtypeStruct((batch * seq, MLA_WIDTH), BF16),
        compiler_params=_cparams(("arbitrary", "arbitrary", "arbitrary")),
        name="sample_latent_attention",
    )(q, kf, v)


def _post_kernel(xp_ref, xs_ref, onp_ref, omp_ref, ons_ref, oms_ref, mod_ref, wout_ref, gffn_ref, wrh_ref, wrl_ref,
                 br_ref, x1_ref, h2p_ref, idx_ref, gate_ref, *, n_prompt_tiles):
    is_prompt = pl.program_id(0) < n_prompt_tiles
    ga = mod_ref[0, 2:3, :]
    sf = mod_ref[0, 3:4, :]
    scf = mod_ref[0, 4:5, :]
    tm = ROW_TILE
    for r0 in range(0, xp_ref.shape[0], tm):
        rs = slice(r0, r0 + tm)
        x = jnp.where(is_prompt, xp_ref[rs, :], xs_ref[rs, :])
        ona = jnp.where(is_prompt, onp_ref[rs, :], ons_ref[rs, :])
        omla = jnp.where(is_prompt, omp_ref[rs, :], oms_ref[rs, :])
        o = _dot(ona, wout_ref[0:NA_WIDTH, :]) + _dot(omla, wout_ref[NA_WIDTH:NA_WIDTH + MLA_WIDTH, :])
        x1 = x + ga * o
        x1_ref[rs, :] = x1
        h2 = _rms(x1, gffn_ref[...]) * (1.0 + scf) + sf
        h_hi = h2.astype(BF16)
        h_lo = (h2 - h_hi.astype(F32)).astype(BF16)
        logits = _dot(h_hi, wrh_ref[...]) + _dot(h_lo, wrh_ref[...]) + _dot(h_hi, wrl_ref[...]) + br_ref[...]
        lane_e = lax.broadcasted_iota(jnp.int32, logits.shape, 1).astype(F32)
        lane_o = lax.broadcasted_iota(jnp.int32, (tm, LANE), 1)
        idx_out = jnp.zeros((tm, LANE), F32)
        gate_out = jnp.zeros((tm, LANE), F32)
        top0 = None
        denom = jnp.zeros((tm, 1), F32)
        cur = logits
        for k in range(TOP_K):
            mx = jnp.max(cur, axis=-1, keepdims=True)
            ix = jnp.min(jnp.where(cur == mx, lane_e, float(N_EXPERTS)), axis=-1, keepdims=True)
            cur = jnp.where(lane_e == ix, -jnp.inf, cur)
            if k == 0:
                top0 = mx
            e = jnp.exp(mx - top0)
            denom = denom + e
            idx_out = jnp.where(lane_o == k, ix, idx_out)
            gate_out = jnp.where(lane_o == k, e, gate_out)
        idx_ref[rs, :] = idx_out.astype(jnp.int32)
        gate_ref[rs, :] = gate_out / denom
        bits = pltpu.bitcast(h_hi.astype(F32), U32)
        for s in range(PACK_ROWS):
            lo = bits[:, s * LANE:(s + 1) * LANE] >> 16
            hi = bits[:, (s + PACK_ROWS) * LANE:(s + PACK_ROWS + 1) * LANE] & jnp.uint32(0xFFFF0000)
            h2p_ref[pl.ds(r0 * PACK_ROWS + s, tm, stride=PACK_ROWS), :] = hi | lo


def _post_attention(xp, xs, onp, omp, ons, oms, mod, w_out, g_ffn, w_router, b_router, *, mod_row):
    tm = POST_TILE
    wr_hi = w_router.astype(BF16)
    wr_lo = (w_router - wr_hi.astype(F32)).astype(BF16)
    npt = xp.shape[0] // tm
    nst = xs.shape[0] // tm
    n = xp.shape[0] + xs.shape[0]
    pidx = lambda i: (jnp.minimum(i, npt - 1), 0)
    sidx = lambda i: (jnp.maximum(i - npt, 0), 0)
    row = lambda i: (i, 0)
    return pl.pallas_call(
        functools.partial(_post_kernel, n_prompt_tiles=npt),
        grid=(npt + nst,),
        in_specs=[pl.BlockSpec((tm, D_MODEL), pidx), pl.BlockSpec((tm, D_MODEL), sidx),
                  pl.BlockSpec((tm, NA_WIDTH), pidx), pl.BlockSpec((tm, MLA_WIDTH), pidx),
                  pl.BlockSpec((tm, NA_WIDTH), sidx), pl.BlockSpec((tm, MLA_WIDTH), sidx),
                  pl.BlockSpec((1, 6, D_MODEL), lambda i: (mod_row(i), 0, 0)),
                  _const_spec(w_out.shape), _const_spec((1, D_MODEL)),
                  _const_spec(w_router.shape), _const_spec(w_router.shape), _const_spec((1, N_EXPERTS))],
        out_specs=[pl.BlockSpec((tm, D_MODEL), row),
                   pl.BlockSpec((tm * PACK_ROWS, LANE), row),
                   pl.BlockSpec((tm, LANE), row),
                   pl.BlockSpec((tm, LANE), row)],
        out_shape=[jax.ShapeDtypeStruct((n, D_MODEL), F32),
                   jax.ShapeDtypeStruct((n * PACK_ROWS, LANE), U32),
                   jax.ShapeDtypeStruct((n, LANE), jnp.int32),
                   jax.ShapeDtypeStruct((n, LANE), F32)],
        compiler_params=_cparams(("arbitrary",)),
        name="post_attention_router",
    )(xp, xs, onp, omp, ons, oms, mod, w_out, g_ffn, wr_hi, wr_lo, b_router)


def _deinterleave_kernel(w_ref, p_ref, o_ref):
    p = p_ref[...]
    grp = p.shape[0]
    for c in range(w_ref.shape[2] // grp):
        w = w_ref[0, :, c * grp:(c + 1) * grp].astype(BF16)
        o_ref[0, :, c * grp:(c + 1) * grp] = _dot(w, p).astype(BF16)


def _deinterleave_gate_up(w_gate_up):
    ne, d, n2 = w_gate_up.shape
    grp = 2 * LANE
    dst = np.arange(grp)
    src = np.where(dst < LANE, 2 * dst, 2 * (dst - LANE) + 1)
    perm = np.zeros((grp, grp), np.float32)
    perm[src, dst] = 1.0
    tk = 512
    return pl.pallas_call(
        _deinterleave_kernel,
        grid=(ne, d // tk),
        in_specs=[pl.BlockSpec((1, tk, n2), lambda e, k: (e, k, 0)), _const_spec((grp, grp))],
        out_specs=pl.BlockSpec((1, tk, n2), lambda e, k: (e, k, 0)),
        out_shape=jax.ShapeDtypeStruct((ne, d, n2), BF16),
        compiler_params=_cparams(("arbitrary", "arbitrary")),
        name="deinterleave_gate_up",
    )(w_gate_up, jnp.asarray(perm, BF16))


def _dispatch_kernel(act_ref, start_ref, order_ref, h_ref, o_ref, buf_ref, sem):
    i = pl.program_id(0)
    n = pl.num_programs(0)
    blk_rows = DISPATCH_ROWS * PACK_ROWS
    last = order_ref.shape[0] - 1

    def issue(blk, slot):
        start = start_ref[blk]

        def body(r0, carry):
            for u in range(DMA_UNROLL):
                r = r0 * DMA_UNROLL + u
                tok = order_ref[jnp.minimum(start + r, last)] // TOP_K
                src = pl.multiple_of(tok * PACK_ROWS, PACK_ROWS)
                dst = pl.multiple_of(slot * blk_rows + r * PACK_ROWS, PACK_ROWS)
                pltpu.make_async_copy(h_ref.at[pl.ds(src, PACK_ROWS)], buf_ref.at[pl.ds(dst, PACK_ROWS)],
                                      sem.at[slot]).start()
            return carry

        lax.fori_loop(0, DISPATCH_ROWS // DMA_UNROLL, body, 0)

    @pl.when(jnp.logical_and(i == 0, act_ref[0] > 0))
    def _():
        issue(0, 0)

    nxt = jnp.minimum(i + 1, n - 1)

    @pl.when(jnp.logical_and(i + 1 < n, act_ref[nxt] > 0))
    def _():
        issue(nxt, (i + 1) % 2)

    @pl.when(act_ref[i] > 0)
    def _():
        slot = i % 2
        base = pl.multiple_of(slot * blk_rows, blk_rows)
        pltpu.make_async_copy(h_ref.at[pl.ds(0, blk_rows)], buf_ref.at[pl.ds(base, blk_rows)], sem.at[slot]).wait()
        o_ref[...] = buf_ref[pl.ds(base, blk_rows), :]


def _dispatch(blk_active, blk_start, order, h2p, cap):
    nblk = cap // DISPATCH_ROWS
    blk_rows = DISPATCH_ROWS * PACK_ROWS
    return pl.pallas_call(
        _dispatch_kernel,
        grid_spec=pltpu.PrefetchScalarGridSpec(
            num_scalar_prefetch=3,
            grid=(nblk,),
            in_specs=[pl.BlockSpec(memory_space=pl.ANY)],
            out_specs=pl.BlockSpec((blk_rows, LANE), lambda i, a, s, o: (i, 0)),
            scratch_shapes=[pltpu.VMEM((2 * blk_rows, LANE), U32), pltpu.SemaphoreType.DMA((2,))]),
        out_shape=jax.ShapeDtypeStruct((cap * PACK_ROWS, LANE), U32),
        compiler_params=_cparams(("arbitrary",)),
        name="moe_dispatch",
    )(blk_active, blk_start, order, h2p)


def _moe_kernel(ce_ref, nv_ref, xb_ref, ob_ref, x_ref, wgu_ref, bgu_ref, wd_ref, bd_ref, out_ref, xbf_ref, acc_ref):
    c = pl.program_id(0)
    j = pl.program_id(1)
    nv = nv_ref[c]
    nsub = MOE_CHUNK // MOE_SUB

    def unpack(sb):
        for s in range(PACK_ROWS):
            w = x_ref[pl.ds(sb * MOE_SUB * PACK_ROWS + s, MOE_SUB, stride=PACK_ROWS), :]
            lo = pltpu.bitcast(w << 16, F32)
            hi = pltpu.bitcast(w & jnp.uint32(0xFFFF0000), F32)
            xbf_ref[sb * MOE_SUB:(sb + 1) * MOE_SUB, s * LANE:(s + 1) * LANE] = lo.astype(BF16)
            xbf_ref[sb * MOE_SUB:(sb + 1) * MOE_SUB,
                    (s + PACK_ROWS) * LANE:(s + PACK_ROWS + 1) * LANE] = hi.astype(BF16)

    def for_active_sub_blocks(fn):
        @pl.when(nv == MOE_CHUNK)
        def _():
            for sb in range(nsub):
                fn(sb)

        @pl.when(nv < MOE_CHUNK)
        def _():
            for sb in range(nsub):
                @pl.when(sb * MOE_SUB < nv)
                def _():
                    fn(sb)

    @pl.when(jnp.logical_and(j == 0, nv > 0))
    def _():
        acc_ref[...] = jnp.broadcast_to(bd_ref[0], (MOE_CHUNK, D_MODEL))
        for_active_sub_blocks(unpack)

    wd = wd_ref[0].astype(BF16)

    def sub_block(sb):
        x = xbf_ref[sb * MOE_SUB:(sb + 1) * MOE_SUB, :]
        gu = _dot(x, wgu_ref[0]) + bgu_ref[0]
        ng = MOE_TF // LANE
        g = jnp.concatenate([gu[:, 2 * b * LANE:(2 * b + 1) * LANE] for b in range(ng)], axis=-1)
        u = jnp.concatenate([gu[:, (2 * b + 1) * LANE:(2 * b + 2) * LANE] for b in range(ng)], axis=-1)
        g = jnp.minimum(g, SWIGLU_LIMIT)
        u = jnp.clip(u, -SWIGLU_LIMIT, SWIGLU_LIMIT)
        act = g * (1.0 / (1.0 + jnp.exp(-(g * SWIGLU_ALPHA)))) * (u + 1.0)
        acc_ref[sb * MOE_SUB:(sb + 1) * MOE_SUB, :] += _dot(act.astype(BF16), wd)

    for_active_sub_blocks(sub_block)

    @pl.when(jnp.logical_and(j == pl.num_programs(1) - 1, nv > 0))
    def _():
        for s in range(OUT_ROWS):
            out_ref[pl.ds(s, MOE_CHUNK, stride=OUT_ROWS), :] = acc_ref[:, s * LANE:(s + 1) * LANE]


def _moe(chunk_expert, chunk_rows, x_blk, out_blk, xs, wgu, bgu, w_down, b_down, nch):
    nj = D_FF // MOE_TF

    def jj(c, j, nv):
        return jnp.where(nv[c] > 0, j, nj - 1)

    return pl.pallas_call(
        _moe_kernel,
        grid_spec=pltpu.PrefetchScalarGridSpec(
            num_scalar_prefetch=4,
            grid=(nch, nj),
            in_specs=[
                pl.BlockSpec((MOE_CHUNK * PACK_ROWS, LANE), lambda c, j, ce, nv, xb, ob: (xb[c], 0)),
                pl.BlockSpec((1, D_MODEL, 2 * MOE_TF), lambda c, j, ce, nv, xb, ob: (ce[c], 0, jj(c, j, nv))),
                pl.BlockSpec((1, 1, 2 * MOE_TF), lambda c, j, ce, nv, xb, ob: (ce[c], 0, jj(c, j, nv))),
                pl.BlockSpec((1, MOE_TF, D_MODEL), lambda c, j, ce, nv, xb, ob: (ce[c], jj(c, j, nv), 0)),
                pl.BlockSpec((1, 1, D_MODEL), lambda c, j, ce, nv, xb, ob: (ce[c], 0, 0))],
            out_specs=pl.BlockSpec((MOE_CHUNK * OUT_ROWS, LANE), lambda c, j, ce, nv, xb, ob: (ob[c], 0)),
            scratch_shapes=[pltpu.VMEM((MOE_CHUNK, D_MODEL), BF16), pltpu.VMEM((MOE_CHUNK, D_MODEL), F32)]),
        out_shape=jax.ShapeDtypeStruct(((nch + 1) * MOE_CHUNK * OUT_ROWS, LANE), F32),
        compiler_params=_cparams(("arbitrary", "arbitrary")),
        name="moe_experts",
    )(chunk_expert, chunk_rows, x_blk, out_blk, xs, wgu, bgu, w_down, b_down)


def _combine_kernel(idx0_ref, idx1_ref, gate_ref, x1_ref, mod_ref, gfin_ref, os_ref, y_ref, buf_ref, sem):
    i = pl.program_id(0)
    n = pl.num_programs(0)
    nrow = COMBINE_TOKENS * TOP_K
    slot_rows = nrow * OUT_ROWS

    def issue(idx_ref, slot):
        def body(r0, carry):
            for u in range(DMA_UNROLL):
                r = r0 * DMA_UNROLL + u
                src = pl.multiple_of(idx_ref[0, 0, r] * OUT_ROWS, OUT_ROWS)
                dst = pl.multiple_of(slot * slot_rows + r * OUT_ROWS, OUT_ROWS)
                pltpu.make_async_copy(os_ref.at[pl.ds(src, OUT_ROWS)], buf_ref.at[pl.ds(dst, OUT_ROWS)],
                                      sem.at[slot]).start()
            return carry

        lax.fori_loop(0, nrow // DMA_UNROLL, body, 0)

    @pl.when(i == 0)
    def _():
        issue(idx0_ref, 0)

    @pl.when(i + 1 < n)
    def _():
        issue(idx1_ref, (i + 1) % 2)

    slot = i % 2
    base = pl.multiple_of(slot * slot_rows, slot_rows)
    pltpu.make_async_copy(os_ref.at[pl.ds(0, slot_rows)], buf_ref.at[pl.ds(base, slot_rows)], sem.at[slot]).wait()
    gate = gate_ref[...]
    pieces = []
    for s in range(OUT_ROWS):
        acc = None
        for k in range(TOP_K):
            rows = buf_ref[pl.ds(base + k * OUT_ROWS + s, COMBINE_TOKENS, stride=TOP_K * OUT_ROWS), :]
            term = gate[:, k:k + 1] * rows
            acc = term if acc is None else acc + term
        pieces.append(acc)
    y = jnp.concatenate(pieces, axis=-1)
    gf = mod_ref[0, 5:6, :]
    y_ref[...] = _rms(x1_ref[...] + gf * y, gfin_ref[...])


def _combine(dest, gates, x1, mod, g_final, out_sorted, *, tile0, ntiles, mod_row):
    tt = COMBINE_TOKENS
    nrow = tt * TOP_K
    total_tiles = dest.shape[0]
    return pl.pallas_call(
        _combine_kernel,
        grid=(ntiles,),
        in_specs=[pl.BlockSpec((1, 1, nrow), lambda i: (tile0 + i, 0, 0), memory_space=pltpu.SMEM),
                  pl.BlockSpec((1, 1, nrow), lambda i: (jnp.minimum(tile0 + i + 1, total_tiles - 1), 0, 0),
                               memory_space=pltpu.SMEM),
                  pl.BlockSpec((tt, LANE), lambda i: (tile0 + i, 0)),
                  pl.BlockSpec((tt, D_MODEL), lambda i: (tile0 + i, 0)),
                  pl.BlockSpec((1, 6, D_MODEL), lambda i: (mod_row(i), 0, 0)),
                  _const_spec((1, D_MODEL)),
                  pl.BlockSpec(memory_space=pl.ANY)],
        out_specs=pl.BlockSpec((tt, D_MODEL), lambda i: (i, 0)),
        out_shape=jax.ShapeDtypeStruct((ntiles * tt, D_MODEL), F32),
        scratch_shapes=[pltpu.VMEM((2 * nrow * OUT_ROWS, LANE), F32), pltpu.SemaphoreType.DMA((2,))],
        compiler_params=_cparams(("arbitrary",)),
        name="moe_combine_final_norm",
    )(dest, dest, gates, x1, mod, g_final, out_sorted)


def _routing(top_idx, nch):
    n = top_idx.shape[0]
    nk = n * TOP_K
    flat_e = top_idx.reshape(nk)
    order = jnp.argsort(flat_e, stable=True).astype(jnp.int32)
    inv = jnp.argsort(order).astype(jnp.int32)
    onehot = flat_e[:, None] == jnp.arange(N_EXPERTS, dtype=jnp.int32)[None, :]
    counts = jnp.sum(onehot.astype(jnp.int32), axis=0)
    grp_start = jnp.cumsum(counts) - counts
    chunks_e = (counts + MOE_CHUNK - 1) // MOE_CHUNK
    chunk_end = jnp.cumsum(chunks_e)
    chunk_start = chunk_end - chunks_e
    slot_start = chunk_start * MOE_CHUNK
    dest = inv + jnp.sum(jnp.where(onehot, (slot_start - grp_start)[None, :], 0), axis=1)

    total = chunk_end[-1]
    cidx = jnp.arange(nch, dtype=jnp.int32)
    active = cidx < total
    ce = jnp.minimum(jnp.searchsorted(chunk_end, cidx, side='right'), N_EXPERTS - 1).astype(jnp.int32)
    nv = jnp.clip(counts[ce] - (cidx - chunk_start[ce]) * MOE_CHUNK, 0, MOE_CHUNK)
    nv = jnp.where(active, nv, 0).astype(jnp.int32)
    last = jnp.maximum(total - 1, 0)
    chunk_expert = jnp.where(active, ce, ce[last]).astype(jnp.int32)
    x_blk = jnp.where(active, cidx, last).astype(jnp.int32)
    out_blk = jnp.where(active, cidx, nch).astype(jnp.int32)

    per = MOE_CHUNK // DISPATCH_ROWS
    sub = jnp.tile(jnp.arange(per, dtype=jnp.int32) * DISPATCH_ROWS, nch)
    blk_valid = jnp.clip(jnp.repeat(nv, per) - sub, 0, DISPATCH_ROWS)
    blk_start = jnp.repeat(grp_start[ce] + (cidx - chunk_start[ce]) * MOE_CHUNK, per) + sub
    blk_start = jnp.clip(jnp.where(blk_valid > 0, blk_start, 0), 0, nk - 1).astype(jnp.int32)
    blk_active = (blk_valid > 0).astype(jnp.int32)
    return dest.astype(jnp.int32), order, blk_start, blk_active, chunk_expert, nv, x_blk, out_blk


def _rope_tables(t):
    pos = jnp.arange(t)
    rows = (pos // GRID_W).astype(F32)
    cols = (pos % GRID_W).astype(F32)
    inv = ROPE_THETA ** (-(jnp.arange(ROPE_AXIS // 2, dtype=F32) * 2.0 / ROPE_AXIS))
    ar = rows[:, None] * inv
    ac = cols[:, None] * inv
    ang = jnp.concatenate([ar, ar, ac, ac], axis=-1)
    return jnp.cos(ang), jnp.sin(ang)


def _rot_cols(w):
    half = ROPE_AXIS // 2
    src = np.concatenate([np.arange(half, ROPE_AXIS), np.arange(0, half),
                          np.arange(ROPE_AXIS + half, 2 * ROPE_AXIS), np.arange(ROPE_AXIS, ROPE_AXIS + half)])
    sign = np.concatenate([-np.ones(half), np.ones(half), -np.ones(half), np.ones(half)]).astype(np.float32)
    return w[..., src] * sign


def kernel(x_prompt, x_sample, cache_na_k, cache_na_v, cache_mla_ckv, cache_mla_krope, c, c_ctx, g_attn, g_ffn, g_final, w_mod, b_mod, w_in, w_out, na_rpb, g_q_a, w_q_b, g_kv_a, w_kv_b, w_router, b_router, w_gate_up, b_gate_up, w_down, b_down):
    bp, sp, d = x_prompt.shape
    bd, td, _ = x_sample.shape
    assert d == D_MODEL and w_mod.shape[0] == 1, "one trunk layer of width D_MODEL"
    n_p = bp * sp
    n_s = bd * td
    xp = x_prompt.reshape(n_p, d)
    xs = x_sample.reshape(n_s, d)

    c8 = jnp.zeros((8, d), F32).at[0].set(c_ctx).at[1:1 + bd].set(c)
    mod = _modulation(c8, w_mod[0], b_mod[0].reshape(1, -1)).reshape(8, 6, d)

    w_in0 = w_in[0]
    w_kr = w_in0[:, KR_OFF:KR_OFF + QK_ROPE]
    w_in_p = jnp.concatenate([w_in0, jnp.zeros((d, LANE - QK_ROPE), F32)], axis=1).astype(BF16)
    w_in_s = jnp.concatenate([w_in0, _rot_cols(w_kr)], axis=1).astype(BF16)
    wq = w_q_b[0].reshape(Q_LORA, MLA_HEADS, QK_NOPE + QK_ROPE)
    zpad = jnp.zeros((Q_LORA, MLA_HEADS, Q_PAD - QK_NOPE - QK_ROPE), F32)
    wq_pad = jnp.concatenate([wq, zpad], axis=-1).reshape(Q_LORA, MLA_HEADS * Q_PAD)
    wq_rot = jnp.concatenate([jnp.zeros((Q_LORA, MLA_HEADS, QK_NOPE), F32), _rot_cols(wq[..., QK_NOPE:]), zpad],
                             axis=-1).reshape(Q_LORA, MLA_HEADS * Q_PAD)
    wqb_p = wq_pad.astype(BF16)
    wqb_s = jnp.concatenate([wq_pad, wq_rot], axis=1).astype(BF16)
    w_kvb = w_kv_b[0].astype(BF16)
    cos, sin = _rope_tables(td)
    cosq = jnp.concatenate([jnp.ones((td, QK_NOPE), F32), cos, jnp.ones((td, Q_PAD - QK_NOPE - QK_ROPE), F32)], axis=1)
    sinq = jnp.concatenate([jnp.zeros((td, QK_NOPE), F32), sin, jnp.zeros((td, Q_PAD - QK_NOPE - QK_ROPE), F32)], axis=1)
    cosq = cosq * MLA_QSCALE
    sinq = sinq * MLA_QSCALE
    csk = jnp.concatenate([cos, sin], axis=1)

    tiles_per_seq = td // ROW_TILE
    g_attn2 = g_attn[0].reshape(1, d)
    gq2 = g_q_a[0].reshape(1, Q_LORA)
    gkv2 = g_kv_a[0].reshape(1, KV_LORA)

    naq_p, nak_p, nav_p, q_p, ckv_p, kr_p, krp_p = _pre_attention(
        xp, mod, g_attn2, w_in_p, gq2, wqb_p, gkv2, None, rope=False, mod_row=lambda i: 0)
    ona_p, omla_p = _prompt_attention(naq_p, nak_p, nav_p, q_p, ckv_p, krp_p, w_kvb, sp)

    naq_s, nak_s, nav_s, q_s, ckv_s, krp_s = _pre_attention(
        xs, mod, g_attn2, w_in_s, gq2, wqb_s, gkv2, (cosq, sinq, csk), rope=True,
        mod_row=lambda i: 1 + i // tiles_per_seq)
    past = cache_na_k.shape[2]
    kc = cache_na_k[:, 0].reshape(bd, past, NA_WIDTH)
    vc = cache_na_v[:, 0].reshape(bd, past, NA_WIDTH)
    bias = _na_bias_table(na_rpb[0] * LOG2E, td // GRID_W)
    ona_s = _sample_na(naq_s, nak_s, nav_s, kc, vc, bias, bd, td)
    ckv_all = jnp.concatenate([ckv_s.reshape(bd, td, KV_LORA), cache_mla_ckv[:, 0].astype(BF16)], axis=1)
    krp_c = jnp.concatenate([cache_mla_krope[:, 0], jnp.zeros((bd, past, LANE - QK_ROPE), F32)], axis=-1).astype(BF16)
    krp_all = jnp.concatenate([krp_s.reshape(bd, td, LANE), krp_c], axis=1)
    kf, vv = _kv_expand(ckv_all, krp_all, w_kvb)
    omla_s = _sample_mla(q_s, kf, vv, bd, td)

    npt = n_p // POST_TILE
    post_tiles_per_seq = td // POST_TILE
    x1, h2p, idx128, gate128 = _post_attention(
        xp, xs, ona_p, omla_p, ona_s, omla_s, mod, w_out[0].astype(BF16), g_ffn[0].reshape(1, d),
        w_router[0], b_router[0].reshape(1, N_EXPERTS),
        mod_row=lambda i: jnp.where(i < npt, 0, 1 + jnp.maximum(i - npt, 0) // post_tiles_per_seq))

    n = n_p + n_s
    nch = n * TOP_K // MOE_CHUNK + N_EXPERTS
    dest, order, blk_start, blk_active, chunk_expert, chunk_rows, x_blk, out_blk = _routing(idx128[:, :TOP_K], nch)
    x_sorted = _dispatch(blk_active, blk_start, order, h2p, nch * MOE_CHUNK)
    wgu = _deinterleave_gate_up(w_gate_up[0])
    ng = 2 * D_FF // (2 * LANE)
    bgu = b_gate_up[0].reshape(N_EXPERTS, ng, LANE, 2).transpose(0, 1, 3, 2).reshape(N_EXPERTS, 1, 2 * D_FF)
    out_sorted = _moe(chunk_expert, chunk_rows, x_blk, out_blk, x_sorted, wgu, bgu, w_down[0],
                      b_down[0].reshape(N_EXPERTS, 1, d), nch)

    tt = COMBINE_TOKENS
    dest3 = dest.reshape(n // tt, 1, tt * TOP_K)
    gfin = g_final.reshape(1, d)
    ctiles_seq = td // tt
    y_p = _combine(dest3, gate128, x1, mod, gfin, out_sorted, tile0=0, ntiles=n_p // tt, mod_row=lambda i: 0)
    y_s = _combine(dest3, gate128, x1, mod, gfin, out_sorted, tile0=n_p // tt, ntiles=n_s // tt,
                   mod_row=lambda i: 1 + i // ctiles_seq)

    return (y_p.reshape(bp, sp, d), y_s.reshape(bd, td, d),
            nak_p.reshape(bp, 1, sp, NA_HEADS, NA_HEAD_DIM), nav_p.reshape(bp, 1, sp, NA_HEADS, NA_HEAD_DIM),
            ckv_p.reshape(bp, 1, sp, KV_LORA), kr_p.reshape(bp, 1, sp, QK_ROPE))
```

```python
import functools

import numpy as np
import jax
import jax.numpy as jnp
from jax import lax
from jax.experimental import pallas as pl
from jax.experimental.pallas import tpu as pltpu

F32 = jnp.float32
BF16 = jnp.bfloat16
U32 = jnp.uint32

D_MODEL = 2048
GRID_W = 64
NA_HEADS = 8
NA_HEAD_DIM = 128
NA_KR = 8
NA_KC = 16
MLA_HEADS = 8
Q_LORA = 512
KV_LORA = 256
QK_NOPE = 128
QK_ROPE = 64
V_DIM = 128
ROPE_AXIS = QK_ROPE // 2
ROPE_THETA = 10000.0
NA_WIDTH = NA_HEADS * NA_HEAD_DIM
MLA_WIDTH = MLA_HEADS * V_DIM
IN_COLS = 3 * NA_WIDTH + Q_LORA + KV_LORA + QK_ROPE
N_EXPERTS = 32
TOP_K = 4
D_FF = D_MODEL
SWIGLU_ALPHA = 1.702
SWIGLU_LIMIT = 7.0
EPS = 1e-6

LANE = 128
Q_PAD = 2 * LANE
KR_OFF = 3 * NA_WIDTH + Q_LORA + KV_LORA
IN_COLS_PAD = KR_OFF + LANE
VMEM_LIMIT = 56 * 1024 * 1024
NEG_BIG = -1e30

LOG2E = 1.4426950408889634
NA_QSCALE = NA_HEAD_DIM ** -0.5 * LOG2E
MLA_QSCALE = (QK_NOPE + QK_ROPE) ** -0.5 * LOG2E

ROW_TILE = 256
POST_TILE = 2 * ROW_TILE
NA_Q_ROWS = 4
NA_WIN_ROWS = 12
MLA_TQ = 512
MLA_TK = 1152
MOE_CHUNK = 1024
MOE_SUB = 256
MOE_TF = 512
COMBINE_TOKENS = 128
TOP_K_SHIFT = TOP_K.bit_length() - 1
DMA_UNROLL = 8
PACK_ROWS = D_MODEL // (2 * LANE)
OUT_ROWS = D_MODEL // LANE


def _cparams(sem):
    return pltpu.CompilerParams(dimension_semantics=sem, vmem_limit_bytes=VMEM_LIMIT)


def _const_spec(shape):
    nd = len(shape)
    return pl.BlockSpec(shape, lambda *a: (0,) * nd, pipeline_mode=pl.Buffered(1))


def _rms(x, g):
    return x * lax.rsqrt(jnp.mean(x * x, axis=-1, keepdims=True) + EPS) * g


def _dot(a, b):
    return jnp.dot(a, b, preferred_element_type=F32)


def _dot_nt(a, b):
    return lax.dot_general(a, b, (((1,), (1,)), ((), ())), preferred_element_type=F32)


def _mod_kernel(c_ref, w_ref, b_ref, o_ref):
    c = c_ref[...]
    s = c / (1.0 + jnp.exp(-c))
    o_ref[...] = _dot(s.astype(BF16), w_ref[...].astype(BF16)) + b_ref[...]


def _modulation(c8, w_mod, b_mod):
    n = w_mod.shape[1]
    tn = 1024
    return pl.pallas_call(
        _mod_kernel,
        grid=(n // tn,),
        in_specs=[pl.BlockSpec((8, D_MODEL), lambda j: (0, 0)),
                  pl.BlockSpec((D_MODEL, tn), lambda j: (0, j)),
                  pl.BlockSpec((1, tn), lambda j: (0, j))],
        out_specs=pl.BlockSpec((8, tn), lambda j: (0, j)),
        out_shape=jax.ShapeDtypeStruct((8, n), F32),
        compiler_params=_cparams(("arbitrary",)),
        name="modulation",
    )(c8, w_mod, b_mod)


def _pre_kernel(*refs, rope):
    if rope:
        (x_ref, mod_ref, g_ref, win_ref, gq_ref, wqb_ref, gkv_ref, cosq_ref, sinq_ref, csk_ref,
         naq_ref, nak_ref, nav_ref, q_ref, ckv_ref, krp_ref) = refs
    else:
        (x_ref, mod_ref, g_ref, win_ref, gq_ref, wqb_ref, gkv_ref,
         naq_ref, nak_ref, nav_ref, q_ref, ckv_ref, kr_ref, krp_ref) = refs
    x = x_ref[...]
    sa = mod_ref[0, 0:1, :]
    sca = mod_ref[0, 1:2, :]
    h = (_rms(x, g_ref[...]) * (1.0 + sca) + sa).astype(BF16)
    proj = _dot(h, win_ref[...])
    naq_ref[...] = (proj[:, 0:NA_WIDTH] * NA_QSCALE).astype(naq_ref.dtype)
    nak_ref[...] = proj[:, NA_WIDTH:2 * NA_WIDTH].astype(nak_ref.dtype)
    nav_ref[...] = proj[:, 2 * NA_WIDTH:3 * NA_WIDTH].astype(nav_ref.dtype)
    q_a = proj[:, 3 * NA_WIDTH:3 * NA_WIDTH + Q_LORA]
    kv_a = proj[:, 3 * NA_WIDTH + Q_LORA:KR_OFF]
    krx = proj[:, KR_OFF:IN_COLS_PAD]
    qan = _rms(q_a, gq_ref[...]).astype(BF16)
    qq = _dot(qan, wqb_ref[...])
    ckv_ref[...] = _rms(kv_a, gkv_ref[...]).astype(ckv_ref.dtype)
    if rope:
        width = MLA_HEADS * Q_PAD
        cosq = cosq_ref[...]
        sinq = sinq_ref[...]
        for hd in range(MLA_HEADS):
            a = qq[:, hd * Q_PAD:(hd + 1) * Q_PAD]
            b = qq[:, width + hd * Q_PAD:width + (hd + 1) * Q_PAD]
            q_ref[:, hd * Q_PAD:(hd + 1) * Q_PAD] = (a * cosq + b * sinq).astype(BF16)
        y = krx * csk_ref[...]
        y = y + pltpu.roll(y, QK_ROPE, 1)
        lane = lax.broadcasted_iota(jnp.int32, y.shape, 1)
        krp_ref[...] = jnp.where(lane < QK_ROPE, y, 0.0).astype(BF16)
    else:
        q_ref[...] = (qq * MLA_QSCALE).astype(BF16)
        kr_ref[...] = krx[:, 0:QK_ROPE]
        krp_ref[...] = krx.astype(BF16)


def _pre_attention(x, mod, g_attn, w_in, g_q_a, w_qb, g_kv_a, rope_tabs, *, rope, mod_row):
    n = x.shape[0]
    tm = ROW_TILE
    row = lambda i: (i, 0)
    in_specs = [pl.BlockSpec((tm, D_MODEL), row),
                pl.BlockSpec((1, 6, D_MODEL), lambda i: (mod_row(i), 0, 0)),
                _const_spec((1, D_MODEL)),
                _const_spec(w_in.shape),
                _const_spec((1, Q_LORA)),
                _const_spec(w_qb.shape),
                _const_spec((1, KV_LORA))]
    args = [x, mod, g_attn, w_in, g_q_a, w_qb, g_kv_a]
    qw = MLA_HEADS * Q_PAD
    if rope:
        tiles_per_seq = rope_tabs[0].shape[0] // tm
        pos = lambda i: (i % tiles_per_seq, 0)
        in_specs += [pl.BlockSpec((tm, Q_PAD), pos), pl.BlockSpec((tm, Q_PAD), pos),
                     pl.BlockSpec((tm, LANE), pos)]
        args += list(rope_tabs)
        out_shape = [jax.ShapeDtypeStruct((n, NA_WIDTH), BF16)] * 3 + [
            jax.ShapeDtypeStruct((n, qw), BF16),
            jax.ShapeDtypeStruct((n, KV_LORA), BF16),
            jax.ShapeDtypeStruct((n, LANE), BF16)]
        out_specs = [pl.BlockSpec((tm, NA_WIDTH), row)] * 3 + [
            pl.BlockSpec((tm, qw), row), pl.BlockSpec((tm, KV_LORA), row), pl.BlockSpec((tm, LANE), row)]
    else:
        out_shape = [jax.ShapeDtypeStruct((n, NA_WIDTH), BF16),
                     jax.ShapeDtypeStruct((n, NA_WIDTH), F32),
                     jax.ShapeDtypeStruct((n, NA_WIDTH), F32),
                     jax.ShapeDtypeStruct((n, qw), BF16),
                     jax.ShapeDtypeStruct((n, KV_LORA), F32),
                     jax.ShapeDtypeStruct((n, QK_ROPE), F32),
                     jax.ShapeDtypeStruct((n, LANE), BF16)]
        out_specs = [pl.BlockSpec((tm, NA_WIDTH), row)] * 3 + [
            pl.BlockSpec((tm, qw), row), pl.BlockSpec((tm, KV_LORA), row),
            pl.BlockSpec((tm, QK_ROPE), row), pl.BlockSpec((tm, LANE), row)]
    return pl.pallas_call(
        functools.partial(_pre_kernel, rope=rope),
        grid=(n // tm,),
        in_specs=in_specs,
        out_specs=out_specs,
        out_shape=out_shape,
        compiler_params=_cparams(("arbitrary",)),
        name="pre_attention_rope" if rope else "pre_attention",
    )(*args)


def _softmax_pv(s, v):
    m = jnp.max(s, axis=-1, keepdims=True)
    p = jnp.exp2(s - m)
    l = jnp.sum(p, axis=-1, keepdims=True)
    return _dot(p.astype(BF16), v) / l


def _prompt_attn_kernel(naq_ref, nak_ref, nav_ref, q_ref, ckv_ref, krp_ref, wkvb_ref, ona_ref, omla_ref):
    kv = _dot(ckv_ref[...].astype(BF16), wkvb_ref[...])
    krp = krp_ref[...]
    for hd in range(NA_HEADS):
        sl = slice(hd * NA_HEAD_DIM, (hd + 1) * NA_HEAD_DIM)
        s = _dot_nt(naq_ref[:, sl], nak_ref[:, sl].astype(BF16))
        ona_ref[:, sl] = _softmax_pv(s, nav_ref[:, sl].astype(BF16)).astype(BF16)
    kvw = QK_NOPE + V_DIM
    for hd in range(MLA_HEADS):
        kf = jnp.concatenate([kv[:, hd * kvw:hd * kvw + QK_NOPE].astype(BF16), krp], axis=-1)
        s = _dot_nt(q_ref[:, hd * Q_PAD:(hd + 1) * Q_PAD], kf)
        v = kv[:, hd * kvw + QK_NOPE:(hd + 1) * kvw].astype(BF16)
        omla_ref[:, hd * V_DIM:(hd + 1) * V_DIM] = _softmax_pv(s, v).astype(BF16)


def _prompt_attention(naq, nak, nav, q, ckv, krp, w_kvb, seq):
    n = naq.shape[0]
    row = lambda b: (b, 0)
    return pl.pallas_call(
        _prompt_attn_kernel,
        grid=(n // seq,),
        in_specs=[pl.BlockSpec((seq, NA_WIDTH), row)] * 3 + [
            pl.BlockSpec((seq, MLA_HEADS * Q_PAD), row),
            pl.BlockSpec((seq, KV_LORA), row),
            pl.BlockSpec((seq, LANE), row),
            _const_spec(w_kvb.shape)],
        out_specs=[pl.BlockSpec((seq, NA_WIDTH), row), pl.BlockSpec((seq, MLA_WIDTH), row)],
        out_shape=[jax.ShapeDtypeStruct((n, NA_WIDTH), BF16), jax.ShapeDtypeStruct((n, MLA_WIDTH), BF16)],
        compiler_params=_cparams(("arbitrary",)),
        name="prompt_attention",
    )(naq, nak, nav, q, ckv, krp, w_kvb)


def _na_kernel(q_ref, k_ref, v_ref, kc_ref, vc_ref, bias_ref, o_ref, *, rows):
    rt = pl.program_id(2)
    ws = jnp.clip(rt * NA_Q_ROWS - NA_KR // 2, 0, rows - NA_WIN_ROWS)
    start = pl.multiple_of(ws * GRID_W, GRID_W)
    nwin = NA_WIN_ROWS * GRID_W
    q = q_ref[...]
    s_loc = _dot_nt(q, k_ref[pl.ds(start, nwin), :]) + bias_ref[0, 0]
    s_ctx = _dot_nt(q, kc_ref[0].astype(BF16))
    m = jnp.maximum(jnp.max(s_loc, axis=-1, keepdims=True), jnp.max(s_ctx, axis=-1, keepdims=True))
    p_loc = jnp.exp2(s_loc - m)
    p_ctx = jnp.exp2(s_ctx - m)
    l = jnp.sum(p_loc, axis=-1, keepdims=True) + jnp.sum(p_ctx, axis=-1, keepdims=True)
    o = _dot(p_loc.astype(BF16), v_ref[pl.ds(start, nwin), :]) + _dot(p_ctx.astype(BF16), vc_ref[0].astype(BF16))
    o_ref[...] = (o / l).astype(BF16)


def _na_bias_table(rpb, rows):
    nh = rpb.shape[0]
    a = np.arange(NA_Q_ROWS)
    b = np.arange(NA_WIN_ROWS)
    col = np.arange(GRID_W)
    cs = np.clip(col - NA_KC // 2, 0, GRID_W - NA_KC)
    valid_col = (col[None, :] >= cs[:, None]) & (col[None, :] < cs[:, None] + NA_KC)
    padw = GRID_W - NA_KC
    rp = jnp.pad(rpb, ((0, 0), (0, 0), (padw, padw)))
    toep = jnp.stack([rp[:, :, GRID_W - 1 - qc:2 * GRID_W - 1 - qc] for qc in range(GRID_W)], axis=2)
    toep = jnp.where(valid_col[None, None], toep, NEG_BIG)
    masked = jnp.full((nh, GRID_W, GRID_W), NEG_BIG, F32)
    tabs = []
    for r0 in (0, NA_Q_ROWS, rows - NA_Q_ROWS):
        ws = int(np.clip(r0 - NA_KR // 2, 0, rows - NA_WIN_ROWS))
        r = r0 + a
        rs = np.clip(r - NA_KR // 2, 0, rows - NA_KR)
        kr = ws + b
        valid_row = (kr[None, :] >= rs[:, None]) & (kr[None, :] < rs[:, None] + NA_KR)
        dr = kr[None, :] - r[:, None] + NA_KR - 1
        tile_rows = []
        for ai in range(NA_Q_ROWS):
            blocks = [toep[:, int(dr[ai, bi])] if valid_row[ai, bi] else masked for bi in range(NA_WIN_ROWS)]
            tile_rows.append(jnp.concatenate(blocks, axis=-1))
        tabs.append(jnp.concatenate(tile_rows, axis=1))
    return jnp.stack(tabs)


def _sample_na(naq, nak, nav, kc, vc, bias, batch, seq):
    rows = seq // GRID_W
    tq = NA_Q_ROWS * GRID_W
    nt = seq // tq
    last = nt - 1

    def pat(rt):
        return jnp.where(rt == 0, 0, jnp.where(rt == last, 2, 1))

    return pl.pallas_call(
        functools.partial(_na_kernel, rows=rows),
        grid=(batch, NA_HEADS, nt),
        in_specs=[pl.BlockSpec((tq, NA_HEAD_DIM), lambda b, h, r: (b * nt + r, h)),
                  pl.BlockSpec((seq, NA_HEAD_DIM), lambda b, h, r: (b, h)),
                  pl.BlockSpec((seq, NA_HEAD_DIM), lambda b, h, r: (b, h)),
                  pl.BlockSpec((1, kc.shape[1], NA_HEAD_DIM), lambda b, h, r: (b, 0, h)),
                  pl.BlockSpec((1, vc.shape[1], NA_HEAD_DIM), lambda b, h, r: (b, 0, h)),
                  pl.BlockSpec((1, 1, tq, NA_WIN_ROWS * GRID_W), lambda b, h, r: (pat(r), h, 0, 0))],
        out_specs=pl.BlockSpec((tq, NA_HEAD_DIM), lambda b, h, r: (b * nt + r, h)),
        out_shape=jax.ShapeDtypeStruct((batch * seq, NA_WIDTH), BF16),
        compiler_params=_cparams(("arbitrary", "arbitrary", "arbitrary")),
        name="sample_neighbourhood_attention",
    )(naq, nak, nav, kc, vc, bias)


def _kv_expand_kernel(ckv_ref, krp_ref, w_ref, kf_ref, v_ref):
    kv = _dot(ckv_ref[0], w_ref[...])
    krp = krp_ref[0]
    kvw = QK_NOPE + V_DIM
    for hd in range(MLA_HEADS):
        kf_ref[0, hd, :, 0:QK_NOPE] = kv[:, hd * kvw:hd * kvw + QK_NOPE].astype(BF16)
        kf_ref[0, hd, :, QK_NOPE:Q_PAD] = krp
        v_ref[0, hd] = kv[:, hd * kvw + QK_NOPE:(hd + 1) * kvw].astype(BF16)


def _kv_expand(ckv, krp, w_kvb):
    batch, nkeys, _ = ckv.shape
    tm = 512
    return pl.pallas_call(
        _kv_expand_kernel,
        grid=(batch, nkeys // tm),
        in_specs=[pl.BlockSpec((1, tm, KV_LORA), lambda b, t: (b, t, 0)),
                  pl.BlockSpec((1, tm, LANE), lambda b, t: (b, t, 0)),
                  _const_spec(w_kvb.shape)],
        out_specs=[pl.BlockSpec((1, MLA_HEADS, tm, Q_PAD), lambda b, t: (b, 0, t, 0)),
                   pl.BlockSpec((1, MLA_HEADS, tm, V_DIM), lambda b, t: (b, 0, t, 0))],
        out_shape=[jax.ShapeDtypeStruct((batch, MLA_HEADS, nkeys, Q_PAD), BF16),
                   jax.ShapeDtypeStruct((batch, MLA_HEADS, nkeys, V_DIM), BF16)],
        compiler_params=_cparams(("arbitrary", "arbitrary")),
        name="latent_kv_expand",
    )(ckv, krp, w_kvb)


def _mla_kernel(q_ref, kf_ref, v_ref, o_ref, *, nkeys):
    q = q_ref[...]
    tq = q.shape[0]
    m = jnp.full((tq, 1), NEG_BIG, F32)
    l = jnp.zeros((tq, 1), F32)
    acc = jnp.zeros((tq, V_DIM), F32)
    for c in range(nkeys // MLA_TK):
        ks = slice(c * MLA_TK, (c + 1) * MLA_TK)
        s = _dot_nt(q, kf_ref[0, 0, ks, :])
        m_new = jnp.maximum(m, jnp.max(s, axis=-1, keepdims=True))
        alpha = jnp.exp2(m - m_new)
        p = jnp.exp2(s - m_new)
        l = alpha * l + jnp.sum(p, axis=-1, keepdims=True)
        acc = alpha * acc + _dot(p.astype(BF16), v_ref[0, 0, ks, :])
        m = m_new
    o_ref[...] = (acc / l).astype(BF16)


def _sample_mla(q, kf, v, batch, seq):
    nkeys = kf.shape[2]
    nt = seq // MLA_TQ
    return pl.pallas_call(
        functools.partial(_mla_kernel, nkeys=nkeys),
        grid=(batch, MLA_HEADS, nt),
        in_specs=[pl.BlockSpec((MLA_TQ, Q_PAD), lambda b, h, t: (b * nt + t, h)),
                  pl.BlockSpec((1, 1, nkeys, Q_PAD), lambda b, h, t: (b, h, 0, 0)),
                  pl.BlockSpec((1, 1, nkeys, V_DIM), lambda b, h, t: (b, h, 0, 0))],
        out_specs=pl.BlockSpec((MLA_TQ, V_DIM), lambda b, h, t: (b * nt + t, h)),
        out_shape=jax.ShapeDtypeStruct((batch * seq, MLA_WIDTH), BF16),
        compiler_params=_cparams(("arbitrary", "arbitrary", "arbitrary")),
        name="sample_latent_attention",
    )(q, kf, v)


def _post_kernel(xp_ref, xs_ref, onp_ref, omp_ref, ons_ref, oms_ref, mod_ref, wout_ref, gffn_ref, wrh_ref, wrl_ref,
                 br_ref, x1_ref, h2p_ref, idx_ref, gate_ref, *, n_prompt_tiles):
    is_prompt = pl.program_id(0) < n_prompt_tiles
    ga = mod_ref[0, 2:3, :]
    sf = mod_ref[0, 3:4, :]
    scf = mod_ref[0, 4:5, :]
    tm = ROW_TILE
    for r0 in range(0, xp_ref.shape[0], tm):
        rs = slice(r0, r0 + tm)
        x = jnp.where(is_prompt, xp_ref[rs, :], xs_ref[rs, :])
        ona = jnp.where(is_prompt, onp_ref[rs, :], ons_ref[rs, :])
        omla = jnp.where(is_prompt, omp_ref[rs, :], oms_ref[rs, :])
        o = _dot(ona, wout_ref[0:NA_WIDTH, :]) + _dot(omla, wout_ref[NA_WIDTH:NA_WIDTH + MLA_WIDTH, :])
        x1 = x + ga * o
        x1_ref[rs, :] = x1
        h2 = _rms(x1, gffn_ref[...]) * (1.0 + scf) + sf
        h_hi = h2.astype(BF16)
        h_lo = (h2 - h_hi.astype(F32)).astype(BF16)
        logits = _dot(h_hi, wrh_ref[...]) + _dot(h_lo, wrh_ref[...]) + _dot(h_hi, wrl_ref[...]) + br_ref[...]
        lane_e = lax.broadcasted_iota(jnp.int32, logits.shape, 1).astype(F32)
        lane_o = lax.broadcasted_iota(jnp.int32, (tm, LANE), 1)
        idx_out = jnp.zeros((tm, LANE), F32)
        gate_out = jnp.zeros((tm, LANE), F32)
        top0 = None
        denom = jnp.zeros((tm, 1), F32)
        cur = logits
        for k in range(TOP_K):
            mx = jnp.max(cur, axis=-1, keepdims=True)
            ix = jnp.min(jnp.where(cur == mx, lane_e, float(N_EXPERTS)), axis=-1, keepdims=True)
            cur = jnp.where(lane_e == ix, -jnp.inf, cur)
            if k == 0:
                top0 = mx
            e = jnp.exp(mx - top0)
            denom = denom + e
            idx_out = jnp.where(lane_o == k, ix, idx_out)
            gate_out = jnp.where(lane_o == k, e, gate_out)
        idx_ref[rs, :] = idx_out.astype(jnp.int32)
        gate_ref[rs, :] = gate_out / denom
        bits = pltpu.bitcast(h_hi.astype(F32), U32)
        for s in range(PACK_ROWS):
            lo = bits[:, s * LANE:(s + 1) * LANE] >> 16
            hi = bits[:, (s + PACK_ROWS) * LANE:(s + PACK_ROWS + 1) * LANE] & jnp.uint32(0xFFFF0000)
            h2p_ref[pl.ds(r0 * PACK_ROWS + s, tm, stride=PACK_ROWS), :] = hi | lo


def _post_attention(xp, xs, onp, omp, ons, oms, mod, w_out, g_ffn, w_router, b_router, *, mod_row):
    tm = POST_TILE
    wr_hi = w_router.astype(BF16)
    wr_lo = (w_router - wr_hi.astype(F32)).astype(BF16)
    npt = xp.shape[0] // tm
    nst = xs.shape[0] // tm
    n = xp.shape[0] + xs.shape[0]
    pidx = lambda i: (jnp.minimum(i, npt - 1), 0)
    sidx = lambda i: (jnp.maximum(i - npt, 0), 0)
    row = lambda i: (i, 0)
    return pl.pallas_call(
        functools.partial(_post_kernel, n_prompt_tiles=npt),
        grid=(npt + nst,),
        in_specs=[pl.BlockSpec((tm, D_MODEL), pidx), pl.BlockSpec((tm, D_MODEL), sidx),
                  pl.BlockSpec((tm, NA_WIDTH), pidx), pl.BlockSpec((tm, MLA_WIDTH), pidx),
                  pl.BlockSpec((tm, NA_WIDTH), sidx), pl.BlockSpec((tm, MLA_WIDTH), sidx),
                  pl.BlockSpec((1, 6, D_MODEL), lambda i: (mod_row(i), 0, 0)),
                  _const_spec(w_out.shape), _const_spec((1, D_MODEL)),
                  _const_spec(w_router.shape), _const_spec(w_router.shape), _const_spec((1, N_EXPERTS))],
        out_specs=[pl.BlockSpec((tm, D_MODEL), row),
                   pl.BlockSpec((tm * PACK_ROWS, LANE), row),
                   pl.BlockSpec((tm, LANE), row),
                   pl.BlockSpec((tm, LANE), row)],
        out_shape=[jax.ShapeDtypeStruct((n, D_MODEL), F32),
                   jax.ShapeDtypeStruct((n * PACK_ROWS, LANE), U32),
                   jax.ShapeDtypeStruct((n, LANE), jnp.int32),
                   jax.ShapeDtypeStruct((n, LANE), F32)],
        compiler_params=_cparams(("arbitrary",)),
        name="post_attention_router",
    )(xp, xs, onp, omp, ons, oms, mod, w_out, g_ffn, wr_hi, wr_lo, b_router)


def _deinterleave_kernel(w_ref, p_ref, o_ref):
    p = p_ref[...]
    grp = p.shape[0]
    for c in range(w_ref.shape[2] // grp):
        w = w_ref[0, :, c * grp:(c + 1) * grp].astype(BF16)
        o_ref[0, :, c * grp:(c + 1) * grp] = _dot(w, p).astype(BF16)


def _deinterleave_gate_up(w_gate_up):
    ne, d, n2 = w_gate_up.shape
    grp = 2 * LANE
    dst = np.arange(grp)
    src = np.where(dst < LANE, 2 * dst, 2 * (dst - LANE) + 1)
    perm = np.zeros((grp, grp), np.float32)
    perm[src, dst] = 1.0
    tk = 512
    return pl.pallas_call(
        _deinterleave_kernel,
        grid=(ne, d // tk),
        in_specs=[pl.BlockSpec((1, tk, n2), lambda e, k: (e, k, 0)), _const_spec((grp, grp))],
        out_specs=pl.BlockSpec((1, tk, n2), lambda e, k: (e, k, 0)),
        out_shape=jax.ShapeDtypeStruct((ne, d, n2), BF16),
        compiler_params=_cparams(("arbitrary", "arbitrary")),
        name="deinterleave_gate_up",
    )(w_gate_up, jnp.asarray(perm, BF16))


def _moe_kernel(ce_ref, nv_ref, cs_ref, order_ref, h_ref, wgu_ref, bgu_ref, wd_ref, bd_ref, y_ref,
                xg_ref, xbf_ref, acc_ref, ost_ref, sem_g, sem_s):
    c = pl.program_id(0)
    j = pl.program_id(1)
    last_c = pl.num_programs(0) - 1
    last_j = pl.num_programs(1) - 1
    nk = order_ref.shape[0]
    nsub = MOE_CHUNK // MOE_SUB
    xg_rows = MOE_CHUNK * PACK_ROWS

    nv = nv_ref[c]
    c_next = jnp.minimum(c + 1, last_c)
    nv_next = jnp.where(c < last_c, nv_ref[c_next], 0)
    cs_next = cs_ref[c_next]
    c_prev = jnp.maximum(c - 1, 0)
    nv_prev = jnp.where(c > 0, nv_ref[c_prev], 0)
    cs_prev = cs_ref[c_prev]
    slot = c % 2
    slot_next = (c + 1) % 2

    def gather_row(cs, dst_slot, row):
        tok = jnp.right_shift(order_ref[jnp.minimum(cs + row, nk - 1)], TOP_K_SHIFT)
        src = pl.multiple_of(tok * PACK_ROWS, PACK_ROWS)
        dst = pl.multiple_of(dst_slot * xg_rows + row * PACK_ROWS, PACK_ROWS)
        pltpu.make_async_copy(h_ref.at[pl.ds(src, PACK_ROWS)], xg_ref.at[pl.ds(dst, PACK_ROWS)],
                              sem_g.at[dst_slot]).start()

    def scatter_row(row):
        flat = order_ref[jnp.minimum(cs_prev + row, nk - 1)]
        dest = jnp.where(row < nv_prev, flat, nk + row)
        src = pl.multiple_of(row * OUT_ROWS, OUT_ROWS)
        dst = pl.multiple_of(dest * OUT_ROWS, OUT_ROWS)
        pltpu.make_async_copy(ost_ref.at[pl.ds(src, OUT_ROWS)], y_ref.at[pl.ds(dst, OUT_ROWS)], sem_s.at[0]).start()

    def rolled(fn, row0, nrows):
        def body(r0, carry):
            for u in range(DMA_UNROLL):
                fn(row0 + r0 * DMA_UNROLL + u)
            return carry

        lax.fori_loop(0, nrows // DMA_UNROLL, body, 0)

    @pl.when(jnp.logical_and(jnp.logical_and(c == 0, j == 0), nv > 0))
    def _():
        rolled(lambda row: gather_row(cs_ref[0], 0, row), 0, MOE_CHUNK)

    def unpack(sb):
        for s in range(PACK_ROWS):
            w = xg_ref[pl.ds(slot * xg_rows + sb * MOE_SUB * PACK_ROWS + s, MOE_SUB, stride=PACK_ROWS), :]
            lo = pltpu.bitcast(w << 16, F32)
            hi = pltpu.bitcast(w & jnp.uint32(0xFFFF0000), F32)
            xbf_ref[sb * MOE_SUB:(sb + 1) * MOE_SUB, s * LANE:(s + 1) * LANE] = lo.astype(BF16)
            xbf_ref[sb * MOE_SUB:(sb + 1) * MOE_SUB,
                    (s + PACK_ROWS) * LANE:(s + PACK_ROWS + 1) * LANE] = hi.astype(BF16)

    @pl.when(jnp.logical_and(j == 0, nv > 0))
    def _():
        base = pl.multiple_of(slot * xg_rows, xg_rows)
        pltpu.make_async_copy(h_ref.at[pl.ds(0, xg_rows)], xg_ref.at[pl.ds(base, xg_rows)], sem_g.at[slot]).wait()
        acc_ref[...] = jnp.broadcast_to(bd_ref[0], (MOE_CHUNK, D_MODEL))

        @pl.when(nv == MOE_CHUNK)
        def _():
            for sb in range(nsub):
                unpack(sb)

        @pl.when(nv < MOE_CHUNK)
        def _():
            for sb in range(nsub):
                @pl.when(sb * MOE_SUB < nv)
                def _():
                    unpack(sb)

    wd = wd_ref[0].astype(BF16)

    def sub_block(sb):
        x = xbf_ref[sb * MOE_SUB:(sb + 1) * MOE_SUB, :]
        gu = _dot(x, wgu_ref[0]) + bgu_ref[0]
        ng = MOE_TF // LANE
        g = jnp.concatenate([gu[:, 2 * b * LANE:(2 * b + 1) * LANE] for b in range(ng)], axis=-1)
        u = jnp.concatenate([gu[:, (2 * b + 1) * LANE:(2 * b + 2) * LANE] for b in range(ng)], axis=-1)
        g = jnp.minimum(g, SWIGLU_LIMIT)
        u = jnp.clip(u, -SWIGLU_LIMIT, SWIGLU_LIMIT)
        act = g * (1.0 / (1.0 + jnp.exp(-(g * SWIGLU_ALPHA)))) * (u + 1.0)
        acc_ref[sb * MOE_SUB:(sb + 1) * MOE_SUB, :] += _dot(act.astype(BF16), wd)

    fast = jnp.logical_and(nv == MOE_CHUNK, jnp.logical_and(nv_next > 0, nv_prev > 0))
    quarter = j * MOE_SUB
    per_sb = MOE_SUB // nsub

    @pl.when(fast)
    def _():
        for sb in range(nsub):
            for r in range(per_sb):
                gather_row(cs_next, slot_next, quarter + sb * per_sb + r)
            for r in range(per_sb):
                scatter_row(quarter + sb * per_sb + r)
            sub_block(sb)

    @pl.when(jnp.logical_not(fast))
    def _():
        @pl.when(nv_next > 0)
        def _():
            rolled(lambda row: gather_row(cs_next, slot_next, row), quarter, MOE_SUB)

        @pl.when(nv_prev > 0)
        def _():
            rolled(scatter_row, quarter, MOE_SUB)

        for sb in range(nsub):
            @pl.when(sb * MOE_SUB < nv)
            def _():
                sub_block(sb)

    @pl.when(j == last_j)
    def _():
        @pl.when(nv_prev > 0)
        def _():
            pltpu.make_async_copy(ost_ref, y_ref.at[pl.ds(0, MOE_CHUNK * OUT_ROWS)], sem_s.at[0]).wait()

        @pl.when(nv > 0)
        def _():
            for s in range(OUT_ROWS):
                ost_ref[pl.ds(s, MOE_CHUNK, stride=OUT_ROWS), :] = acc_ref[:, s * LANE:(s + 1) * LANE]


def _moe(chunk_expert, chunk_rows, chunk_start, order, h2p, wgu, bgu, w_down, b_down):
    nj = D_FF // MOE_TF
    ngrid = chunk_expert.shape[0]
    nk = order.shape[0]

    def jj(c, j, nv):
        return jnp.where(nv[c] > 0, j, nj - 1)

    return pl.pallas_call(
        _moe_kernel,
        grid_spec=pltpu.PrefetchScalarGridSpec(
            num_scalar_prefetch=4,
            grid=(ngrid, nj),
            in_specs=[
                pl.BlockSpec(memory_space=pl.ANY),
                pl.BlockSpec((1, D_MODEL, 2 * MOE_TF), lambda c, j, ce, nv, cs, od: (ce[c], 0, jj(c, j, nv))),
                pl.BlockSpec((1, 1, 2 * MOE_TF), lambda c, j, ce, nv, cs, od: (ce[c], 0, jj(c, j, nv))),
                pl.BlockSpec((1, MOE_TF, D_MODEL), lambda c, j, ce, nv, cs, od: (ce[c], jj(c, j, nv), 0)),
                pl.BlockSpec((1, 1, D_MODEL), lambda c, j, ce, nv, cs, od: (ce[c], 0, 0))],
            out_specs=pl.BlockSpec(memory_space=pl.ANY),
            scratch_shapes=[pltpu.VMEM((2 * MOE_CHUNK * PACK_ROWS, LANE), U32),
                            pltpu.VMEM((MOE_CHUNK, D_MODEL), BF16),
                            pltpu.VMEM((MOE_CHUNK, D_MODEL), F32),
                            pltpu.VMEM((MOE_CHUNK * OUT_ROWS, LANE), F32),
                            pltpu.SemaphoreType.DMA((2,)),
                            pltpu.SemaphoreType.DMA((1,))]),
        out_shape=jax.ShapeDtypeStruct(((nk + MOE_CHUNK) * OUT_ROWS, LANE), F32),
        compiler_params=_cparams(("arbitrary", "arbitrary")),
        name="moe_experts",
    )(chunk_expert, chunk_rows, chunk_start, order, h2p, wgu, bgu, w_down, b_down)


def _combine_kernel(gate_ref, x1_ref, mod_ref, gfin_ref, ye_ref, y_ref):
    gate = gate_ref[...]
    pieces = []
    for s in range(OUT_ROWS):
        acc = None
        for k in range(TOP_K):
            rows = ye_ref[pl.ds(k * OUT_ROWS + s, COMBINE_TOKENS, stride=TOP_K * OUT_ROWS), :]
            term = gate[:, k:k + 1] * rows
            acc = term if acc is None else acc + term
        pieces.append(acc)
    y = jnp.concatenate(pieces, axis=-1)
    gf = mod_ref[0, 5:6, :]
    y_ref[...] = _rms(x1_ref[...] + gf * y, gfin_ref[...])


def _combine(gates, x1, mod, g_final, y_experts, *, tile0, ntiles, mod_row):
    tt = COMBINE_TOKENS
    return pl.pallas_call(
        _combine_kernel,
        grid=(ntiles,),
        in_specs=[pl.BlockSpec((tt, LANE), lambda i: (tile0 + i, 0)),
                  pl.BlockSpec((tt, D_MODEL), lambda i: (tile0 + i, 0)),
                  pl.BlockSpec((1, 6, D_MODEL), lambda i: (mod_row(i), 0, 0)),
                  _const_spec((1, D_MODEL)),
                  pl.BlockSpec((tt * TOP_K * OUT_ROWS, LANE), lambda i: (tile0 + i, 0))],
        out_specs=pl.BlockSpec((tt, D_MODEL), lambda i: (i, 0)),
        out_shape=jax.ShapeDtypeStruct((ntiles * tt, D_MODEL), F32),
        compiler_params=_cparams(("arbitrary",)),
        name="moe_combine_final_norm",
    )(gates, x1, mod, g_final, y_experts)


def _routing(top_idx, nch):
    n = top_idx.shape[0]
    nk = n * TOP_K
    flat_e = top_idx.reshape(nk)
    order = jnp.argsort(flat_e, stable=True).astype(jnp.int32)
    onehot = flat_e[:, None] == jnp.arange(N_EXPERTS, dtype=jnp.int32)[None, :]
    counts = jnp.sum(onehot.astype(jnp.int32), axis=0)
    grp_start = jnp.cumsum(counts) - counts
    chunks_e = (counts + MOE_CHUNK - 1) // MOE_CHUNK
    chunk_end = jnp.cumsum(chunks_e)
    chunk_start = chunk_end - chunks_e

    total = chunk_end[-1]
    cidx = jnp.arange(nch + 1, dtype=jnp.int32)
    active = cidx < total
    ce = jnp.minimum(jnp.searchsorted(chunk_end, cidx, side='right'), N_EXPERTS - 1).astype(jnp.int32)
    local = (cidx - chunk_start[ce]) * MOE_CHUNK
    nv = jnp.where(active, jnp.clip(counts[ce] - local, 0, MOE_CHUNK), 0).astype(jnp.int32)
    last = jnp.maximum(total - 1, 0)
    chunk_expert = jnp.where(active, ce, ce[last]).astype(jnp.int32)
    chunk_first = jnp.where(active, jnp.clip(grp_start[ce] + local, 0, nk - 1), 0).astype(jnp.int32)
    return order, chunk_expert, nv, chunk_first


def _rope_tables(t):
    pos = jnp.arange(t)
    rows = (pos // GRID_W).astype(F32)
    cols = (pos % GRID_W).astype(F32)
    inv = ROPE_THETA ** (-(jnp.arange(ROPE_AXIS // 2, dtype=F32) * 2.0 / ROPE_AXIS))
    ar = rows[:, None] * inv
    ac = cols[:, None] * inv
    ang = jnp.concatenate([ar, ar, ac, ac], axis=-1)
    return jnp.cos(ang), jnp.sin(ang)


def _rot_cols(w):
    half = ROPE_AXIS // 2
    src = np.concatenate([np.arange(half, ROPE_AXIS), np.arange(0, half),
                          np.arange(ROPE_AXIS + half, 2 * ROPE_AXIS), np.arange(ROPE_AXIS, ROPE_AXIS + half)])
    sign = np.concatenate([-np.ones(half), np.ones(half), -np.ones(half), np.ones(half)]).astype(np.float32)
    return w[..., src] * sign


def kernel(x_prompt, x_sample, cache_na_k, cache_na_v, cache_mla_ckv, cache_mla_krope, c, c_ctx, g_attn, g_ffn, g_final, w_mod, b_mod, w_in, w_out, na_rpb, g_q_a, w_q_b, g_kv_a, w_kv_b, w_router, b_router, w_gate_up, b_gate_up, w_down, b_down):
    bp, sp, d = x_prompt.shape
    bd, td, _ = x_sample.shape
    assert d == D_MODEL and w_mod.shape[0] == 1, "one trunk layer of width D_MODEL"
    n_p = bp * sp
    n_s = bd * td
    xp = x_prompt.reshape(n_p, d)
    xs = x_sample.reshape(n_s, d)

    c8 = jnp.zeros((8, d), F32).at[0].set(c_ctx).at[1:1 + bd].set(c)
    mod = _modulation(c8, w_mod[0], b_mod[0].reshape(1, -1)).reshape(8, 6, d)

    w_in0 = w_in[0]
    w_kr = w_in0[:, KR_OFF:KR_OFF + QK_ROPE]
    w_in_p = jnp.concatenate([w_in0, jnp.zeros((d, LANE - QK_ROPE), F32)], axis=1).astype(BF16)
    w_in_s = jnp.concatenate([w_in0, _rot_cols(w_kr)], axis=1).astype(BF16)
    wq = w_q_b[0].reshape(Q_LORA, MLA_HEADS, QK_NOPE + QK_ROPE)
    zpad = jnp.zeros((Q_LORA, MLA_HEADS, Q_PAD - QK_NOPE - QK_ROPE), F32)
    wq_pad = jnp.concatenate([wq, zpad], axis=-1).reshape(Q_LORA, MLA_HEADS * Q_PAD)
    wq_rot = jnp.concatenate([jnp.zeros((Q_LORA, MLA_HEADS, QK_NOPE), F32), _rot_cols(wq[..., QK_NOPE:]), zpad],
                             axis=-1).reshape(Q_LORA, MLA_HEADS * Q_PAD)
    wqb_p = wq_pad.astype(BF16)
    wqb_s = jnp.concatenate([wq_pad, wq_rot], axis=1).astype(BF16)
    w_kvb = w_kv_b[0].astype(BF16)
    cos, sin = _rope_tables(td)
    cosq = jnp.concatenate([jnp.ones((td, QK_NOPE), F32), cos, jnp.ones((td, Q_PAD - QK_NOPE - QK_ROPE), F32)], axis=1)
    sinq = jnp.concatenate([jnp.zeros((td, QK_NOPE), F32), sin, jnp.zeros((td, Q_PAD - QK_NOPE - QK_ROPE), F32)], axis=1)
    cosq = cosq * MLA_QSCALE
    sinq = sinq * MLA_QSCALE
    csk = jnp.concatenate([cos, sin], axis=1)

    tiles_per_seq = td // ROW_TILE
    g_attn2 = g_attn[0].reshape(1, d)
    gq2 = g_q_a[0].reshape(1, Q_LORA)
    gkv2 = g_kv_a[0].reshape(1, KV_LORA)

    naq_p, nak_p, nav_p, q_p, ckv_p, kr_p, krp_p = _pre_attention(
        xp, mod, g_attn2, w_in_p, gq2, wqb_p, gkv2, None, rope=False, mod_row=lambda i: 0)
    ona_p, omla_p = _prompt_attention(naq_p, nak_p, nav_p, q_p, ckv_p, krp_p, w_kvb, sp)

    naq_s, nak_s, nav_s, q_s, ckv_s, krp_s = _pre_attention(
        xs, mod, g_attn2, w_in_s, gq2, wqb_s, gkv2, (cosq, sinq, csk), rope=True,
        mod_row=lambda i: 1 + i // tiles_per_seq)
    past = cache_na_k.shape[2]
    kc = cache_na_k[:, 0].reshape(bd, past, NA_WIDTH)
    vc = cache_na_v[:, 0].reshape(bd, past, NA_WIDTH)
    bias = _na_bias_table(na_rpb[0] * LOG2E, td // GRID_W)
    ona_s = _sample_na(naq_s, nak_s, nav_s, kc, vc, bias, bd, td)
    ckv_all = jnp.concatenate([ckv_s.reshape(bd, td, KV_LORA), cache_mla_ckv[:, 0].astype(BF16)], axis=1)
    krp_c = jnp.concatenate([cache_mla_krope[:, 0], jnp.zeros((bd, past, LANE - QK_ROPE), F32)], axis=-1).astype(BF16)
    krp_all = jnp.concatenate([krp_s.reshape(bd, td, LANE), krp_c], axis=1)
    kf, vv = _kv_expand(ckv_all, krp_all, w_kvb)
    omla_s = _sample_mla(q_s, kf, vv, bd, td)

    npt = n_p // POST_TILE
    post_tiles_per_seq = td // POST_TILE
    x1, h2p, idx128, gate128 = _post_attention(
        xp, xs, ona_p, omla_p, ona_s, omla_s, mod, w_out[0].astype(BF16), g_ffn[0].reshape(1, d),
        w_router[0], b_router[0].reshape(1, N_EXPERTS),
        mod_row=lambda i: jnp.where(i < npt, 0, 1 + jnp.maximum(i - npt, 0) // post_tiles_per_seq))

    n = n_p + n_s
    nch = n * TOP_K // MOE_CHUNK + N_EXPERTS
    order, chunk_expert, chunk_rows, chunk_first = _routing(idx128[:, :TOP_K], nch)
    wgu = _deinterleave_gate_up(w_gate_up[0])
    ng = 2 * D_FF // (2 * LANE)
    bgu = b_gate_up[0].reshape(N_EXPERTS, ng, LANE, 2).transpose(0, 1, 3, 2).reshape(N_EXPERTS, 1, 2 * D_FF)
    y_experts = _moe(chunk_expert, chunk_rows, chunk_first, order, h2p, wgu, bgu, w_down[0],
                     b_down[0].reshape(N_EXPERTS, 1, d))

    tt = COMBINE_TOKENS
    gfin = g_final.reshape(1, d)
    ctiles_seq = td // tt
    y_p = _combine(gate128, x1, mod, gfin, y_experts, tile0=0, ntiles=n_p // tt, mod_row=lambda i: 0)
    y_s = _combine(gate128, x1, mod, gfin, y_experts, tile0=n_p // tt, ntiles=n_s // tt,
                   mod_row=lambda i: 1 + i // ctiles_seq)

    return (y_p.reshape(bp, sp, d), y_s.reshape(bd, td, d),
            nak_p.reshape(bp, 1, sp, NA_HEADS, NA_HEAD_DIM), nav_p.reshape(bp, 1, sp, NA_HEADS, NA_HEAD_DIM),
            ckv_p.reshape(bp, 1, sp, KV_LORA), kr_p.reshape(bp, 1, sp, QK_ROPE))
```

```python
import functools

import numpy as np
import jax
import jax.numpy as jnp
from jax import lax
from jax.experimental import pallas as pl
from jax.experimental.pallas import tpu as pltpu

F32 = jnp.float32
BF16 = jnp.bfloat16
U32 = jnp.uint32

D_MODEL = 2048
GRID_W = 64
NA_HEADS = 8
NA_HEAD_DIM = 128
NA_KR = 8
NA_KC = 16
MLA_HEADS = 8
Q_LORA = 512
KV_LORA = 256
QK_NOPE = 128
QK_ROPE = 64
V_DIM = 128
ROPE_AXIS = QK_ROPE // 2
ROPE_THETA = 10000.0
NA_WIDTH = NA_HEADS * NA_HEAD_DIM
MLA_WIDTH = MLA_HEADS * V_DIM
IN_COLS = 3 * NA_WIDTH + Q_LORA + KV_LORA + QK_ROPE
N_EXPERTS = 32
TOP_K = 4
D_FF = D_MODEL
SWIGLU_ALPHA = 1.702
SWIGLU_LIMIT = 7.0
EPS = 1e-6

LANE = 128
Q_PAD = 2 * LANE
KR_OFF = 3 * NA_WIDTH + Q_LORA + KV_LORA
IN_COLS_PAD = KR_OFF + LANE
VMEM_LIMIT = 56 * 1024 * 1024
NEG_BIG = -1e30

LOG2E = 1.4426950408889634
NA_QSCALE = NA_HEAD_DIM ** -0.5 * LOG2E
MLA_QSCALE = (QK_NOPE + QK_ROPE) ** -0.5 * LOG2E

ROW_TILE = 256
POST_TILE = 2 * ROW_TILE
NA_Q_ROWS = 4
NA_WIN_ROWS = 12
MLA_TQ = 512
MLA_TK = 1152
MOE_CHUNK = 1024
MOE_SUB = 256
MOE_TF = 512
COMBINE_TOKENS = 128
TOP_K_SHIFT = TOP_K.bit_length() - 1
DMA_UNROLL = 8
PACK_ROWS = D_MODEL // (2 * LANE)
OUT_ROWS = D_MODEL // LANE


def _cparams(sem):
    return pltpu.CompilerParams(dimension_semantics=sem, vmem_limit_bytes=VMEM_LIMIT)


def _const_spec(shape):
    nd = len(shape)
    return pl.BlockSpec(shape, lambda *a: (0,) * nd, pipeline_mode=pl.Buffered(1))


def _rms(x, g):
    return x * lax.rsqrt(jnp.mean(x * x, axis=-1, keepdims=True) + EPS) * g


def _dot(a, b):
    return jnp.dot(a, b, preferred_element_type=F32)


def _dot_nt(a, b):
    return lax.dot_general(a, b, (((1,), (1,)), ((), ())), preferred_element_type=F32)


def _mod_kernel(c_ref, w_ref, b_ref, o_ref):
    c = c_ref[...]
    s = c / (1.0 + jnp.exp(-c))
    o_ref[...] = _dot(s.astype(BF16), w_ref[...].astype(BF16)) + b_ref[...]


def _modulation(c8, w_mod, b_mod):
    n = w_mod.shape[1]
    tn = 1024
    return pl.pallas_call(
        _mod_kernel,
        grid=(n // tn,),
        in_specs=[pl.BlockSpec((8, D_MODEL), lambda j: (0, 0)),
                  pl.BlockSpec((D_MODEL, tn), lambda j: (0, j)),
                  pl.BlockSpec((1, tn), lambda j: (0, j))],
        out_specs=pl.BlockSpec((8, tn), lambda j: (0, j)),
        out_shape=jax.ShapeDtypeStruct((8, n), F32),
        compiler_params=_cparams(("arbitrary",)),
        name="modulation",
    )(c8, w_mod, b_mod)


def _pre_kernel(*refs, rope):
    if rope:
        (x_ref, mod_ref, g_ref, win_ref, gq_ref, wqb_ref, gkv_ref, cosq_ref, sinq_ref, csk_ref,
         naq_ref, nak_ref, nav_ref, q_ref, ckv_ref, krp_ref) = refs
    else:
        (x_ref, mod_ref, g_ref, win_ref, gq_ref, wqb_ref, gkv_ref,
         naq_ref, nak_ref, nav_ref, q_ref, ckv_ref, kr_ref, krp_ref) = refs
    x = x_ref[...]
    sa = mod_ref[0, 0:1, :]
    sca = mod_ref[0, 1:2, :]
    h = (_rms(x, g_ref[...]) * (1.0 + sca) + sa).astype(BF16)
    proj = _dot(h, win_ref[...])
    naq_ref[...] = (proj[:, 0:NA_WIDTH] * NA_QSCALE).astype(naq_ref.dtype)
    nak_ref[...] = proj[:, NA_WIDTH:2 * NA_WIDTH].astype(nak_ref.dtype)
    nav_ref[...] = proj[:, 2 * NA_WIDTH:3 * NA_WIDTH].astype(nav_ref.dtype)
    q_a = proj[:, 3 * NA_WIDTH:3 * NA_WIDTH + Q_LORA]
    kv_a = proj[:, 3 * NA_WIDTH + Q_LORA:KR_OFF]
    krx = proj[:, KR_OFF:IN_COLS_PAD]
    qan = _rms(q_a, gq_ref[...]).astype(BF16)
    qq = _dot(qan, wqb_ref[...])
    ckv_ref[...] = _rms(kv_a, gkv_ref[...]).astype(ckv_ref.dtype)
    if rope:
        width = MLA_HEADS * Q_PAD
        cosq = cosq_ref[...]
        sinq = sinq_ref[...]
        for hd in range(MLA_HEADS):
            a = qq[:, hd * Q_PAD:(hd + 1) * Q_PAD]
            b = qq[:, width + hd * Q_PAD:width + (hd + 1) * Q_PAD]
            q_ref[:, hd * Q_PAD:(hd + 1) * Q_PAD] = (a * cosq + b * sinq).astype(BF16)
        y = krx * csk_ref[...]
        y = y + pltpu.roll(y, QK_ROPE, 1)
        lane = lax.broadcasted_iota(jnp.int32, y.shape, 1)
        krp_ref[...] = jnp.where(lane < QK_ROPE, y, 0.0).astype(BF16)
    else:
        q_ref[...] = (qq * MLA_QSCALE).astype(BF16)
        kr_ref[...] = krx[:, 0:QK_ROPE]
        krp_ref[...] = krx.astype(BF16)


def _pre_attention(x, mod, g_attn, w_in, g_q_a, w_qb, g_kv_a, rope_tabs, *, rope, mod_row):
    n = x.shape[0]
    tm = ROW_TILE
    row = lambda i: (i, 0)
    in_specs = [pl.BlockSpec((tm, D_MODEL), row),
                pl.BlockSpec((1, 6, D_MODEL), lambda i: (mod_row(i), 0, 0)),
                _const_spec((1, D_MODEL)),
                _const_spec(w_in.shape),
                _const_spec((1, Q_LORA)),
                _const_spec(w_qb.shape),
                _const_spec((1, KV_LORA))]
    args = [x, mod, g_attn, w_in, g_q_a, w_qb, g_kv_a]
    qw = MLA_HEADS * Q_PAD
    if rope:
        tiles_per_seq = rope_tabs[0].shape[0] // tm
        pos = lambda i: (i % tiles_per_seq, 0)
        in_specs += [pl.BlockSpec((tm, Q_PAD), pos), pl.BlockSpec((tm, Q_PAD), pos),
                     pl.BlockSpec((tm, LANE), pos)]
        args += list(rope_tabs)
        out_shape = [jax.ShapeDtypeStruct((n, NA_WIDTH), BF16)] * 3 + [
            jax.ShapeDtypeStruct((n, qw), BF16),
            jax.ShapeDtypeStruct((n, KV_LORA), BF16),
            jax.ShapeDtypeStruct((n, LANE), BF16)]
        out_specs = [pl.BlockSpec((tm, NA_WIDTH), row)] * 3 + [
            pl.BlockSpec((tm, qw), row), pl.BlockSpec((tm, KV_LORA), row), pl.BlockSpec((tm, LANE), row)]
    else:
        out_shape = [jax.ShapeDtypeStruct((n, NA_WIDTH), BF16),
                     jax.ShapeDtypeStruct((n, NA_WIDTH), F32),
                     jax.ShapeDtypeStruct((n, NA_WIDTH), F32),
                     jax.ShapeDtypeStruct((n, qw), BF16),
                     jax.ShapeDtypeStruct((n, KV_LORA), F32),
                     jax.ShapeDtypeStruct((n, QK_ROPE), F32),
                     jax.ShapeDtypeStruct((n, LANE), BF16)]
        out_specs = [pl.BlockSpec((tm, NA_WIDTH), row)] * 3 + [
            pl.BlockSpec((tm, qw), row), pl.BlockSpec((tm, KV_LORA), row),
            pl.BlockSpec((tm, QK_ROPE), row), pl.BlockSpec((tm, LANE), row)]
    return pl.pallas_call(
        functools.partial(_pre_kernel, rope=rope),
        grid=(n // tm,),
        in_specs=in_specs,
        out_specs=out_specs,
        out_shape=out_shape,
        compiler_params=_cparams(("arbitrary",)),
        name="pre_attention_rope" if rope else "pre_attention",
    )(*args)


def _softmax_pv(s, v):
    m = jnp.max(s, axis=-1, keepdims=True)
    p = jnp.exp2(s - m)
    l = jnp.sum(p, axis=-1, keepdims=True)
    return _dot(p.astype(BF16), v) / l


def _prompt_attn_kernel(naq_ref, nak_ref, nav_ref, q_ref, ckv_ref, krp_ref, wkvb_ref, ona_ref, omla_ref):
    kv = _dot(ckv_ref[...].astype(BF16), wkvb_ref[...])
    krp = krp_ref[...]
    for hd in range(NA_HEADS):
        sl = slice(hd * NA_HEAD_DIM, (hd + 1) * NA_HEAD_DIM)
        s = _dot_nt(naq_ref[:, sl], nak_ref[:, sl].astype(BF16))
        ona_ref[:, sl] = _softmax_pv(s, nav_ref[:, sl].astype(BF16)).astype(BF16)
    kvw = QK_NOPE + V_DIM
    for hd in range(MLA_HEADS):
        kf = jnp.concatenate([kv[:, hd * kvw:hd * kvw + QK_NOPE].astype(BF16), krp], axis=-1)
        s = _dot_nt(q_ref[:, hd * Q_PAD:(hd + 1) * Q_PAD], kf)
        v = kv[:, hd * kvw + QK_NOPE:(hd + 1) * kvw].astype(BF16)
        omla_ref[:, hd * V_DIM:(hd + 1) * V_DIM] = _softmax_pv(s, v).astype(BF16)


def _prompt_attention(naq, nak, nav, q, ckv, krp, w_kvb, seq):
    n = naq.shape[0]
    row = lambda b: (b, 0)
    return pl.pallas_call(
        _prompt_attn_kernel,
        grid=(n // seq,),
        in_specs=[pl.BlockSpec((seq, NA_WIDTH), row)] * 3 + [
            pl.BlockSpec((seq, MLA_HEADS * Q_PAD), row),
            pl.BlockSpec((seq, KV_LORA), row),
            pl.BlockSpec((seq, LANE), row),
            _const_spec(w_kvb.shape)],
        out_specs=[pl.BlockSpec((seq, NA_WIDTH), row), pl.BlockSpec((seq, MLA_WIDTH), row)],
        out_shape=[jax.ShapeDtypeStruct((n, NA_WIDTH), BF16), jax.ShapeDtypeStruct((n, MLA_WIDTH), BF16)],
        compiler_params=_cparams(("arbitrary",)),
        name="prompt_attention",
    )(naq, nak, nav, q, ckv, krp, w_kvb)


def _na_kernel(q_ref, k_ref, v_ref, kc_ref, vc_ref, bias_ref, o_ref, *, rows):
    rt = pl.program_id(2)
    ws = jnp.clip(rt * NA_Q_ROWS - NA_KR // 2, 0, rows - NA_WIN_ROWS)
    start = pl.multiple_of(ws * GRID_W, GRID_W)
    nwin = NA_WIN_ROWS * GRID_W
    q = q_ref[...]
    s_loc = _dot_nt(q, k_ref[pl.ds(start, nwin), :]) + bias_ref[0, 0]
    s_ctx = _dot_nt(q, kc_ref[0].astype(BF16))
    m = jnp.maximum(jnp.max(s_loc, axis=-1, keepdims=True), jnp.max(s_ctx, axis=-1, keepdims=True))
    p_loc = jnp.exp2(s_loc - m)
    p_ctx = jnp.exp2(s_ctx - m)
    l = jnp.sum(p_loc, axis=-1, keepdims=True) + jnp.sum(p_ctx, axis=-1, keepdims=True)
    o = _dot(p_loc.astype(BF16), v_ref[pl.ds(start, nwin), :]) + _dot(p_ctx.astype(BF16), vc_ref[0].astype(BF16))
    o_ref[...] = (o / l).astype(BF16)


def _na_bias_table(rpb, rows):
    nh = rpb.shape[0]
    a = np.arange(NA_Q_ROWS)
    b = np.arange(NA_WIN_ROWS)
    col = np.arange(GRID_W)
    cs = np.clip(col - NA_KC // 2, 0, GRID_W - NA_KC)
    valid_col = (col[None, :] >= cs[:, None]) & (col[None, :] < cs[:, None] + NA_KC)
    padw = GRID_W - NA_KC
    rp = jnp.pad(rpb, ((0, 0), (0, 0), (padw, padw)))
    toep = jnp.stack([rp[:, :, GRID_W - 1 - qc:2 * GRID_W - 1 - qc] for qc in range(GRID_W)], axis=2)
    toep = jnp.where(valid_col[None, None], toep, NEG_BIG)
    masked = jnp.full((nh, GRID_W, GRID_W), NEG_BIG, F32)
    tabs = []
    for r0 in (0, NA_Q_ROWS, rows - NA_Q_ROWS):
        ws = int(np.clip(r0 - NA_KR // 2, 0, rows - NA_WIN_ROWS))
        r = r0 + a
        rs = np.clip(r - NA_KR // 2, 0, rows - NA_KR)
        kr = ws + b
        valid_row = (kr[None, :] >= rs[:, None]) & (kr[None, :] < rs[:, None] + NA_KR)
        dr = kr[None, :] - r[:, None] + NA_KR - 1
        tile_rows = []
        for ai in range(NA_Q_ROWS):
            blocks = [toep[:, int(dr[ai, bi])] if valid_row[ai, bi] else masked for bi in range(NA_WIN_ROWS)]
            tile_rows.append(jnp.concatenate(blocks, axis=-1))
        tabs.append(jnp.concatenate(tile_rows, axis=1))
    return jnp.stack(tabs)


def _sample_na(naq, nak, nav, kc, vc, bias, batch, seq):
    rows = seq // GRID_W
    tq = NA_Q_ROWS * GRID_W
    nt = seq // tq
    last = nt - 1

    def pat(rt):
        return jnp.where(rt == 0, 0, jnp.where(rt == last, 2, 1))

    return pl.pallas_call(
        functools.partial(_na_kernel, rows=rows),
        grid=(batch, NA_HEADS, nt),
        in_specs=[pl.BlockSpec((tq, NA_HEAD_DIM), lambda b, h, r: (b * nt + r, h)),
                  pl.BlockSpec((seq, NA_HEAD_DIM), lambda b, h, r: (b, h)),
                  pl.BlockSpec((seq, NA_HEAD_DIM), lambda b, h, r: (b, h)),
                  pl.BlockSpec((1, kc.shape[1], NA_HEAD_DIM), lambda b, h, r: (b, 0, h)),
                  pl.BlockSpec((1, vc.shape[1], NA_HEAD_DIM), lambda b, h, r: (b, 0, h)),
                  pl.BlockSpec((1, 1, tq, NA_WIN_ROWS * GRID_W), lambda b, h, r: (pat(r), h, 0, 0))],
        out_specs=pl.BlockSpec((tq, NA_HEAD_DIM), lambda b, h, r: (b * nt + r, h)),
        out_shape=jax.ShapeDtypeStruct((batch * seq, NA_WIDTH), BF16),
        compiler_params=_cparams(("arbitrary", "arbitrary", "arbitrary")),
        name="sample_neighbourhood_attention",
    )(naq, nak, nav, kc, vc, bias)


def _kv_expand_kernel(ckv_ref, krp_ref, w_ref, kf_ref, v_ref):
    kv = _dot(ckv_ref[0], w_ref[...])
    krp = krp_ref[0]
    kvw = QK_NOPE + V_DIM
    for hd in range(MLA_HEADS):
        kf_ref[0, hd, :, 0:QK_NOPE] = kv[:, hd * kvw:hd * kvw + QK_NOPE].astype(BF16)
        kf_ref[0, hd, :, QK_NOPE:Q_PAD] = krp
        v_ref[0, hd] = kv[:, hd * kvw + QK_NOPE:(hd + 1) * kvw].astype(BF16)


def _kv_expand(ckv, krp, w_kvb):
    batch, nkeys, _ = ckv.shape
    tm = 512
    return pl.pallas_call(
        _kv_expand_kernel,
        grid=(batch, nkeys // tm),
        in_specs=[pl.BlockSpec((1, tm, KV_LORA), lambda b, t: (b, t, 0)),
                  pl.BlockSpec((1, tm, LANE), lambda b, t: (b, t, 0)),
                  _const_spec(w_kvb.shape)],
        out_specs=[pl.BlockSpec((1, MLA_HEADS, tm, Q_PAD), lambda b, t: (b, 0, t, 0)),
                   pl.BlockSpec((1, MLA_HEADS, tm, V_DIM), lambda b, t: (b, 0, t, 0))],
        out_shape=[jax.ShapeDtypeStruct((batch, MLA_HEADS, nkeys, Q_PAD), BF16),
                   jax.ShapeDtypeStruct((batch, MLA_HEADS, nkeys, V_DIM), BF16)],
        compiler_params=_cparams(("arbitrary", "arbitrary")),
        name="latent_kv_expand",
    )(ckv, krp, w_kvb)


def _mla_kernel(q_ref, kf_ref, v_ref, o_ref, *, nkeys):
    q = q_ref[...]
    tq = q.shape[0]
    m = jnp.full((tq, 1), NEG_BIG, F32)
    l = jnp.zeros((tq, 1), F32)
    acc = jnp.zeros((tq, V_DIM), F32)
    for c in range(nkeys // MLA_TK):
        ks = slice(c * MLA_TK, (c + 1) * MLA_TK)
        s = _dot_nt(q, kf_ref[0, 0, ks, :])
        m_new = jnp.maximum(m, jnp.max(s, axis=-1, keepdims=True))
        alpha = jnp.exp2(m - m_new)
        p = jnp.exp2(s - m_new)
        l = alpha * l + jnp.sum(p, axis=-1, keepdims=True)
        acc = alpha * acc + _dot(p.astype(BF16), v_ref[0, 0, ks, :])
        m = m_new
    o_ref[...] = (acc / l).astype(BF16)


def _sample_mla(q, kf, v, batch, seq):
    nkeys = kf.shape[2]
    nt = seq // MLA_TQ
    return pl.pallas_call(
        functools.partial(_mla_kernel, nkeys=nkeys),
        grid=(batch, MLA_HEADS, nt),
        in_specs=[pl.BlockSpec((MLA_TQ, Q_PAD), lambda b, h, t: (b * nt + t, h)),
                  pl.BlockSpec((1, 1, nkeys, Q_PAD), lambda b, h, t: (b, h, 0, 0)),
                  pl.BlockSpec((1, 1, nkeys, V_DIM), lambda b, h, t: (b, h, 0, 0))],
        out_specs=pl.BlockSpec((MLA_TQ, V_DIM), lambda b, h, t: (b * nt + t, h)),
        out_shape=jax.ShapeDtypeStruct((batch * seq, MLA_WIDTH), BF16),
        compiler_params=_cparams(("arbitrary", "arbitrary", "arbitrary")),
        name="sample_latent_attention",
    )(q, kf, v)


def _post_kernel(xp_ref, xs_ref, onp_ref, omp_ref, ons_ref, oms_ref, mod_ref, wout_ref, gffn_ref, wrh_ref, wrl_ref,
                 br_ref, x1_ref, h2p_ref, idx_ref, gate_ref, *, n_prompt_tiles):
    is_prompt = pl.program_id(0) < n_prompt_tiles
    ga = mod_ref[0, 2:3, :]
    sf = mod_ref[0, 3:4, :]
    scf = mod_ref[0, 4:5, :]
    tm = ROW_TILE
    for r0 in range(0, xp_ref.shape[0], tm):
        rs = slice(r0, r0 + tm)
        x = jnp.where(is_prompt, xp_ref[rs, :], xs_ref[rs, :])
        ona = jnp.where(is_prompt, onp_ref[rs, :], ons_ref[rs, :])
        omla = jnp.where(is_prompt, omp_ref[rs, :], oms_ref[rs, :])
        o = _dot(ona, wout_ref[0:NA_WIDTH, :]) + _dot(omla, wout_ref[NA_WIDTH:NA_WIDTH + MLA_WIDTH, :])
        x1 = x + ga * o
        x1_ref[rs, :] = x1
        h2 = _rms(x1, gffn_ref[...]) * (1.0 + scf) + sf
        h_hi = h2.astype(BF16)
        h_lo = (h2 - h_hi.astype(F32)).astype(BF16)
        logits = _dot(h_hi, wrh_ref[...]) + _dot(h_lo, wrh_ref[...]) + _dot(h_hi, wrl_ref[...]) + br_ref[...]
        lane_e = lax.broadcasted_iota(jnp.int32, logits.shape, 1).astype(F32)
        lane_o = lax.broadcasted_iota(jnp.int32, (tm, LANE), 1)
        idx_out = jnp.zeros((tm, LANE), F32)
        gate_out = jnp.zeros((tm, LANE), F32)
        top0 = None
        denom = jnp.zeros((tm, 1), F32)
        cur = logits
        for k in range(TOP_K):
            mx = jnp.max(cur, axis=-1, keepdims=True)
            ix = jnp.min(jnp.where(cur == mx, lane_e, float(N_EXPERTS)), axis=-1, keepdims=True)
            cur = jnp.where(lane_e == ix, -jnp.inf, cur)
            if k == 0:
                top0 = mx
            e = jnp.exp(mx - top0)
            denom = denom + e
            idx_out = jnp.where(lane_o == k, ix, idx_out)
            gate_out = jnp.where(lane_o == k, e, gate_out)
        idx_ref[rs, :] = idx_out.astype(jnp.int32)
        gate_ref[rs, :] = gate_out / denom
        bits = pltpu.bitcast(h_hi.astype(F32), U32)
        for s in range(PACK_ROWS):
            lo = bits[:, s * LANE:(s + 1) * LANE] >> 16
            hi = bits[:, (s + PACK_ROWS) * LANE:(s + PACK_ROWS + 1) * LANE] & jnp.uint32(0xFFFF0000)
            h2p_ref[pl.ds(r0 * PACK_ROWS + s, tm, stride=PACK_ROWS), :] = hi | lo


def _post_attention(xp, xs, onp, omp, ons, oms, mod, w_out, g_ffn, w_router, b_router, *, mod_row):
    tm = POST_TILE
    wr_hi = w_router.astype(BF16)
    wr_lo = (w_router - wr_hi.astype(F32)).astype(BF16)
    npt = xp.shape[0] // tm
    nst = xs.shape[0] // tm
    n = xp.shape[0] + xs.shape[0]
    pidx = lambda i: (jnp.minimum(i, npt - 1), 0)
    sidx = lambda i: (jnp.maximum(i - npt, 0), 0)
    row = lambda i: (i, 0)
    return pl.pallas_call(
        functools.partial(_post_kernel, n_prompt_tiles=npt),
        grid=(npt + nst,),
        in_specs=[pl.BlockSpec((tm, D_MODEL), pidx), pl.BlockSpec((tm, D_MODEL), sidx),
                  pl.BlockSpec((tm, NA_WIDTH), pidx), pl.BlockSpec((tm, MLA_WIDTH), pidx),
                  pl.BlockSpec((tm, NA_WIDTH), sidx), pl.BlockSpec((tm, MLA_WIDTH), sidx),
                  pl.BlockSpec((1, 6, D_MODEL), lambda i: (mod_row(i), 0, 0)),
                  _const_spec(w_out.shape), _const_spec((1, D_MODEL)),
                  _const_spec(w_router.shape), _const_spec(w_router.shape), _const_spec((1, N_EXPERTS))],
        out_specs=[pl.BlockSpec((tm, D_MODEL), row),
                   pl.BlockSpec((tm * PACK_ROWS, LANE), row),
                   pl.BlockSpec((tm, LANE), row),
                   pl.BlockSpec((tm, LANE), row)],
        out_shape=[jax.ShapeDtypeStruct((n, D_MODEL), F32),
                   jax.ShapeDtypeStruct((n * PACK_ROWS, LANE), U32),
                   jax.ShapeDtypeStruct((n, LANE), jnp.int32),
                   jax.ShapeDtypeStruct((n, LANE), F32)],
        compiler_params=_cparams(("arbitrary",)),
        name="post_attention_router",
    )(xp, xs, onp, omp, ons, oms, mod, w_out, g_ffn, wr_hi, wr_lo, b_router)


def _deinterleave_kernel(w_ref, p_ref, o_ref):
    p = p_ref[...]
    grp = p.shape[0]
    for c in range(w_ref.shape[2] // grp):
        w = w_ref[0, :, c * grp:(c + 1) * grp].astype(BF16)
        o_ref[0, :, c * grp:(c + 1) * grp] = _dot(w, p).astype(BF16)


def _deinterleave_gate_up(w_gate_up):
    ne, d, n2 = w_gate_up.shape
    grp = 2 * LANE
    dst = np.arange(grp)
    src = np.where(dst < LANE, 2 * dst, 2 * (dst - LANE) + 1)
    perm = np.zeros((grp, grp), np.float32)
    perm[src, dst] = 1.0
    tk = 512
    return pl.pallas_call(
        _deinterleave_kernel,
        grid=(ne, d // tk),
        in_specs=[pl.BlockSpec((1, tk, n2), lambda e, k: (e, k, 0)), _const_spec((grp, grp))],
        out_specs=pl.BlockSpec((1, tk, n2), lambda e, k: (e, k, 0)),
        out_shape=jax.ShapeDtypeStruct((ne, d, n2), BF16),
        compiler_params=_cparams(("arbitrary", "arbitrary")),
        name="deinterleave_gate_up",
    )(w_gate_up, jnp.asarray(perm, BF16))


def _moe_kernel(ce_ref, nv_ref, cs_ref, order_ref, h_ref, wgu_ref, bgu_ref, wd_ref, bd_ref, y_ref,
                xg_ref, xbf_ref, acc_ref, ost_ref, sem_g, sem_s):
    c = pl.program_id(0)
    j = pl.program_id(1)
    last_c = pl.num_programs(0) - 1
    last_j = pl.num_programs(1) - 1
    nk = order_ref.shape[0]
    nsub = MOE_CHUNK // MOE_SUB
    xg_rows = MOE_CHUNK * PACK_ROWS

    nv = nv_ref[c]
    c_next = jnp.minimum(c + 1, last_c)
    nv_next = jnp.where(c < last_c, nv_ref[c_next], 0)
    cs_next = cs_ref[c_next]
    c_prev = jnp.maximum(c - 1, 0)
    nv_prev = jnp.where(c > 0, nv_ref[c_prev], 0)
    cs_prev = cs_ref[c_prev]
    nv_prev2 = jnp.where(c > 1, nv_ref[jnp.maximum(c - 2, 0)], 0)
    ost_rows = MOE_CHUNK * OUT_ROWS
    slot = c % 2
    slot_next = (c + 1) % 2

    def gather_row(cs, dst_slot, row):
        tok = jnp.right_shift(order_ref[jnp.minimum(cs + row, nk - 1)], TOP_K_SHIFT)
        src = pl.multiple_of(tok * PACK_ROWS, PACK_ROWS)
        dst = pl.multiple_of(dst_slot * xg_rows + row * PACK_ROWS, PACK_ROWS)
        pltpu.make_async_copy(h_ref.at[pl.ds(src, PACK_ROWS)], xg_ref.at[pl.ds(dst, PACK_ROWS)],
                              sem_g.at[dst_slot]).start()

    def scatter_row(row):
        flat = order_ref[jnp.minimum(cs_prev + row, nk - 1)]
        dest = jnp.where(row < nv_prev, flat, nk + slot_next * MOE_CHUNK + row)
        src = pl.multiple_of(slot_next * ost_rows + row * OUT_ROWS, OUT_ROWS)
        dst = pl.multiple_of(dest * OUT_ROWS, OUT_ROWS)
        pltpu.make_async_copy(ost_ref.at[pl.ds(src, OUT_ROWS)], y_ref.at[pl.ds(dst, OUT_ROWS)],
                              sem_s.at[slot_next]).start()

    def rolled(fn, row0, nrows):
        def body(r0, carry):
            for u in range(DMA_UNROLL):
                fn(row0 + r0 * DMA_UNROLL + u)
            return carry

        lax.fori_loop(0, nrows // DMA_UNROLL, body, 0)

    @pl.when(jnp.logical_and(jnp.logical_and(c == 0, j == 0), nv > 0))
    def _():
        rolled(lambda row: gather_row(cs_ref[0], 0, row), 0, MOE_CHUNK)

    def unpack(sb):
        for s in range(PACK_ROWS):
            w = xg_ref[pl.ds(slot * xg_rows + sb * MOE_SUB * PACK_ROWS + s, MOE_SUB, stride=PACK_ROWS), :]
            lo = pltpu.bitcast(w << 16, F32)
            hi = pltpu.bitcast(w & jnp.uint32(0xFFFF0000), F32)
            xbf_ref[sb * MOE_SUB:(sb + 1) * MOE_SUB, s * LANE:(s + 1) * LANE] = lo.astype(BF16)
            xbf_ref[sb * MOE_SUB:(sb + 1) * MOE_SUB,
                    (s + PACK_ROWS) * LANE:(s + PACK_ROWS + 1) * LANE] = hi.astype(BF16)

    @pl.when(jnp.logical_and(j == 0, nv > 0))
    def _():
        base = pl.multiple_of(slot * xg_rows, xg_rows)
        pltpu.make_async_copy(h_ref.at[pl.ds(0, xg_rows)], xg_ref.at[pl.ds(base, xg_rows)], sem_g.at[slot]).wait()
        acc_ref[...] = jnp.broadcast_to(bd_ref[0], (MOE_CHUNK, D_MODEL))

        @pl.when(nv == MOE_CHUNK)
        def _():
            for sb in range(nsub):
                unpack(sb)

        @pl.when(nv < MOE_CHUNK)
        def _():
            for sb in range(nsub):
                @pl.when(sb * MOE_SUB < nv)
                def _():
                    unpack(sb)

    wd = wd_ref[0].astype(BF16)

    def sub_block(sb):
        x = xbf_ref[sb * MOE_SUB:(sb + 1) * MOE_SUB, :]
        gu = _dot(x, wgu_ref[0]) + bgu_ref[0]
        ng = MOE_TF // LANE
        g = jnp.concatenate([gu[:, 2 * b * LANE:(2 * b + 1) * LANE] for b in range(ng)], axis=-1)
        u = jnp.concatenate([gu[:, (2 * b + 1) * LANE:(2 * b + 2) * LANE] for b in range(ng)], axis=-1)
        g = jnp.minimum(g, SWIGLU_LIMIT)
        u = jnp.clip(u, -SWIGLU_LIMIT, SWIGLU_LIMIT)
        act = g * (1.0 / (1.0 + jnp.exp(-(g * SWIGLU_ALPHA)))) * (u + 1.0)
        acc_ref[sb * MOE_SUB:(sb + 1) * MOE_SUB, :] += _dot(act.astype(BF16), wd)

    dma_both = jnp.logical_and(nv_next > 0, nv_prev > 0)
    per_sb = MOE_SUB // nsub
    for sb in range(nsub):
        row0 = j * MOE_SUB + sb * per_sb
        compute = sb * MOE_SUB < nv
        fused = jnp.logical_and(compute, dma_both)

        @pl.when(fused)
        def _():
            for r in range(per_sb):
                gather_row(cs_next, slot_next, row0 + r)
                scatter_row(row0 + r)
            sub_block(sb)

        @pl.when(jnp.logical_not(fused))
        def _():
            @pl.when(nv_next > 0)
            def _():
                rolled(lambda row: gather_row(cs_next, slot_next, row), row0, per_sb)

            @pl.when(nv_prev > 0)
            def _():
                rolled(scatter_row, row0, per_sb)

            @pl.when(compute)
            def _():
                sub_block(sb)

    def wait_scatter(buf):
        base = pl.multiple_of(buf * ost_rows, ost_rows)
        pltpu.make_async_copy(ost_ref.at[pl.ds(base, ost_rows)], y_ref.at[pl.ds(0, ost_rows)], sem_s.at[buf]).wait()

    @pl.when(j == last_j)
    def _():
        @pl.when(nv_prev2 > 0)
        def _():
            wait_scatter(slot)

        @pl.when(nv > 0)
        def _():
            for s in range(OUT_ROWS):
                ost_ref[pl.ds(slot * ost_rows + s, MOE_CHUNK, stride=OUT_ROWS), :] = acc_ref[:, s * LANE:(s + 1) * LANE]

        @pl.when(jnp.logical_and(c == last_c, nv_prev > 0))
        def _():
            wait_scatter(slot_next)


def _moe(chunk_expert, chunk_rows, chunk_start, order, h2p, wgu, bgu, w_down, b_down):
    nj = D_FF // MOE_TF
    ngrid = chunk_expert.shape[0]
    nk = order.shape[0]

    def jj(c, j, nv):
        return jnp.where(nv[c] > 0, j, nj - 1)

    return pl.pallas_call(
        _moe_kernel,
        grid_spec=pltpu.PrefetchScalarGridSpec(
            num_scalar_prefetch=4,
            grid=(ngrid, nj),
            in_specs=[
                pl.BlockSpec(memory_space=pl.ANY),
                pl.BlockSpec((1, D_MODEL, 2 * MOE_TF), lambda c, j, ce, nv, cs, od: (ce[c], 0, jj(c, j, nv))),
                pl.BlockSpec((1, 1, 2 * MOE_TF), lambda c, j, ce, nv, cs, od: (ce[c], 0, jj(c, j, nv))),
                pl.BlockSpec((1, MOE_TF, D_MODEL), lambda c, j, ce, nv, cs, od: (ce[c], jj(c, j, nv), 0)),
                pl.BlockSpec((1, 1, D_MODEL), lambda c, j, ce, nv, cs, od: (ce[c], 0, 0))],
            out_specs=pl.BlockSpec(memory_space=pl.ANY),
            scratch_shapes=[pltpu.VMEM((2 * MOE_CHUNK * PACK_ROWS, LANE), U32),
                            pltpu.VMEM((MOE_CHUNK, D_MODEL), BF16),
                            pltpu.VMEM((MOE_CHUNK, D_MODEL), F32),
                            pltpu.VMEM((2 * MOE_CHUNK * OUT_ROWS, LANE), F32),
                            pltpu.SemaphoreType.DMA((2,)),
                            pltpu.SemaphoreType.DMA((2,))]),
        out_shape=jax.ShapeDtypeStruct(((nk + 2 * MOE_CHUNK) * OUT_ROWS, LANE), F32),
        compiler_params=_cparams(("arbitrary", "arbitrary")),
        name="moe_experts",
    )(chunk_expert, chunk_rows, chunk_start, order, h2p, wgu, bgu, w_down, b_down)


def _combine_kernel(gate_ref, x1_ref, mod_ref, gfin_ref, ye_ref, y_ref):
    gate = gate_ref[...]
    pieces = []
    for s in range(OUT_ROWS):
        acc = None
        for k in range(TOP_K):
            rows = ye_ref[pl.ds(k * OUT_ROWS + s, COMBINE_TOKENS, stride=TOP_K * OUT_ROWS), :]
            term = gate[:, k:k + 1] * rows
            acc = term if acc is None else acc + term
        pieces.append(acc)
    y = jnp.concatenate(pieces, axis=-1)
    gf = mod_ref[0, 5:6, :]
    y_ref[...] = _rms(x1_ref[...] + gf * y, gfin_ref[...])


def _combine(gates, x1, mod, g_final, y_experts, *, tile0, ntiles, mod_row):
    tt = COMBINE_TOKENS
    return pl.pallas_call(
        _combine_kernel,
        grid=(ntiles,),
        in_specs=[pl.BlockSpec((tt, LANE), lambda i: (tile0 + i, 0)),
                  pl.BlockSpec((tt, D_MODEL), lambda i: (tile0 + i, 0)),
                  pl.BlockSpec((1, 6, D_MODEL), lambda i: (mod_row(i), 0, 0)),
                  _const_spec((1, D_MODEL)),
                  pl.BlockSpec((tt * TOP_K * OUT_ROWS, LANE), lambda i: (tile0 + i, 0))],
        out_specs=pl.BlockSpec((tt, D_MODEL), lambda i: (i, 0)),
        out_shape=jax.ShapeDtypeStruct((ntiles * tt, D_MODEL), F32),
        compiler_params=_cparams(("arbitrary",)),
        name="moe_combine_final_norm",
    )(gates, x1, mod, g_final, y_experts)


def _routing(top_idx, nch):
    n = top_idx.shape[0]
    nk = n * TOP_K
    flat_e = top_idx.reshape(nk)
    order = jnp.argsort(flat_e, stable=True).astype(jnp.int32)
    onehot = flat_e[:, None] == jnp.arange(N_EXPERTS, dtype=jnp.int32)[None, :]
    counts = jnp.sum(onehot.astype(jnp.int32), axis=0)
    grp_start = jnp.cumsum(counts) - counts
    chunks_e = (counts + MOE_CHUNK - 1) // MOE_CHUNK
    chunk_end = jnp.cumsum(chunks_e)
    chunk_start = chunk_end - chunks_e

    total = chunk_end[-1]
    cidx = jnp.arange(nch + 1, dtype=jnp.int32)
    active = cidx < total
    ce = jnp.minimum(jnp.searchsorted(chunk_end, cidx, side='right'), N_EXPERTS - 1).astype(jnp.int32)
    local = (cidx - chunk_start[ce]) * MOE_CHUNK
    nv = jnp.where(active, jnp.clip(counts[ce] - local, 0, MOE_CHUNK), 0).astype(jnp.int32)
    last = jnp.maximum(total - 1, 0)
    chunk_expert = jnp.where(active, ce, ce[last]).astype(jnp.int32)
    chunk_first = jnp.where(active, jnp.clip(grp_start[ce] + local, 0, nk - 1), 0).astype(jnp.int32)
    return order, chunk_expert, nv, chunk_first


def _rope_tables(t):
    pos = jnp.arange(t)
    rows = (pos // GRID_W).astype(F32)
    cols = (pos % GRID_W).astype(F32)
    inv = ROPE_THETA ** (-(jnp.arange(ROPE_AXIS // 2, dtype=F32) * 2.0 / ROPE_AXIS))
    ar = rows[:, None] * inv
    ac = cols[:, None] * inv
    ang = jnp.concatenate([ar, ar, ac, ac], axis=-1)
    return jnp.cos(ang), jnp.sin(ang)


def _rot_cols(w):
    half = ROPE_AXIS // 2
    src = np.concatenate([np.arange(half, ROPE_AXIS), np.arange(0, half),
                          np.arange(ROPE_AXIS + half, 2 * ROPE_AXIS), np.arange(ROPE_AXIS, ROPE_AXIS + half)])
    sign = np.concatenate([-np.ones(half), np.ones(half), -np.ones(half), np.ones(half)]).astype(np.float32)
    return w[..., src] * sign


def kernel(x_prompt, x_sample, cache_na_k, cache_na_v, cache_mla_ckv, cache_mla_krope, c, c_ctx, g_attn, g_ffn, g_final, w_mod, b_mod, w_in, w_out, na_rpb, g_q_a, w_q_b, g_kv_a, w_kv_b, w_router, b_router, w_gate_up, b_gate_up, w_down, b_down):
    bp, sp, d = x_prompt.shape
    bd, td, _ = x_sample.shape
    assert d == D_MODEL and w_mod.shape[0] == 1, "one trunk layer of width D_MODEL"
    n_p = bp * sp
    n_s = bd * td
    xp = x_prompt.reshape(n_p, d)
    xs = x_sample.reshape(n_s, d)

    c8 = jnp.zeros((8, d), F32).at[0].set(c_ctx).at[1:1 + bd].set(c)
    mod = _modulation(c8, w_mod[0], b_mod[0].reshape(1, -1)).reshape(8, 6, d)

    w_in0 = w_in[0]
    w_kr = w_in0[:, KR_OFF:KR_OFF + QK_ROPE]
    w_in_p = jnp.concatenate([w_in0, jnp.zeros((d, LANE - QK_ROPE), F32)], axis=1).astype(BF16)
    w_in_s = jnp.concatenate([w_in0, _rot_cols(w_kr)], axis=1).astype(BF16)
    wq = w_q_b[0].reshape(Q_LORA, MLA_HEADS, QK_NOPE + QK_ROPE)
    zpad = jnp.zeros((Q_LORA, MLA_HEADS, Q_PAD - QK_NOPE - QK_ROPE), F32)
    wq_pad = jnp.concatenate([wq, zpad], axis=-1).reshape(Q_LORA, MLA_HEADS * Q_PAD)
    wq_rot = jnp.concatenate([jnp.zeros((Q_LORA, MLA_HEADS, QK_NOPE), F32), _rot_cols(wq[..., QK_NOPE:]), zpad],
                             axis=-1).reshape(Q_LORA, MLA_HEADS * Q_PAD)
    wqb_p = wq_pad.astype(BF16)
    wqb_s = jnp.concatenate([wq_pad, wq_rot], axis=1).astype(BF16)
    w_kvb = w_kv_b[0].astype(BF16)
    cos, sin = _rope_tables(td)
    cosq = jnp.concatenate([jnp.ones((td, QK_NOPE), F32), cos, jnp.ones((td, Q_PAD - QK_NOPE - QK_ROPE), F32)], axis=1)
    sinq = jnp.concatenate([jnp.zeros((td, QK_NOPE), F32), sin, jnp.zeros((td, Q_PAD - QK_NOPE - QK_ROPE), F32)], axis=1)
    cosq = cosq * MLA_QSCALE
    sinq = sinq * MLA_QSCALE
    csk = jnp.concatenate([cos, sin], axis=1)

    tiles_per_seq = td // ROW_TILE
    g_attn2 = g_attn[0].reshape(1, d)
    gq2 = g_q_a[0].reshape(1, Q_LORA)
    gkv2 = g_kv_a[0].reshape(1, KV_LORA)

    naq_p, nak_p, nav_p, q_p, ckv_p, kr_p, krp_p = _pre_attention(
        xp, mod, g_attn2, w_in_p, gq2, wqb_p, gkv2, None, rope=False, mod_row=lambda i: 0)
    ona_p, omla_p = _prompt_attention(naq_p, nak_p, nav_p, q_p, ckv_p, krp_p, w_kvb, sp)

    naq_s, nak_s, nav_s, q_s, ckv_s, krp_s = _pre_attention(
        xs, mod, g_attn2, w_in_s, gq2, wqb_s, gkv2, (cosq, sinq, csk), rope=True,
        mod_row=lambda i: 1 + i // tiles_per_seq)
    past = cache_na_k.shape[2]
    kc = cache_na_k[:, 0].reshape(bd, past, NA_WIDTH)
    vc = cache_na_v[:, 0].reshape(bd, past, NA_WIDTH)
    bias = _na_bias_table(na_rpb[0] * LOG2E, td // GRID_W)
    ona_s = _sample_na(naq_s, nak_s, nav_s, kc, vc, bias, bd, td)
    ckv_all = jnp.concatenate([ckv_s.reshape(bd, td, KV_LORA), cache_mla_ckv[:, 0].astype(BF16)], axis=1)
    krp_c = jnp.concatenate([cache_mla_krope[:, 0], jnp.zeros((bd, past, LANE - QK_ROPE), F32)], axis=-1).astype(BF16)
    krp_all = jnp.concatenate([krp_s.reshape(bd, td, LANE), krp_c], axis=1)
    kf, vv = _kv_expand(ckv_all, krp_all, w_kvb)
    omla_s = _sample_mla(q_s, kf, vv, bd, td)

    npt = n_p // POST_TILE
    post_tiles_per_seq = td // POST_TILE
    x1, h2p, idx128, gate128 = _post_attention(
        xp, xs, ona_p, omla_p, ona_s, omla_s, mod, w_out[0].astype(BF16), g_ffn[0].reshape(1, d),
        w_router[0], b_router[0].reshape(1, N_EXPERTS),
        mod_row=lambda i: jnp.where(i < npt, 0, 1 + jnp.maximum(i - npt, 0) // post_tiles_per_seq))

    n = n_p + n_s
    nch = n * TOP_K // MOE_CHUNK + N_EXPERTS
    order, chunk_expert, chunk_rows, chunk_first = _routing(idx128[:, :TOP_K], nch)
    wgu = _deinterleave_gate_up(w_gate_up[0])
    ng = 2 * D_FF // (2 * LANE)
    bgu = b_gate_up[0].reshape(N_EXPERTS, ng, LANE, 2).transpose(0, 1, 3, 2).reshape(N_EXPERTS, 1, 2 * D_FF)
    y_experts = _moe(chunk_expert, chunk_rows, chunk_first, order, h2p, wgu, bgu, w_down[0],
                     b_down[0].reshape(N_EXPERTS, 1, d))

    tt = COMBINE_TOKENS
    gfin = g_final.reshape(1, d)
    ctiles_seq = td // tt
    y_p = _combine(gate128, x1, mod, gfin, y_experts, tile0=0, ntiles=n_p // tt, mod_row=lambda i: 0)
    y_s = _combine(gate128, x1, mod, gfin, y_experts, tile0=n_p // tt, ntiles=n_s // tt,
                   mod_row=lambda i: 1 + i // ctiles_seq)

    return (y_p.reshape(bp, sp, d), y_s.reshape(bd, td, d),
            nak_p.reshape(bp, 1, sp, NA_HEADS, NA_HEAD_DIM), nav_p.reshape(bp, 1, sp, NA_HEADS, NA_HEAD_DIM),
            ckv_p.reshape(bp, 1, sp, KV_LORA), kr_p.reshape(bp, 1, sp, QK_ROPE))
```

```python
import functools

import numpy as np
import jax
import jax.numpy as jnp
from jax import lax
from jax.experimental import pallas as pl
from jax.experimental.pallas import tpu as pltpu

F32 = jnp.float32
BF16 = jnp.bfloat16
U32 = jnp.uint32

D_MODEL = 2048
GRID_W = 64
NA_HEADS = 8
NA_HEAD_DIM = 128
NA_KR = 8
NA_KC = 16
MLA_HEADS = 8
Q_LORA = 512
KV_LORA = 256
QK_NOPE = 128
QK_ROPE = 64
V_DIM = 128
ROPE_AXIS = QK_ROPE // 2
ROPE_THETA = 10000.0
NA_WIDTH = NA_HEADS * NA_HEAD_DIM
MLA_WIDTH = MLA_HEADS * V_DIM
IN_COLS = 3 * NA_WIDTH + Q_LORA + KV_LORA + QK_ROPE
N_EXPERTS = 32
TOP_K = 4
D_FF = D_MODEL
SWIGLU_ALPHA = 1.702
SWIGLU_LIMIT = 7.0
EPS = 1e-6

LANE = 128
Q_PAD = 2 * LANE
KR_OFF = 3 * NA_WIDTH + Q_LORA + KV_LORA
IN_COLS_PAD = KR_OFF + LANE
VMEM_LIMIT = 56 * 1024 * 1024
NEG_BIG = -1e30

LOG2E = 1.4426950408889634
NA_QSCALE = NA_HEAD_DIM ** -0.5 * LOG2E
MLA_QSCALE = (QK_NOPE + QK_ROPE) ** -0.5 * LOG2E

ROW_TILE = 256
POST_TILE = 2 * ROW_TILE
NA_Q_ROWS = 4
NA_WIN_ROWS = 12
MLA_TQ = 512
MLA_TK = 1152
MOE_CHUNK = 1024
MOE_SUB = 256
MOE_TF = 512
COMBINE_TOKENS = 128
TOP_K_SHIFT = TOP_K.bit_length() - 1
DMA_UNROLL = 8
PACK_ROWS = D_MODEL // (2 * LANE)
OUT_ROWS = D_MODEL // LANE


def _cparams(sem):
    return pltpu.CompilerParams(dimension_semantics=sem, vmem_limit_bytes=VMEM_LIMIT)


def _const_spec(shape):
    nd = len(shape)
    return pl.BlockSpec(shape, lambda *a: (0,) * nd, pipeline_mode=pl.Buffered(1))


def _rms(x, g):
    return x * lax.rsqrt(jnp.mean(x * x, axis=-1, keepdims=True) + EPS) * g


def _dot(a, b):
    return jnp.dot(a, b, preferred_element_type=F32)


def _dot_nt(a, b):
    return lax.dot_general(a, b, (((1,), (1,)), ((), ())), preferred_element_type=F32)


def _mod_kernel(c_ref, w_ref, b_ref, o_ref):
    c = c_ref[...]
    s = c / (1.0 + jnp.exp(-c))
    o_ref[...] = _dot(s.astype(BF16), w_ref[...].astype(BF16)) + b_ref[...]


def _modulation(c8, w_mod, b_mod):
    n = w_mod.shape[1]
    tn = 1024
    return pl.pallas_call(
        _mod_kernel,
        grid=(n // tn,),
        in_specs=[pl.BlockSpec((8, D_MODEL), lambda j: (0, 0)),
                  pl.BlockSpec((D_MODEL, tn), lambda j: (0, j)),
                  pl.BlockSpec((1, tn), lambda j: (0, j))],
        out_specs=pl.BlockSpec((8, tn), lambda j: (0, j)),
        out_shape=jax.ShapeDtypeStruct((8, n), F32),
        compiler_params=_cparams(("arbitrary",)),
        name="modulation",
    )(c8, w_mod, b_mod)


def _pre_kernel(*refs, rope):
    if rope:
        (x_ref, mod_ref, g_ref, win_ref, gq_ref, wqb_ref, gkv_ref, cosq_ref, sinq_ref, csk_ref,
         naq_ref, nak_ref, nav_ref, q_ref, ckv_ref, krp_ref) = refs
    else:
        (x_ref, mod_ref, g_ref, win_ref, gq_ref, wqb_ref, gkv_ref,
         naq_ref, nak_ref, nav_ref, q_ref, ckv_ref, kr_ref, krp_ref) = refs
    x = x_ref[...]
    sa = mod_ref[0, 0:1, :]
    sca = mod_ref[0, 1:2, :]
    h = (_rms(x, g_ref[...]) * (1.0 + sca) + sa).astype(BF16)
    proj = _dot(h, win_ref[...])
    naq_ref[...] = (proj[:, 0:NA_WIDTH] * NA_QSCALE).astype(naq_ref.dtype)
    nak_ref[...] = proj[:, NA_WIDTH:2 * NA_WIDTH].astype(nak_ref.dtype)
    nav_ref[...] = proj[:, 2 * NA_WIDTH:3 * NA_WIDTH].astype(nav_ref.dtype)
    q_a = proj[:, 3 * NA_WIDTH:3 * NA_WIDTH + Q_LORA]
    kv_a = proj[:, 3 * NA_WIDTH + Q_LORA:KR_OFF]
    krx = proj[:, KR_OFF:IN_COLS_PAD]
    qan = _rms(q_a, gq_ref[...]).astype(BF16)
    qq = _dot(qan, wqb_ref[...])
    ckv_ref[...] = _rms(kv_a, gkv_ref[...]).astype(ckv_ref.dtype)
    if rope:
        width = MLA_HEADS * Q_PAD
        cosq = cosq_ref[...]
        sinq = sinq_ref[...]
        for hd in range(MLA_HEADS):
            a = qq[:, hd * Q_PAD:(hd + 1) * Q_PAD]
            b = qq[:, width + hd * Q_PAD:width + (hd + 1) * Q_PAD]
            q_ref[:, hd * Q_PAD:(hd + 1) * Q_PAD] = (a * cosq + b * sinq).astype(BF16)
        y = krx * csk_ref[...]
        y = y + pltpu.roll(y, QK_ROPE, 1)
        lane = lax.broadcasted_iota(jnp.int32, y.shape, 1)
        krp_ref[...] = jnp.where(lane < QK_ROPE, y, 0.0).astype(BF16)
    else:
        q_ref[...] = (qq * MLA_QSCALE).astype(BF16)
        kr_ref[...] = krx[:, 0:QK_ROPE]
        krp_ref[...] = krx.astype(BF16)


def _pre_attention(x, mod, g_attn, w_in, g_q_a, w_qb, g_kv_a, rope_tabs, *, rope, mod_row):
    n = x.shape[0]
    tm = ROW_TILE
    row = lambda i: (i, 0)
    in_specs = [pl.BlockSpec((tm, D_MODEL), row),
                pl.BlockSpec((1, 6, D_MODEL), lambda i: (mod_row(i), 0, 0)),
                _const_spec((1, D_MODEL)),
                _const_spec(w_in.shape),
                _const_spec((1, Q_LORA)),
                _const_spec(w_qb.shape),
                _const_spec((1, KV_LORA))]
    args = [x, mod, g_attn, w_in, g_q_a, w_qb, g_kv_a]
    qw = MLA_HEADS * Q_PAD
    if rope:
        tiles_per_seq = rope_tabs[0].shape[0] // tm
        pos = lambda i: (i % tiles_per_seq, 0)
        in_specs += [pl.BlockSpec((tm, Q_PAD), pos), pl.BlockSpec((tm, Q_PAD), pos),
                     pl.BlockSpec((tm, LANE), pos)]
        args += list(rope_tabs)
        out_shape = [jax.ShapeDtypeStruct((n, NA_WIDTH), BF16)] * 3 + [
            jax.ShapeDtypeStruct((n, qw), BF16),
            jax.ShapeDtypeStruct((n, KV_LORA), BF16),
            jax.ShapeDtypeStruct((n, LANE), BF16)]
        out_specs = [pl.BlockSpec((tm, NA_WIDTH), row)] * 3 + [
            pl.BlockSpec((tm, qw), row), pl.BlockSpec((tm, KV_LORA), row), pl.BlockSpec((tm, LANE), row)]
    else:
        out_shape = [jax.ShapeDtypeStruct((n, NA_WIDTH), BF16),
                     jax.ShapeDtypeStruct((n, NA_WIDTH), F32),
                     jax.ShapeDtypeStruct((n, NA_WIDTH), F32),
                     jax.ShapeDtypeStruct((n, qw), BF16),
                     jax.ShapeDtypeStruct((n, KV_LORA), F32),
                     jax.ShapeDtypeStruct((n, QK_ROPE), F32),
                     jax.ShapeDtypeStruct((n, LANE), BF16)]
        out_specs = [pl.BlockSpec((tm, NA_WIDTH), row)] * 3 + [
            pl.BlockSpec((tm, qw), row), pl.BlockSpec((tm, KV_LORA), row),
            pl.BlockSpec((tm, QK_ROPE), row), pl.BlockSpec((tm, LANE), row)]
    return pl.pallas_call(
        functools.partial(_pre_kernel, rope=rope),
        grid=(n // tm,),
        in_specs=in_specs,
        out_specs=out_specs,
        out_shape=out_shape,
        compiler_params=_cparams(("arbitrary",)),
        name="pre_attention_rope" if rope else "pre_attention",
    )(*args)


def _softmax_pv(s, v):
    m = jnp.max(s, axis=-1, keepdims=True)
    p = jnp.exp2(s - m)
    l = jnp.sum(p, axis=-1, keepdims=True)
    return _dot(p.astype(BF16), v) / l


def _prompt_attn_kernel(naq_ref, nak_ref, nav_ref, q_ref, ckv_ref, krp_ref, wkvb_ref, ona_ref, omla_ref):
    kv = _dot(ckv_ref[...].astype(BF16), wkvb_ref[...])
    krp = krp_ref[...]
    for hd in range(NA_HEADS):
        sl = slice(hd * NA_HEAD_DIM, (hd + 1) * NA_HEAD_DIM)
        s = _dot_nt(naq_ref[:, sl], nak_ref[:, sl].astype(BF16))
        ona_ref[:, sl] = _softmax_pv(s, nav_ref[:, sl].astype(BF16)).astype(BF16)
    kvw = QK_NOPE + V_DIM
    for hd in range(MLA_HEADS):
        kf = jnp.concatenate([kv[:, hd * kvw:hd * kvw + QK_NOPE].astype(BF16), krp], axis=-1)
        s = _dot_nt(q_ref[:, hd * Q_PAD:(hd + 1) * Q_PAD], kf)
        v = kv[:, hd * kvw + QK_NOPE:(hd + 1) * kvw].astype(BF16)
        omla_ref[:, hd * V_DIM:(hd + 1) * V_DIM] = _softmax_pv(s, v).astype(BF16)


def _prompt_attention(naq, nak, nav, q, ckv, krp, w_kvb, seq):
    n = naq.shape[0]
    row = lambda b: (b, 0)
    return pl.pallas_call(
        _prompt_attn_kernel,
        grid=(n // seq,),
        in_specs=[pl.BlockSpec((seq, NA_WIDTH), row)] * 3 + [
            pl.BlockSpec((seq, MLA_HEADS * Q_PAD), row),
            pl.BlockSpec((seq, KV_LORA), row),
            pl.BlockSpec((seq, LANE), row),
            _const_spec(w_kvb.shape)],
        out_specs=[pl.BlockSpec((seq, NA_WIDTH), row), pl.BlockSpec((seq, MLA_WIDTH), row)],
        out_shape=[jax.ShapeDtypeStruct((n, NA_WIDTH), BF16), jax.ShapeDtypeStruct((n, MLA_WIDTH), BF16)],
        compiler_params=_cparams(("arbitrary",)),
        name="prompt_attention",
    )(naq, nak, nav, q, ckv, krp, w_kvb)


def _na_kernel(q_ref, k_ref, v_ref, kc_ref, vc_ref, bias_ref, o_ref, *, rows):
    rt = pl.program_id(2)
    ws = jnp.clip(rt * NA_Q_ROWS - NA_KR // 2, 0, rows - NA_WIN_ROWS)
    start = pl.multiple_of(ws * GRID_W, GRID_W)
    nwin = NA_WIN_ROWS * GRID_W
    q = q_ref[...]
    s_loc = _dot_nt(q, k_ref[pl.ds(start, nwin), :]) + bias_ref[0, 0]
    s_ctx = _dot_nt(q, kc_ref[0].astype(BF16))
    m = jnp.maximum(jnp.max(s_loc, axis=-1, keepdims=True), jnp.max(s_ctx, axis=-1, keepdims=True))
    p_loc = jnp.exp2(s_loc - m)
    p_ctx = jnp.exp2(s_ctx - m)
    l = jnp.sum(p_loc, axis=-1, keepdims=True) + jnp.sum(p_ctx, axis=-1, keepdims=True)
    o = _dot(p_loc.astype(BF16), v_ref[pl.ds(start, nwin), :]) + _dot(p_ctx.astype(BF16), vc_ref[0].astype(BF16))
    o_ref[...] = (o / l).astype(BF16)


def _na_bias_table(rpb, rows):
    nh = rpb.shape[0]
    a = np.arange(NA_Q_ROWS)
    b = np.arange(NA_WIN_ROWS)
    col = np.arange(GRID_W)
    cs = np.clip(col - NA_KC // 2, 0, GRID_W - NA_KC)
    valid_col = (col[None, :] >= cs[:, None]) & (col[None, :] < cs[:, None] + NA_KC)
    padw = GRID_W - NA_KC
    rp = jnp.pad(rpb, ((0, 0), (0, 0), (padw, padw)))
    toep = jnp.stack([rp[:, :, GRID_W - 1 - qc:2 * GRID_W - 1 - qc] for qc in range(GRID_W)], axis=2)
    toep = jnp.where(valid_col[None, None], toep, NEG_BIG)
    masked = jnp.full((nh, GRID_W, GRID_W), NEG_BIG, F32)
    tabs = []
    for r0 in (0, NA_Q_ROWS, rows - NA_Q_ROWS):
        ws = int(np.clip(r0 - NA_KR // 2, 0, rows - NA_WIN_ROWS))
        r = r0 + a
        rs = np.clip(r - NA_KR // 2, 0, rows - NA_KR)
        kr = ws + b
        valid_row = (kr[None, :] >= rs[:, None]) & (kr[None, :] < rs[:, None] + NA_KR)
        dr = kr[None, :] - r[:, None] + NA_KR - 1
        tile_rows = []
        for ai in range(NA_Q_ROWS):
            blocks = [toep[:, int(dr[ai, bi])] if valid_row[ai, bi] else masked for bi in range(NA_WIN_ROWS)]
            tile_rows.append(jnp.concatenate(blocks, axis=-1))
        tabs.append(jnp.concatenate(tile_rows, axis=1))
    return jnp.stack(tabs)


def _sample_na(naq, nak, nav, kc, vc, bias, batch, seq):
    rows = seq // GRID_W
    tq = NA_Q_ROWS * GRID_W
    nt = seq // tq
    last = nt - 1

    def pat(rt):
        return jnp.where(rt == 0, 0, jnp.where(rt == last, 2, 1))

    return pl.pallas_call(
        functools.partial(_na_kernel, rows=rows),
        grid=(batch, NA_HEADS, nt),
        in_specs=[pl.BlockSpec((tq, NA_HEAD_DIM), lambda b, h, r: (b * nt + r, h)),
                  pl.BlockSpec((seq, NA_HEAD_DIM), lambda b, h, r: (b, h)),
                  pl.BlockSpec((seq, NA_HEAD_DIM), lambda b, h, r: (b, h)),
                  pl.BlockSpec((1, kc.shape[1], NA_HEAD_DIM), lambda b, h, r: (b, 0, h)),
                  pl.BlockSpec((1, vc.shape[1], NA_HEAD_DIM), lambda b, h, r: (b, 0, h)),
                  pl.BlockSpec((1, 1, tq, NA_WIN_ROWS * GRID_W), lambda b, h, r: (pat(r), h, 0, 0))],
        out_specs=pl.BlockSpec((tq, NA_HEAD_DIM), lambda b, h, r: (b * nt + r, h)),
        out_shape=jax.ShapeDtypeStruct((batch * seq, NA_WIDTH), BF16),
        compiler_params=_cparams(("arbitrary", "arbitrary", "arbitrary")),
        name="sample_neighbourhood_attention",
    )(naq, nak, nav, kc, vc, bias)


def _kv_expand_kernel(ckv_ref, krp_ref, w_ref, kf_ref, v_ref):
    kv = _dot(ckv_ref[0], w_ref[...])
    krp = krp_ref[0]
    kvw = QK_NOPE + V_DIM
    for hd in range(MLA_HEADS):
        kf_ref[0, hd, :, 0:QK_NOPE] = kv[:, hd * kvw:hd * kvw + QK_NOPE].astype(BF16)
        kf_ref[0, hd, :, QK_NOPE:Q_PAD] = krp
        v_ref[0, hd] = kv[:, hd * kvw + QK_NOPE:(hd + 1) * kvw].astype(BF16)


def _kv_expand(ckv, krp, w_kvb):
    batch, nkeys, _ = ckv.shape
    tm = 512
    return pl.pallas_call(
        _kv_expand_kernel,
        grid=(batch, nkeys // tm),
        in_specs=[pl.BlockSpec((1, tm, KV_LORA), lambda b, t: (b, t, 0)),
                  pl.BlockSpec((1, tm, LANE), lambda b, t: (b, t, 0)),
                  _const_spec(w_kvb.shape)],
        out_specs=[pl.BlockSpec((1, MLA_HEADS, tm, Q_PAD), lambda b, t: (b, 0, t, 0)),
                   pl.BlockSpec((1, MLA_HEADS, tm, V_DIM), lambda b, t: (b, 0, t, 0))],
        out_shape=[jax.ShapeDtypeStruct((batch, MLA_HEADS, nkeys, Q_PAD), BF16),
                   jax.ShapeDtypeStruct((batch, MLA_HEADS, nkeys, V_DIM), BF16)],
        compiler_params=_cparams(("arbitrary", "arbitrary")),
        name="latent_kv_expand",
    )(ckv, krp, w_kvb)


def _mla_kernel(q_ref, kf_ref, v_ref, o_ref, *, nkeys):
    q = q_ref[...]
    tq = q.shape[0]
    m = jnp.full((tq, 1), NEG_BIG, F32)
    l = jnp.zeros((tq, 1), F32)
    acc = jnp.zeros((tq, V_DIM), F32)
    for c in range(nkeys // MLA_TK):
        ks = slice(c * MLA_TK, (c + 1) * MLA_TK)
        s = _dot_nt(q, kf_ref[0, 0, ks, :])
        m_new = jnp.maximum(m, jnp.max(s, axis=-1, keepdims=True))
        alpha = jnp.exp2(m - m_new)
        p = jnp.exp2(s - m_new)
        l = alpha * l + jnp.sum(p, axis=-1, keepdims=True)
        acc = alpha * acc + _dot(p.astype(BF16), v_ref[0, 0, ks, :])
        m = m_new
    o_ref[...] = (acc / l).astype(BF16)


def _sample_mla(q, kf, v, batch, seq):
    nkeys = kf.shape[2]
    nt = seq // MLA_TQ
    return pl.pallas_call(
        functools.partial(_mla_kernel, nkeys=nkeys),
        grid=(batch, MLA_HEADS, nt),
        in_specs=[pl.BlockSpec((MLA_TQ, Q_PAD), lambda b, h, t: (b * nt + t, h)),
                  pl.BlockSpec((1, 1, nkeys, Q_PAD), lambda b, h, t: (b, h, 0, 0)),
                  pl.BlockSpec((1, 1, nkeys, V_DIM), lambda b, h, t: (b, h, 0, 0))],
        out_specs=pl.BlockSpec((MLA_TQ, V_DIM), lambda b, h, t: (b * nt + t, h)),
        out_shape=jax.ShapeDtypeStruct((batch * seq, MLA_WIDTH), BF16),
        compiler_params=_cparams(("arbitrary", "arbitrary", "arbitrary")),
        name="sample_latent_attention",
    )(q, kf, v)


def _post_kernel(xp_ref, xs_ref, onp_ref, omp_ref, ons_ref, oms_ref, mod_ref, wout_ref, gffn_ref, wrh_ref, wrl_ref,
                 br_ref, x1_ref, h2p_ref, idx_ref, gate_ref, *, n_prompt_tiles):
    is_prompt = pl.program_id(0) < n_prompt_tiles
    ga = mod_ref[0, 2:3, :]
    sf = mod_ref[0, 3:4, :]
    scf = mod_ref[0, 4:5, :]
    tm = ROW_TILE
    for r0 in range(0, xp_ref.shape[0], tm):
        rs = slice(r0, r0 + tm)
        x = jnp.where(is_prompt, xp_ref[rs, :], xs_ref[rs, :])
        ona = jnp.where(is_prompt, onp_ref[rs, :], ons_ref[rs, :])
        omla = jnp.where(is_prompt, omp_ref[rs, :], oms_ref[rs, :])
        o = _dot(ona, wout_ref[0:NA_WIDTH, :]) + _dot(omla, wout_ref[NA_WIDTH:NA_WIDTH + MLA_WIDTH, :])
        x1 = x + ga * o
        x1_ref[rs, :] = x1
        h2 = _rms(x1, gffn_ref[...]) * (1.0 + scf) + sf
        h_hi = h2.astype(BF16)
        h_lo = (h2 - h_hi.astype(F32)).astype(BF16)
        logits = _dot(h_hi, wrh_ref[...]) + _dot(h_lo, wrh_ref[...]) + _dot(h_hi, wrl_ref[...]) + br_ref[...]
        lane_e = lax.broadcasted_iota(jnp.int32, logits.shape, 1).astype(F32)
        lane_o = lax.broadcasted_iota(jnp.int32, (tm, LANE), 1)
        idx_out = jnp.zeros((tm, LANE), F32)
        gate_out = jnp.zeros((tm, LANE), F32)
        top0 = None
        denom = jnp.zeros((tm, 1), F32)
        cur = logits
        for k in range(TOP_K):
            mx = jnp.max(cur, axis=-1, keepdims=True)
            ix = jnp.min(jnp.where(cur == mx, lane_e, float(N_EXPERTS)), axis=-1, keepdims=True)
            cur = jnp.where(lane_e == ix, -jnp.inf, cur)
            if k == 0:
                top0 = mx
            e = jnp.exp(mx - top0)
            denom = denom + e
            idx_out = jnp.where(lane_o == k, ix, idx_out)
            gate_out = jnp.where(lane_o == k, e, gate_out)
        idx_ref[rs, :] = idx_out.astype(jnp.int32)
        gate_ref[rs, :] = gate_out / denom
        bits = pltpu.bitcast(h_hi.astype(F32), U32)
        for s in range(PACK_ROWS):
            lo = bits[:, s * LANE:(s + 1) * LANE] >> 16
            hi = bits[:, (s + PACK_ROWS) * LANE:(s + PACK_ROWS + 1) * LANE] & jnp.uint32(0xFFFF0000)
            h2p_ref[pl.ds(r0 * PACK_ROWS + s, tm, stride=PACK_ROWS), :] = hi | lo


def _post_attention(xp, xs, onp, omp, ons, oms, mod, w_out, g_ffn, w_router, b_router, *, mod_row):
    tm = POST_TILE
    wr_hi = w_router.astype(BF16)
    wr_lo = (w_router - wr_hi.astype(F32)).astype(BF16)
    npt = xp.shape[0] // tm
    nst = xs.shape[0] // tm
    n = xp.shape[0] + xs.shape[0]
    pidx = lambda i: (jnp.minimum(i, npt - 1), 0)
    sidx = lambda i: (jnp.maximum(i - npt, 0), 0)
    row = lambda i: (i, 0)
    return pl.pallas_call(
        functools.partial(_post_kernel, n_prompt_tiles=npt),
        grid=(npt + nst,),
        in_specs=[pl.BlockSpec((tm, D_MODEL), pidx), pl.BlockSpec((tm, D_MODEL), sidx),
                  pl.BlockSpec((tm, NA_WIDTH), pidx), pl.BlockSpec((tm, MLA_WIDTH), pidx),
                  pl.BlockSpec((tm, NA_WIDTH), sidx), pl.BlockSpec((tm, MLA_WIDTH), sidx),
                  pl.BlockSpec((1, 6, D_MODEL), lambda i: (mod_row(i), 0, 0)),
                  _const_spec(w_out.shape), _const_spec((1, D_MODEL)),
                  _const_spec(w_router.shape), _const_spec(w_router.shape), _const_spec((1, N_EXPERTS))],
        out_specs=[pl.BlockSpec((tm, D_MODEL), row),
                   pl.BlockSpec((tm * PACK_ROWS, LANE), row),
                   pl.BlockSpec((tm, LANE), row),
                   pl.BlockSpec((tm, LANE), row)],
        out_shape=[jax.ShapeDtypeStruct((n, D_MODEL), F32),
                   jax.ShapeDtypeStruct((n * PACK_ROWS, LANE), U32),
                   jax.ShapeDtypeStruct((n, LANE), jnp.int32),
                   jax.ShapeDtypeStruct((n, LANE), F32)],
        compiler_params=_cparams(("arbitrary",)),
        name="post_attention_router",
    )(xp, xs, onp, omp, ons, oms, mod, w_out, g_ffn, wr_hi, wr_lo, b_router)


def _deinterleave_kernel(w_ref, p_ref, o_ref):
    p = p_ref[...]
    grp = p.shape[0]
    per_tile = o_ref.shape[3] // grp
    for c in range(w_ref.shape[2] // grp):
        w = w_ref[0, :, c * grp:(c + 1) * grp].astype(BF16)
        col = (c % per_tile) * grp
        o_ref[0, c // per_tile, :, col:col + grp] = _dot(w, p).astype(BF16)


def _deinterleave_gate_up(w_gate_up):
    ne, d, n2 = w_gate_up.shape
    nj = n2 // (2 * MOE_TF)
    grp = 2 * LANE
    dst = np.arange(grp)
    src = np.where(dst < LANE, 2 * dst, 2 * (dst - LANE) + 1)
    perm = np.zeros((grp, grp), np.float32)
    perm[src, dst] = 1.0
    tk = 512
    return pl.pallas_call(
        _deinterleave_kernel,
        grid=(ne, d // tk),
        in_specs=[pl.BlockSpec((1, tk, n2), lambda e, k: (e, k, 0)), _const_spec((grp, grp))],
        out_specs=pl.BlockSpec((1, nj, tk, 2 * MOE_TF), lambda e, k: (e, 0, k, 0)),
        out_shape=jax.ShapeDtypeStruct((ne, nj, d, 2 * MOE_TF), BF16),
        compiler_params=_cparams(("arbitrary", "arbitrary")),
        name="deinterleave_gate_up",
    )(w_gate_up, jnp.asarray(perm, BF16))


def _moe_kernel(ce_ref, nv_ref, cs_ref, order_ref, h_ref, wgu_ref, bgu_ref, wd_ref, bd_ref, y_ref,
                xg_ref, xbf_ref, acc_ref, ost_ref, sem_g, sem_s):
    c = pl.program_id(0)
    j = pl.program_id(1)
    last_c = pl.num_programs(0) - 1
    last_j = pl.num_programs(1) - 1
    nk = order_ref.shape[0]
    nsub = MOE_CHUNK // MOE_SUB
    xg_rows = MOE_CHUNK * PACK_ROWS

    nv = nv_ref[c]
    c_next = jnp.minimum(c + 1, last_c)
    nv_next = jnp.where(c < last_c, nv_ref[c_next], 0)
    cs_next = cs_ref[c_next]
    c_prev = jnp.maximum(c - 1, 0)
    nv_prev = jnp.where(c > 0, nv_ref[c_prev], 0)
    cs_prev = cs_ref[c_prev]
    nv_prev2 = jnp.where(c > 1, nv_ref[jnp.maximum(c - 2, 0)], 0)
    ost_rows = MOE_CHUNK * OUT_ROWS
    slot = c % 2
    slot_next = (c + 1) % 2

    def gather_row(cs, dst_slot, row):
        tok = jnp.right_shift(order_ref[jnp.minimum(cs + row, nk - 1)], TOP_K_SHIFT)
        src = pl.multiple_of(tok * PACK_ROWS, PACK_ROWS)
        dst = pl.multiple_of(dst_slot * xg_rows + row * PACK_ROWS, PACK_ROWS)
        pltpu.make_async_copy(h_ref.at[pl.ds(src, PACK_ROWS)], xg_ref.at[pl.ds(dst, PACK_ROWS)],
                              sem_g.at[dst_slot]).start()

    def scatter_row(row):
        flat = order_ref[jnp.minimum(cs_prev + row, nk - 1)]
        dest = jnp.where(row < nv_prev, flat, nk + slot_next * MOE_CHUNK + row)
        src = pl.multiple_of(slot_next * ost_rows + row * OUT_ROWS, OUT_ROWS)
        dst = pl.multiple_of(dest * OUT_ROWS, OUT_ROWS)
        pltpu.make_async_copy(ost_ref.at[pl.ds(src, OUT_ROWS)], y_ref.at[pl.ds(dst, OUT_ROWS)],
                              sem_s.at[slot_next]).start()

    def rolled(fn, row0, nrows):
        def body(r0, carry):
            for u in range(DMA_UNROLL):
                fn(row0 + r0 * DMA_UNROLL + u)
            return carry

        lax.fori_loop(0, nrows // DMA_UNROLL, body, 0)

    @pl.when(jnp.logical_and(jnp.logical_and(c == 0, j == 0), nv > 0))
    def _():
        rolled(lambda row: gather_row(cs_ref[0], 0, row), 0, MOE_CHUNK)

    def unpack(sb):
        for s in range(PACK_ROWS):
            w = xg_ref[pl.ds(slot * xg_rows + sb * MOE_SUB * PACK_ROWS + s, MOE_SUB, stride=PACK_ROWS), :]
            lo = pltpu.bitcast(w << 16, F32)
            hi = pltpu.bitcast(w & jnp.uint32(0xFFFF0000), F32)
            xbf_ref[sb * MOE_SUB:(sb + 1) * MOE_SUB, s * LANE:(s + 1) * LANE] = lo.astype(BF16)
            xbf_ref[sb * MOE_SUB:(sb + 1) * MOE_SUB,
                    (s + PACK_ROWS) * LANE:(s + PACK_ROWS + 1) * LANE] = hi.astype(BF16)

    @pl.when(jnp.logical_and(j == 0, nv > 0))
    def _():
        base = pl.multiple_of(slot * xg_rows, xg_rows)
        pltpu.make_async_copy(h_ref.at[pl.ds(0, xg_rows)], xg_ref.at[pl.ds(base, xg_rows)], sem_g.at[slot]).wait()

        @pl.when(c == 0)
        def _():
            acc_ref[...] = jnp.zeros((MOE_CHUNK, D_MODEL), F32)
            ost_ref[...] = jnp.zeros(ost_ref.shape, F32)

        @pl.when(nv == MOE_CHUNK)
        def _():
            for sb in range(nsub):
                unpack(sb)

        @pl.when(nv < MOE_CHUNK)
        def _():
            for sb in range(nsub):
                @pl.when(sb * MOE_SUB < nv)
                def _():
                    unpack(sb)

    wd = wd_ref[0].astype(BF16)

    def sub_block(sb):
        x = xbf_ref[sb * MOE_SUB:(sb + 1) * MOE_SUB, :]
        gu = _dot(x, wgu_ref[0, 0]) + bgu_ref[0]
        ng = MOE_TF // LANE
        g = jnp.concatenate([gu[:, 2 * b * LANE:(2 * b + 1) * LANE] for b in range(ng)], axis=-1)
        u = jnp.concatenate([gu[:, (2 * b + 1) * LANE:(2 * b + 2) * LANE] for b in range(ng)], axis=-1)
        g = jnp.minimum(g, SWIGLU_LIMIT)
        u = jnp.clip(u, -SWIGLU_LIMIT, SWIGLU_LIMIT)
        act = g * (1.0 / (1.0 + jnp.exp(-(g * SWIGLU_ALPHA)))) * (u + 1.0)
        rows = slice(sb * MOE_SUB, (sb + 1) * MOE_SUB)
        prev = jnp.where(j == 0, jnp.broadcast_to(bd_ref[0], (MOE_SUB, D_MODEL)), acc_ref[rows, :])
        acc_ref[rows, :] = prev + _dot(act.astype(BF16), wd)

    dma_both = jnp.logical_and(nv_next > 0, nv_prev > 0)
    per_sb = MOE_SUB // nsub
    for sb in range(nsub):
        row0 = j * MOE_SUB + sb * per_sb
        compute = sb * MOE_SUB < nv
        fused = jnp.logical_and(compute, dma_both)

        @pl.when(fused)
        def _():
            for r in range(per_sb):
                gather_row(cs_next, slot_next, row0 + r)
                scatter_row(row0 + r)
            sub_block(sb)

        @pl.when(jnp.logical_not(fused))
        def _():
            @pl.when(nv_next > 0)
            def _():
                for r in range(per_sb):
                    gather_row(cs_next, slot_next, row0 + r)

            @pl.when(nv_prev > 0)
            def _():
                for r in range(per_sb):
                    scatter_row(row0 + r)

            @pl.when(compute)
            def _():
                sub_block(sb)

    def wait_scatter(buf):
        base = pl.multiple_of(buf * ost_rows, ost_rows)
        pltpu.make_async_copy(ost_ref.at[pl.ds(base, ost_rows)], y_ref.at[pl.ds(0, ost_rows)], sem_s.at[buf]).wait()

    @pl.when(j == last_j)
    def _():
        @pl.when(nv_prev2 > 0)
        def _():
            wait_scatter(slot)

        for sb in range(nsub):
            @pl.when(sb * MOE_SUB < nv)
            def _():
                for s in range(OUT_ROWS):
                    dst = pl.ds(slot * ost_rows + sb * MOE_SUB * OUT_ROWS + s, MOE_SUB, stride=OUT_ROWS)
                    ost_ref[dst, :] = acc_ref[sb * MOE_SUB:(sb + 1) * MOE_SUB, s * LANE:(s + 1) * LANE]

        @pl.when(jnp.logical_and(c == last_c, nv_prev > 0))
        def _():
            wait_scatter(slot_next)


def _moe(chunk_expert, chunk_rows, chunk_start, order, h2p, wgu, bgu, w_down, b_down):
    nj = D_FF // MOE_TF
    ngrid = chunk_expert.shape[0]
    nk = order.shape[0]

    def jj(c, j, nv):
        return jnp.where(nv[c] > 0, j, nj - 1)

    return pl.pallas_call(
        _moe_kernel,
        grid_spec=pltpu.PrefetchScalarGridSpec(
            num_scalar_prefetch=4,
            grid=(ngrid, nj),
            in_specs=[
                pl.BlockSpec(memory_space=pl.ANY),
                pl.BlockSpec((1, 1, D_MODEL, 2 * MOE_TF), lambda c, j, ce, nv, cs, od: (ce[c], jj(c, j, nv), 0, 0)),
                pl.BlockSpec((1, 1, 2 * MOE_TF), lambda c, j, ce, nv, cs, od: (ce[c], 0, jj(c, j, nv))),
                pl.BlockSpec((1, MOE_TF, D_MODEL), lambda c, j, ce, nv, cs, od: (ce[c], jj(c, j, nv), 0)),
                pl.BlockSpec((1, 1, D_MODEL), lambda c, j, ce, nv, cs, od: (ce[c], 0, 0))],
            out_specs=pl.BlockSpec(memory_space=pl.ANY),
            scratch_shapes=[pltpu.VMEM((2 * MOE_CHUNK * PACK_ROWS, LANE), U32),
                            pltpu.VMEM((MOE_CHUNK, D_MODEL), BF16),
                            pltpu.VMEM((MOE_CHUNK, D_MODEL), F32),
                            pltpu.VMEM((2 * MOE_CHUNK * OUT_ROWS, LANE), F32),
                            pltpu.SemaphoreType.DMA((2,)),
                            pltpu.SemaphoreType.DMA((2,))]),
        out_shape=jax.ShapeDtypeStruct(((nk + 2 * MOE_CHUNK) * OUT_ROWS, LANE), F32),
        compiler_params=_cparams(("arbitrary", "arbitrary")),
        name="moe_experts",
    )(chunk_expert, chunk_rows, chunk_start, order, h2p, wgu, bgu, w_down, b_down)


def _combine_kernel(gate_ref, x1_ref, mod_ref, gfin_ref, ye_ref, y_ref):
    gate = gate_ref[...]
    pieces = []
    for s in range(OUT_ROWS):
        acc = None
        for k in range(TOP_K):
            rows = ye_ref[pl.ds(k * OUT_ROWS + s, COMBINE_TOKENS, stride=TOP_K * OUT_ROWS), :]
            term = gate[:, k:k + 1] * rows
            acc = term if acc is None else acc + term
        pieces.append(acc)
    y = jnp.concatenate(pieces, axis=-1)
    gf = mod_ref[0, 5:6, :]
    y_ref[...] = _rms(x1_ref[...] + gf * y, gfin_ref[...])


def _combine(gates, x1, mod, g_final, y_experts, *, tile0, ntiles, mod_row):
    tt = COMBINE_TOKENS
    return pl.pallas_call(
        _combine_kernel,
        grid=(ntiles,),
        in_specs=[pl.BlockSpec((tt, LANE), lambda i: (tile0 + i, 0)),
                  pl.BlockSpec((tt, D_MODEL), lambda i: (tile0 + i, 0)),
                  pl.BlockSpec((1, 6, D_MODEL), lambda i: (mod_row(i), 0, 0)),
                  _const_spec((1, D_MODEL)),
                  pl.BlockSpec((tt * TOP_K * OUT_ROWS, LANE), lambda i: (tile0 + i, 0))],
        out_specs=pl.BlockSpec((tt, D_MODEL), lambda i: (i, 0)),
        out_shape=jax.ShapeDtypeStruct((ntiles * tt, D_MODEL), F32),
        compiler_params=_cparams(("arbitrary",)),
        name="moe_combine_final_norm",
    )(gates, x1, mod, g_final, y_experts)


def _routing(top_idx, nch):
    n = top_idx.shape[0]
    nk = n * TOP_K
    flat_e = top_idx.reshape(nk)
    order = jnp.argsort(flat_e, stable=True).astype(jnp.int32)
    onehot = flat_e[:, None] == jnp.arange(N_EXPERTS, dtype=jnp.int32)[None, :]
    counts = jnp.sum(onehot.astype(jnp.int32), axis=0)
    grp_start = jnp.cumsum(counts) - counts
    chunks_e = (counts + MOE_CHUNK - 1) // MOE_CHUNK
    chunk_end = jnp.cumsum(chunks_e)
    chunk_start = chunk_end - chunks_e

    total = chunk_end[-1]
    cidx = jnp.arange(nch + 1, dtype=jnp.int32)
    active = cidx < total
    ce = jnp.minimum(jnp.searchsorted(chunk_end, cidx, side='right'), N_EXPERTS - 1).astype(jnp.int32)
    local = (cidx - chunk_start[ce]) * MOE_CHUNK
    nv = jnp.where(active, jnp.clip(counts[ce] - local, 0, MOE_CHUNK), 0).astype(jnp.int32)
    last = jnp.maximum(total - 1, 0)
    chunk_expert = jnp.where(active, ce, ce[last]).astype(jnp.int32)
    chunk_first = jnp.where(active, jnp.clip(grp_start[ce] + local, 0, nk - 1), 0).astype(jnp.int32)
    return order, chunk_expert, nv, chunk_first


def _rope_tables(t):
    pos = jnp.arange(t)
    rows = (pos // GRID_W).astype(F32)
    cols = (pos % GRID_W).astype(F32)
    inv = ROPE_THETA ** (-(jnp.arange(ROPE_AXIS // 2, dtype=F32) * 2.0 / ROPE_AXIS))
    ar = rows[:, None] * inv
    ac = cols[:, None] * inv
    ang = jnp.concatenate([ar, ar, ac, ac], axis=-1)
    return jnp.cos(ang), jnp.sin(ang)


def _rot_cols(w):
    half = ROPE_AXIS // 2
    src = np.concatenate([np.arange(half, ROPE_AXIS), np.arange(0, half),
                          np.arange(ROPE_AXIS + half, 2 * ROPE_AXIS), np.arange(ROPE_AXIS, ROPE_AXIS + half)])
    sign = np.concatenate([-np.ones(half), np.ones(half), -np.ones(half), np.ones(half)]).astype(np.float32)
    return w[..., src] * sign


def kernel(x_prompt, x_sample, cache_na_k, cache_na_v, cache_mla_ckv, cache_mla_krope, c, c_ctx, g_attn, g_ffn, g_final, w_mod, b_mod, w_in, w_out, na_rpb, g_q_a, w_q_b, g_kv_a, w_kv_b, w_router, b_router, w_gate_up, b_gate_up, w_down, b_down):
    bp, sp, d = x_prompt.shape
    bd, td, _ = x_sample.shape
    assert d == D_MODEL and w_mod.shape[0] == 1, "one trunk layer of width D_MODEL"
    n_p = bp * sp
    n_s = bd * td
    xp = x_prompt.reshape(n_p, d)
    xs = x_sample.reshape(n_s, d)

    c8 = jnp.zeros((8, d), F32).at[0].set(c_ctx).at[1:1 + bd].set(c)
    mod = _modulation(c8, w_mod[0], b_mod[0].reshape(1, -1)).reshape(8, 6, d)

    w_in0 = w_in[0]
    w_kr = w_in0[:, KR_OFF:KR_OFF + QK_ROPE]
    w_in_p = jnp.concatenate([w_in0, jnp.zeros((d, LANE - QK_ROPE), F32)], axis=1).astype(BF16)
    w_in_s = jnp.concatenate([w_in0, _rot_cols(w_kr)], axis=1).astype(BF16)
    wq = w_q_b[0].reshape(Q_LORA, MLA_HEADS, QK_NOPE + QK_ROPE)
    zpad = jnp.zeros((Q_LORA, MLA_HEADS, Q_PAD - QK_NOPE - QK_ROPE), F32)
    wq_pad = jnp.concatenate([wq, zpad], axis=-1).reshape(Q_LORA, MLA_HEADS * Q_PAD)
    wq_rot = jnp.concatenate([jnp.zeros((Q_LORA, MLA_HEADS, QK_NOPE), F32), _rot_cols(wq[..., QK_NOPE:]), zpad],
                             axis=-1).reshape(Q_LORA, MLA_HEADS * Q_PAD)
    wqb_p = wq_pad.astype(BF16)
    wqb_s = jnp.concatenate([wq_pad, wq_rot], axis=1).astype(BF16)
    w_kvb = w_kv_b[0].astype(BF16)
    cos, sin = _rope_tables(td)
    cosq = jnp.concatenate([jnp.ones((td, QK_NOPE), F32), cos, jnp.ones((td, Q_PAD - QK_NOPE - QK_ROPE), F32)], axis=1)
    sinq = jnp.concatenate([jnp.zeros((td, QK_NOPE), F32), sin, jnp.zeros((td, Q_PAD - QK_NOPE - QK_ROPE), F32)], axis=1)
    cosq = cosq * MLA_QSCALE
    sinq = sinq * MLA_QSCALE
    csk = jnp.concatenate([cos, sin], axis=1)

    tiles_per_seq = td // ROW_TILE
    g_attn2 = g_attn[0].reshape(1, d)
    gq2 = g_q_a[0].reshape(1, Q_LORA)
    gkv2 = g_kv_a[0].reshape(1, KV_LORA)

    naq_p, nak_p, nav_p, q_p, ckv_p, kr_p, krp_p = _pre_attention(
        xp, mod, g_attn2, w_in_p, gq2, wqb_p, gkv2, None, rope=False, mod_row=lambda i: 0)
    ona_p, omla_p = _prompt_attention(naq_p, nak_p, nav_p, q_p, ckv_p, krp_p, w_kvb, sp)

    naq_s, nak_s, nav_s, q_s, ckv_s, krp_s = _pre_attention(
        xs, mod, g_attn2, w_in_s, gq2, wqb_s, gkv2, (cosq, sinq, csk), rope=True,
        mod_row=lambda i: 1 + i // tiles_per_seq)
    past = cache_na_k.shape[2]
    kc = cache_na_k[:, 0].reshape(bd, past, NA_WIDTH)
    vc = cache_na_v[:, 0].reshape(bd, past, NA_WIDTH)
    bias = _na_bias_table(na_rpb[0] * LOG2E, td // GRID_W)
    ona_s = _sample_na(naq_s, nak_s, nav_s, kc, vc, bias, bd, td)
    ckv_all = jnp.concatenate([ckv_s.reshape(bd, td, KV_LORA), cache_mla_ckv[:, 0].astype(BF16)], axis=1)
    krp_c = jnp.concatenate([cache_mla_krope[:, 0], jnp.zeros((bd, past, LANE - QK_ROPE), F32)], axis=-1).astype(BF16)
    krp_all = jnp.concatenate([krp_s.reshape(bd, td, LANE), krp_c], axis=1)
    kf, vv = _kv_expand(ckv_all, krp_all, w_kvb)
    omla_s = _sample_mla(q_s, kf, vv, bd, td)

    npt = n_p // POST_TILE
    post_tiles_per_seq = td // POST_TILE
    x1, h2p, idx128, gate128 = _post_attention(
        xp, xs, ona_p, omla_p, ona_s, omla_s, mod, w_out[0].astype(BF16), g_ffn[0].reshape(1, d),
        w_router[0], b_router[0].reshape(1, N_EXPERTS),
        mod_row=lambda i: jnp.where(i < npt, 0, 1 + jnp.maximum(i - npt, 0) // post_tiles_per_seq))

    n = n_p + n_s
    nch = n * TOP_K // MOE_CHUNK + N_EXPERTS
    order, chunk_expert, chunk_rows, chunk_first = _routing(idx128[:, :TOP_K], nch)
    wgu = _deinterleave_gate_up(w_gate_up[0])
    ng = 2 * D_FF // (2 * LANE)
    bgu = b_gate_up[0].reshape(N_EXPERTS, ng, LANE, 2).transpose(0, 1, 3, 2).reshape(N_EXPERTS, 1, 2 * D_FF)
    y_experts = _moe(chunk_expert, chunk_rows, chunk_first, order, h2p, wgu, bgu, w_down[0],
                     b_down[0].reshape(N_EXPERTS, 1, d))

    tt = COMBINE_TOKENS
    gfin = g_final.reshape(1, d)
    ctiles_seq = td // tt
    y_p = _combine(gate128, x1, mod, gfin, y_experts, tile0=0, ntiles=n_p // tt, mod_row=lambda i: 0)
    y_s = _combine(gate128, x1, mod, gfin, y_experts, tile0=n_p // tt, ntiles=n_s // tt,
                   mod_row=lambda i: 1 + i // ctiles_seq)

    return (y_p.reshape(bp, sp, d), y_s.reshape(bd, td, d),
            nak_p.reshape(bp, 1, sp, NA_HEADS, NA_HEAD_DIM), nav_p.reshape(bp, 1, sp, NA_HEADS, NA_HEAD_DIM),
            ckv_p.reshape(bp, 1, sp, KV_LORA), kr_p.reshape(bp, 1, sp, QK_ROPE))
```

```python
import functools

import numpy as np
import jax
import jax.numpy as jnp
from jax import lax
from jax.experimental import pallas as pl
from jax.experimental.pallas import tpu as pltpu

F32 = jnp.float32
BF16 = jnp.bfloat16
U32 = jnp.uint32

D_MODEL = 2048
GRID_W = 64
NA_HEADS = 8
NA_HEAD_DIM = 128
NA_KR = 8
NA_KC = 16
MLA_HEADS = 8
Q_LORA = 512
KV_LORA = 256
QK_NOPE = 128
QK_ROPE = 64
V_DIM = 128
ROPE_AXIS = QK_ROPE // 2
ROPE_THETA = 10000.0
NA_WIDTH = NA_HEADS * NA_HEAD_DIM
MLA_WIDTH = MLA_HEADS * V_DIM
IN_COLS = 3 * NA_WIDTH + Q_LORA + KV_LORA + QK_ROPE
N_EXPERTS = 32
TOP_K = 4
D_FF = D_MODEL
SWIGLU_ALPHA = 1.702
SWIGLU_LIMIT = 7.0
EPS = 1e-6

LANE = 128
Q_PAD = 2 * LANE
KR_OFF = 3 * NA_WIDTH + Q_LORA + KV_LORA
IN_COLS_PAD = KR_OFF + LANE
VMEM_LIMIT = 56 * 1024 * 1024
NEG_BIG = -1e30

LOG2E = 1.4426950408889634
NA_QSCALE = NA_HEAD_DIM ** -0.5 * LOG2E
MLA_QSCALE = (QK_NOPE + QK_ROPE) ** -0.5 * LOG2E

ROW_TILE = 256
POST_TILE = 2 * ROW_TILE
NA_Q_ROWS = 4
NA_WIN_ROWS = 12
MLA_TQ = 512
MLA_TK = 1152
MOE_CHUNK = 1024
MOE_SUB = 256
MOE_TF = 512
COMBINE_TOKENS = 128
TOP_K_SHIFT = TOP_K.bit_length() - 1
DMA_UNROLL = 8
PACK_ROWS = D_MODEL // (2 * LANE)
OUT_ROWS = D_MODEL // LANE


def _cparams(sem):
    return pltpu.CompilerParams(dimension_semantics=sem, vmem_limit_bytes=VMEM_LIMIT)


def _const_spec(shape):
    nd = len(shape)
    return pl.BlockSpec(shape, lambda *a: (0,) * nd, pipeline_mode=pl.Buffered(1))


def _rms(x, g):
    return x * lax.rsqrt(jnp.mean(x * x, axis=-1, keepdims=True) + EPS) * g


def _dot(a, b):
    return jnp.dot(a, b, preferred_element_type=F32)


def _dot_nt(a, b):
    return lax.dot_general(a, b, (((1,), (1,)), ((), ())), preferred_element_type=F32)


def _mod_kernel(c_ref, w_ref, b_ref, o_ref):
    c = c_ref[...]
    s = c / (1.0 + jnp.exp(-c))
    o_ref[...] = _dot(s.astype(BF16), w_ref[...].astype(BF16)) + b_ref[...]


def _modulation(c8, w_mod, b_mod):
    n = w_mod.shape[1]
    tn = 1024
    return pl.pallas_call(
        _mod_kernel,
        grid=(n // tn,),
        in_specs=[pl.BlockSpec((8, D_MODEL), lambda j: (0, 0)),
                  pl.BlockSpec((D_MODEL, tn), lambda j: (0, j)),
                  pl.BlockSpec((1, tn), lambda j: (0, j))],
        out_specs=pl.BlockSpec((8, tn), lambda j: (0, j)),
        out_shape=jax.ShapeDtypeStruct((8, n), F32),
        compiler_params=_cparams(("arbitrary",)),
        name="modulation",
    )(c8, w_mod, b_mod)


def _pre_kernel(*refs, rope):
    if rope:
        (x_ref, mod_ref, g_ref, win_ref, gq_ref, wqb_ref, gkv_ref, cosq_ref, sinq_ref, csk_ref,
         naq_ref, nak_ref, nav_ref, q_ref, ckv_ref, krp_ref) = refs
    else:
        (x_ref, mod_ref, g_ref, win_ref, gq_ref, wqb_ref, gkv_ref,
         naq_ref, nak_ref, nav_ref, q_ref, ckv_ref, kr_ref, krp_ref) = refs
    x = x_ref[...]
    sa = mod_ref[0, 0:1, :]
    sca = mod_ref[0, 1:2, :]
    h = (_rms(x, g_ref[...]) * (1.0 + sca) + sa).astype(BF16)
    proj = _dot(h, win_ref[...])
    naq_ref[...] = (proj[:, 0:NA_WIDTH] * NA_QSCALE).astype(naq_ref.dtype)
    nak_ref[...] = proj[:, NA_WIDTH:2 * NA_WIDTH].astype(nak_ref.dtype)
    nav_ref[...] = proj[:, 2 * NA_WIDTH:3 * NA_WIDTH].astype(nav_ref.dtype)
    q_a = proj[:, 3 * NA_WIDTH:3 * NA_WIDTH + Q_LORA]
    kv_a = proj[:, 3 * NA_WIDTH + Q_LORA:KR_OFF]
    krx = proj[:, KR_OFF:IN_COLS_PAD]
    qan = _rms(q_a, gq_ref[...]).astype(BF16)
    qq = _dot(qan, wqb_ref[...])
    ckv_ref[...] = _rms(kv_a, gkv_ref[...]).astype(ckv_ref.dtype)
    if rope:
        width = MLA_HEADS * Q_PAD
        cosq = cosq_ref[...]
        sinq = sinq_ref[...]
        for hd in range(MLA_HEADS):
            a = qq[:, hd * Q_PAD:(hd + 1) * Q_PAD]
            b = qq[:, width + hd * Q_PAD:width + (hd + 1) * Q_PAD]
            q_ref[:, hd * Q_PAD:(hd + 1) * Q_PAD] = (a * cosq + b * sinq).astype(BF16)
        y = krx * csk_ref[...]
        y = y + pltpu.roll(y, QK_ROPE, 1)
        lane = lax.broadcasted_iota(jnp.int32, y.shape, 1)
        krp_ref[...] = jnp.where(lane < QK_ROPE, y, 0.0).astype(BF16)
    else:
        q_ref[...] = (qq * MLA_QSCALE).astype(BF16)
        kr_ref[...] = krx[:, 0:QK_ROPE]
        krp_ref[...] = krx.astype(BF16)


def _pre_attention(x, mod, g_attn, w_in, g_q_a, w_qb, g_kv_a, rope_tabs, *, rope, mod_row):
    n = x.shape[0]
    tm = ROW_TILE
    row = lambda i: (i, 0)
    in_specs = [pl.BlockSpec((tm, D_MODEL), row),
                pl.BlockSpec((1, 6, D_MODEL), lambda i: (mod_row(i), 0, 0)),
                _const_spec((1, D_MODEL)),
                _const_spec(w_in.shape),
                _const_spec((1, Q_LORA)),
                _const_spec(w_qb.shape),
                _const_spec((1, KV_LORA))]
    args = [x, mod, g_attn, w_in, g_q_a, w_qb, g_kv_a]
    qw = MLA_HEADS * Q_PAD
    if rope:
        tiles_per_seq = rope_tabs[0].shape[0] // tm
        pos = lambda i: (i % tiles_per_seq, 0)
        in_specs += [pl.BlockSpec((tm, Q_PAD), pos), pl.BlockSpec((tm, Q_PAD), pos),
                     pl.BlockSpec((tm, LANE), pos)]
        args += list(rope_tabs)
        out_shape = [jax.ShapeDtypeStruct((n, NA_WIDTH), BF16)] * 3 + [
            jax.ShapeDtypeStruct((n, qw), BF16),
            jax.ShapeDtypeStruct((n, KV_LORA), BF16),
            jax.ShapeDtypeStruct((n, LANE), BF16)]
        out_specs = [pl.BlockSpec((tm, NA_WIDTH), row)] * 3 + [
            pl.BlockSpec((tm, qw), row), pl.BlockSpec((tm, KV_LORA), row), pl.BlockSpec((tm, LANE), row)]
    else:
        out_shape = [jax.ShapeDtypeStruct((n, NA_WIDTH), BF16),
                     jax.ShapeDtypeStruct((n, NA_WIDTH), F32),
                     jax.ShapeDtypeStruct((n, NA_WIDTH), F32),
                     jax.ShapeDtypeStruct((n, qw), BF16),
                     jax.ShapeDtypeStruct((n, KV_LORA), F32),
                     jax.ShapeDtypeStruct((n, QK_ROPE), F32),
                     jax.ShapeDtypeStruct((n, LANE), BF16)]
        out_specs = [pl.BlockSpec((tm, NA_WIDTH), row)] * 3 + [
            pl.BlockSpec((tm, qw), row), pl.BlockSpec((tm, KV_LORA), row),
            pl.BlockSpec((tm, QK_ROPE), row), pl.BlockSpec((tm, LANE), row)]
    return pl.pallas_call(
        functools.partial(_pre_kernel, rope=rope),
        grid=(n // tm,),
        in_specs=in_specs,
        out_specs=out_specs,
        out_shape=out_shape,
        compiler_params=_cparams(("arbitrary",)),
        name="pre_attention_rope" if rope else "pre_attention",
    )(*args)


def _softmax_pv(s, v):
    m = jnp.max(s, axis=-1, keepdims=True)
    p = jnp.exp2(s - m)
    l = jnp.sum(p, axis=-1, keepdims=True)
    return _dot(p.astype(BF16), v) / l


def _prompt_attn_kernel(naq_ref, nak_ref, nav_ref, q_ref, ckv_ref, krp_ref, wkvb_ref, ona_ref, omla_ref):
    kv = _dot(ckv_ref[...].astype(BF16), wkvb_ref[...])
    krp = krp_ref[...]
    for hd in range(NA_HEADS):
        sl = slice(hd * NA_HEAD_DIM, (hd + 1) * NA_HEAD_DIM)
        s = _dot_nt(naq_ref[:, sl], nak_ref[:, sl].astype(BF16))
        ona_ref[:, sl] = _softmax_pv(s, nav_ref[:, sl].astype(BF16)).astype(BF16)
    kvw = QK_NOPE + V_DIM
    for hd in range(MLA_HEADS):
        kf = jnp.concatenate([kv[:, hd * kvw:hd * kvw + QK_NOPE].astype(BF16), krp], axis=-1)
        s = _dot_nt(q_ref[:, hd * Q_PAD:(hd + 1) * Q_PAD], kf)
        v = kv[:, hd * kvw + QK_NOPE:(hd + 1) * kvw].astype(BF16)
        omla_ref[:, hd * V_DIM:(hd + 1) * V_DIM] = _softmax_pv(s, v).astype(BF16)


def _prompt_attention(naq, nak, nav, q, ckv, krp, w_kvb, seq):
    n = naq.shape[0]
    row = lambda b: (b, 0)
    return pl.pallas_call(
        _prompt_attn_kernel,
        grid=(n // seq,),
        in_specs=[pl.BlockSpec((seq, NA_WIDTH), row)] * 3 + [
            pl.BlockSpec((seq, MLA_HEADS * Q_PAD), row),
            pl.BlockSpec((seq, KV_LORA), row),
            pl.BlockSpec((seq, LANE), row),
            _const_spec(w_kvb.shape)],
        out_specs=[pl.BlockSpec((seq, NA_WIDTH), row), pl.BlockSpec((seq, MLA_WIDTH), row)],
        out_shape=[jax.ShapeDtypeStruct((n, NA_WIDTH), BF16), jax.ShapeDtypeStruct((n, MLA_WIDTH), BF16)],
        compiler_params=_cparams(("arbitrary",)),
        name="prompt_attention",
    )(naq, nak, nav, q, ckv, krp, w_kvb)


def _na_kernel(q_ref, k_ref, v_ref, kc_ref, vc_ref, bias_ref, o_ref, *, rows):
    rt = pl.program_id(2)
    ws = jnp.clip(rt * NA_Q_ROWS - NA_KR // 2, 0, rows - NA_WIN_ROWS)
    start = pl.multiple_of(ws * GRID_W, GRID_W)
    nwin = NA_WIN_ROWS * GRID_W
    q = q_ref[...]
    s_loc = _dot_nt(q, k_ref[pl.ds(start, nwin), :]) + bias_ref[0, 0]
    s_ctx = _dot_nt(q, kc_ref[0].astype(BF16))
    m = jnp.maximum(jnp.max(s_loc, axis=-1, keepdims=True), jnp.max(s_ctx, axis=-1, keepdims=True))
    p_loc = jnp.exp2(s_loc - m)
    p_ctx = jnp.exp2(s_ctx - m)
    l = jnp.sum(p_loc, axis=-1, keepdims=True) + jnp.sum(p_ctx, axis=-1, keepdims=True)
    o = _dot(p_loc.astype(BF16), v_ref[pl.ds(start, nwin), :]) + _dot(p_ctx.astype(BF16), vc_ref[0].astype(BF16))
    o_ref[...] = (o / l).astype(BF16)


def _na_bias_table(rpb, rows):
    nh = rpb.shape[0]
    a = np.arange(NA_Q_ROWS)
    b = np.arange(NA_WIN_ROWS)
    col = np.arange(GRID_W)
    cs = np.clip(col - NA_KC // 2, 0, GRID_W - NA_KC)
    valid_col = (col[None, :] >= cs[:, None]) & (col[None, :] < cs[:, None] + NA_KC)
    padw = GRID_W - NA_KC
    rp = jnp.pad(rpb, ((0, 0), (0, 0), (padw, padw)))
    toep = jnp.stack([rp[:, :, GRID_W - 1 - qc:2 * GRID_W - 1 - qc] for qc in range(GRID_W)], axis=2)
    toep = jnp.where(valid_col[None, None], toep, NEG_BIG)
    masked = jnp.full((nh, GRID_W, GRID_W), NEG_BIG, F32)
    tabs = []
    for r0 in (0, NA_Q_ROWS, rows - NA_Q_ROWS):
        ws = int(np.clip(r0 - NA_KR // 2, 0, rows - NA_WIN_ROWS))
        r = r0 + a
        rs = np.clip(r - NA_KR // 2, 0, rows - NA_KR)
        kr = ws + b
        valid_row = (kr[None, :] >= rs[:, None]) & (kr[None, :] < rs[:, None] + NA_KR)
        dr = kr[None, :] - r[:, None] + NA_KR - 1
        tile_rows = []
        for ai in range(NA_Q_ROWS):
            blocks = [toep[:, int(dr[ai, bi])] if valid_row[ai, bi] else masked for bi in range(NA_WIN_ROWS)]
            tile_rows.append(jnp.concatenate(blocks, axis=-1))
        tabs.append(jnp.concatenate(tile_rows, axis=1))
    return jnp.stack(tabs)


def _sample_na(naq, nak, nav, kc, vc, bias, batch, seq):
    rows = seq // GRID_W
    tq = NA_Q_ROWS * GRID_W
    nt = seq // tq
    last = nt - 1

    def pat(rt):
        return jnp.where(rt == 0, 0, jnp.where(rt == last, 2, 1))

    return pl.pallas_call(
        functools.partial(_na_kernel, rows=rows),
        grid=(batch, NA_HEADS, nt),
        in_specs=[pl.BlockSpec((tq, NA_HEAD_DIM), lambda b, h, r: (b * nt + r, h)),
                  pl.BlockSpec((seq, NA_HEAD_DIM), lambda b, h, r: (b, h)),
                  pl.BlockSpec((seq, NA_HEAD_DIM), lambda b, h, r: (b, h)),
                  pl.BlockSpec((1, kc.shape[1], NA_HEAD_DIM), lambda b, h, r: (b, 0, h)),
                  pl.BlockSpec((1, vc.shape[1], NA_HEAD_DIM), lambda b, h, r: (b, 0, h)),
                  pl.BlockSpec((1, 1, tq, NA_WIN_ROWS * GRID_W), lambda b, h, r: (pat(r), h, 0, 0))],
        out_specs=pl.BlockSpec((tq, NA_HEAD_DIM), lambda b, h, r: (b * nt + r, h)),
        out_shape=jax.ShapeDtypeStruct((batch * seq, NA_WIDTH), BF16),
        compiler_params=_cparams(("arbitrary", "arbitrary", "arbitrary")),
        name="sample_neighbourhood_attention",
    )(naq, nak, nav, kc, vc, bias)


def _kv_expand_kernel(ckv_ref, krp_ref, w_ref, kf_ref, v_ref):
    kv = _dot(ckv_ref[0], w_ref[...])
    krp = krp_ref[0]
    kvw = QK_NOPE + V_DIM
    for hd in range(MLA_HEADS):
        kf_ref[0, hd, :, 0:QK_NOPE] = kv[:, hd * kvw:hd * kvw + QK_NOPE].astype(BF16)
        kf_ref[0, hd, :, QK_NOPE:Q_PAD] = krp
        v_ref[0, hd] = kv[:, hd * kvw + QK_NOPE:(hd + 1) * kvw].astype(BF16)


def _kv_expand(ckv, krp, w_kvb):
    batch, nkeys, _ = ckv.shape
    tm = 512
    return pl.pallas_call(
        _kv_expand_kernel,
        grid=(batch, nkeys // tm),
        in_specs=[pl.BlockSpec((1, tm, KV_LORA), lambda b, t: (b, t, 0)),
                  pl.BlockSpec((1, tm, LANE), lambda b, t: (b, t, 0)),
                  _const_spec(w_kvb.shape)],
        out_specs=[pl.BlockSpec((1, MLA_HEADS, tm, Q_PAD), lambda b, t: (b, 0, t, 0)),
                   pl.BlockSpec((1, MLA_HEADS, tm, V_DIM), lambda b, t: (b, 0, t, 0))],
        out_shape=[jax.ShapeDtypeStruct((batch, MLA_HEADS, nkeys, Q_PAD), BF16),
                   jax.ShapeDtypeStruct((batch, MLA_HEADS, nkeys, V_DIM), BF16)],
        compiler_params=_cparams(("arbitrary", "arbitrary")),
        name="latent_kv_expand",
    )(ckv, krp, w_kvb)


def _mla_kernel(q_ref, kf_ref, v_ref, o_ref, *, nkeys):
    q = q_ref[...]
    tq = q.shape[0]
    m = jnp.full((tq, 1), NEG_BIG, F32)
    l = jnp.zeros((tq, 1), F32)
    acc = jnp.zeros((tq, V_DIM), F32)
    for c in range(nkeys // MLA_TK):
        ks = slice(c * MLA_TK, (c + 1) * MLA_TK)
        s = _dot_nt(q, kf_ref[0, 0, ks, :])
        m_new = jnp.maximum(m, jnp.max(s, axis=-1, keepdims=True))
        alpha = jnp.exp2(m - m_new)
        p = jnp.exp2(s - m_new)
        l = alpha * l + jnp.sum(p, axis=-1, keepdims=True)
        acc = alpha * acc + _dot(p.astype(BF16), v_ref[0, 0, ks, :])
        m = m_new
    o_ref[...] = (acc / l).astype(BF16)


def _sample_mla(q, kf, v, batch, seq):
    nkeys = kf.shape[2]
    nt = seq // MLA_TQ
    return pl.pallas_call(
        functools.partial(_mla_kernel, nkeys=nkeys),
        grid=(batch, MLA_HEADS, nt),
        in_specs=[pl.BlockSpec((MLA_TQ, Q_PAD), lambda b, h, t: (b * nt + t, h)),
                  pl.BlockSpec((1, 1, nkeys, Q_PAD), lambda b, h, t: (b, h, 0, 0)),
                  pl.BlockSpec((1, 1, nkeys, V_DIM), lambda b, h, t: (b, h, 0, 0))],
        out_specs=pl.BlockSpec((MLA_TQ, V_DIM), lambda b, h, t: (b * nt + t, h)),
        out_shape=jax.ShapeDtypeStruct((batch * seq, MLA_WIDTH), BF16),
        compiler_params=_cparams(("arbitrary", "arbitrary", "arbitrary")),
        name="sample_latent_attention",
    )(q, kf, v)


def _post_kernel(xp_ref, xs_ref, onp_ref, omp_ref, ons_ref, oms_ref, mod_ref, wout_ref, gffn_ref, wrh_ref, wrl_ref,
                 br_ref, x1_ref, h2p_ref, idx_ref, gate_ref, *, n_prompt_tiles):
    is_prompt = pl.program_id(0) < n_prompt_tiles
    ga = mod_ref[0, 2:3, :]
    sf = mod_ref[0, 3:4, :]
    scf = mod_ref[0, 4:5, :]
    tm = ROW_TILE
    for r0 in range(0, xp_ref.shape[0], tm):
        rs = slice(r0, r0 + tm)
        x = jnp.where(is_prompt, xp_ref[rs, :], xs_ref[rs, :])
        ona = jnp.where(is_prompt, onp_ref[rs, :], ons_ref[rs, :])
        omla = jnp.where(is_prompt, omp_ref[rs, :], oms_ref[rs, :])
        o = _dot(ona, wout_ref[0:NA_WIDTH, :]) + _dot(omla, wout_ref[NA_WIDTH:NA_WIDTH + MLA_WIDTH, :])
        x1 = x + ga * o
        x1_ref[rs, :] = x1
        h2 = _rms(x1, gffn_ref[...]) * (1.0 + scf) + sf
        h_hi = h2.astype(BF16)
        h_lo = (h2 - h_hi.astype(F32)).astype(BF16)
        logits = _dot(h_hi, wrh_ref[...]) + _dot(h_lo, wrh_ref[...]) + _dot(h_hi, wrl_ref[...]) + br_ref[...]
        lane_e = lax.broadcasted_iota(jnp.int32, logits.shape, 1).astype(F32)
        lane_o = lax.broadcasted_iota(jnp.int32, (tm, LANE), 1)
        idx_out = jnp.zeros((tm, LANE), F32)
        gate_out = jnp.zeros((tm, LANE), F32)
        top0 = None
        denom = jnp.zeros((tm, 1), F32)
        cur = logits
        for k in range(TOP_K):
            mx = jnp.max(cur, axis=-1, keepdims=True)
            ix = jnp.min(jnp.where(cur == mx, lane_e, float(N_EXPERTS)), axis=-1, keepdims=True)
            cur = jnp.where(lane_e == ix, -jnp.inf, cur)
            if k == 0:
                top0 = mx
            e = jnp.exp(mx - top0)
            denom = denom + e
            idx_out = jnp.where(lane_o == k, ix, idx_out)
            gate_out = jnp.where(lane_o == k, e, gate_out)
        idx_ref[rs, :] = idx_out.astype(jnp.int32)
        gate_ref[rs, :] = gate_out / denom
        bits = pltpu.bitcast(h_hi.astype(F32), U32)
        for s in range(PACK_ROWS):
            lo = bits[:, s * LANE:(s + 1) * LANE] >> 16
            hi = bits[:, (s + PACK_ROWS) * LANE:(s + PACK_ROWS + 1) * LANE] & jnp.uint32(0xFFFF0000)
            h2p_ref[pl.ds(r0 * PACK_ROWS + s, tm, stride=PACK_ROWS), :] = hi | lo


def _post_attention(xp, xs, onp, omp, ons, oms, mod, w_out, g_ffn, w_router, b_router, *, mod_row):
    tm = POST_TILE
    wr_hi = w_router.astype(BF16)
    wr_lo = (w_router - wr_hi.astype(F32)).astype(BF16)
    npt = xp.shape[0] // tm
    nst = xs.shape[0] // tm
    n = xp.shape[0] + xs.shape[0]
    pidx = lambda i: (jnp.minimum(i, npt - 1), 0)
    sidx = lambda i: (jnp.maximum(i - npt, 0), 0)
    row = lambda i: (i, 0)
    return pl.pallas_call(
        functools.partial(_post_kernel, n_prompt_tiles=npt),
        grid=(npt + nst,),
        in_specs=[pl.BlockSpec((tm, D_MODEL), pidx), pl.BlockSpec((tm, D_MODEL), sidx),
                  pl.BlockSpec((tm, NA_WIDTH), pidx), pl.BlockSpec((tm, MLA_WIDTH), pidx),
                  pl.BlockSpec((tm, NA_WIDTH), sidx), pl.BlockSpec((tm, MLA_WIDTH), sidx),
                  pl.BlockSpec((1, 6, D_MODEL), lambda i: (mod_row(i), 0, 0)),
                  _const_spec(w_out.shape), _const_spec((1, D_MODEL)),
                  _const_spec(w_router.shape), _const_spec(w_router.shape), _const_spec((1, N_EXPERTS))],
        out_specs=[pl.BlockSpec((tm, D_MODEL), row),
                   pl.BlockSpec((tm * PACK_ROWS, LANE), row),
                   pl.BlockSpec((tm, LANE), row),
                   pl.BlockSpec((tm, LANE), row)],
        out_shape=[jax.ShapeDtypeStruct((n, D_MODEL), F32),
                   jax.ShapeDtypeStruct((n * PACK_ROWS, LANE), U32),
                   jax.ShapeDtypeStruct((n, LANE), jnp.int32),
                   jax.ShapeDtypeStruct((n, LANE), F32)],
        compiler_params=_cparams(("arbitrary",)),
        name="post_attention_router",
    )(xp, xs, onp, omp, ons, oms, mod, w_out, g_ffn, wr_hi, wr_lo, b_router)


def _deinterleave_kernel(w_ref, p_ref, o_ref):
    p = p_ref[...]
    grp = p.shape[0]
    per_tile = o_ref.shape[3] // grp
    for c in range(w_ref.shape[2] // grp):
        w = w_ref[0, :, c * grp:(c + 1) * grp].astype(BF16)
        col = (c % per_tile) * grp
        o_ref[0, c // per_tile, :, col:col + grp] = _dot(w, p).astype(BF16)


def _deinterleave_gate_up(w_gate_up):
    ne, d, n2 = w_gate_up.shape
    nj = n2 // (2 * MOE_TF)
    grp = 2 * LANE
    dst = np.arange(grp)
    src = np.where(dst < LANE, 2 * dst, 2 * (dst - LANE) + 1)
    perm = np.zeros((grp, grp), np.float32)
    perm[src, dst] = 1.0
    tk = 512
    return pl.pallas_call(
        _deinterleave_kernel,
        grid=(ne, d // tk),
        in_specs=[pl.BlockSpec((1, tk, n2), lambda e, k: (e, k, 0)), _const_spec((grp, grp))],
        out_specs=pl.BlockSpec((1, nj, tk, 2 * MOE_TF), lambda e, k: (e, 0, k, 0)),
        out_shape=jax.ShapeDtypeStruct((ne, nj, d, 2 * MOE_TF), BF16),
        compiler_params=_cparams(("arbitrary", "arbitrary")),
        name="deinterleave_gate_up",
    )(w_gate_up, jnp.asarray(perm, BF16))


def _moe_kernel(ce_ref, nv_ref, cs_ref, order_ref, h_ref, wgu_ref, bgu_ref, wd_ref, bd_ref, y_ref,
                xg_ref, xbf_ref, acc_ref, wdb_ref, ost_ref, sem_g, sem_s):
    c = pl.program_id(0)
    j = pl.program_id(1)
    last_c = pl.num_programs(0) - 1
    last_j = pl.num_programs(1) - 1
    nk = order_ref.shape[0]
    nsub = MOE_CHUNK // MOE_SUB
    xg_rows = MOE_CHUNK * PACK_ROWS

    nv = nv_ref[c]
    c_next = jnp.minimum(c + 1, last_c)
    nv_next = jnp.where(c < last_c, nv_ref[c_next], 0)
    cs_next = cs_ref[c_next]
    c_prev = jnp.maximum(c - 1, 0)
    nv_prev = jnp.where(c > 0, nv_ref[c_prev], 0)
    cs_prev = cs_ref[c_prev]
    nv_prev2 = jnp.where(c > 1, nv_ref[jnp.maximum(c - 2, 0)], 0)
    ost_rows = MOE_CHUNK * OUT_ROWS
    slot = c % 2
    slot_next = (c + 1) % 2

    def gather_row(cs, dst_slot, row):
        tok = jnp.right_shift(order_ref[jnp.minimum(cs + row, nk - 1)], TOP_K_SHIFT)
        src = pl.multiple_of(tok * PACK_ROWS, PACK_ROWS)
        dst = pl.multiple_of(dst_slot * xg_rows + row * PACK_ROWS, PACK_ROWS)
        pltpu.make_async_copy(h_ref.at[pl.ds(src, PACK_ROWS)], xg_ref.at[pl.ds(dst, PACK_ROWS)],
                              sem_g.at[dst_slot]).start()

    def scatter_row(row):
        flat = order_ref[jnp.minimum(cs_prev + row, nk - 1)]
        dest = jnp.where(row < nv_prev, flat, nk + slot_next * MOE_CHUNK + row)
        src = pl.multiple_of(slot_next * ost_rows + row * OUT_ROWS, OUT_ROWS)
        dst = pl.multiple_of(dest * OUT_ROWS, OUT_ROWS)
        pltpu.make_async_copy(ost_ref.at[pl.ds(src, OUT_ROWS)], y_ref.at[pl.ds(dst, OUT_ROWS)],
                              sem_s.at[slot_next]).start()

    def rolled(fn, row0, nrows):
        def body(r0, carry):
            for u in range(DMA_UNROLL):
                fn(row0 + r0 * DMA_UNROLL + u)
            return carry

        lax.fori_loop(0, nrows // DMA_UNROLL, body, 0)

    @pl.when(jnp.logical_and(jnp.logical_and(c == 0, j == 0), nv > 0))
    def _():
        rolled(lambda row: gather_row(cs_ref[0], 0, row), 0, MOE_CHUNK)

    def unpack(sb):
        for s in range(PACK_ROWS):
            w = xg_ref[pl.ds(slot * xg_rows + sb * MOE_SUB * PACK_ROWS + s, MOE_SUB, stride=PACK_ROWS), :]
            lo = pltpu.bitcast(w << 16, F32)
            hi = pltpu.bitcast(w & jnp.uint32(0xFFFF0000), F32)
            xbf_ref[sb * MOE_SUB:(sb + 1) * MOE_SUB, s * LANE:(s + 1) * LANE] = lo.astype(BF16)
            xbf_ref[sb * MOE_SUB:(sb + 1) * MOE_SUB,
                    (s + PACK_ROWS) * LANE:(s + PACK_ROWS + 1) * LANE] = hi.astype(BF16)

    @pl.when(jnp.logical_and(j == 0, nv > 0))
    def _():
        base = pl.multiple_of(slot * xg_rows, xg_rows)
        pltpu.make_async_copy(h_ref.at[pl.ds(0, xg_rows)], xg_ref.at[pl.ds(base, xg_rows)], sem_g.at[slot]).wait()

        acc_ref[...] = jnp.broadcast_to(bd_ref[0], (MOE_CHUNK, D_MODEL))

        @pl.when(c == 0)
        def _():
            ost_ref[...] = jnp.zeros(ost_ref.shape, F32)

        @pl.when(nv == MOE_CHUNK)
        def _():
            for sb in range(nsub):
                unpack(sb)

        @pl.when(nv < MOE_CHUNK)
        def _():
            for sb in range(nsub):
                @pl.when(sb * MOE_SUB < nv)
                def _():
                    unpack(sb)

    def sub_block(sb):
        if sb == 0:
            wdb_ref[...] = wd_ref[0].astype(BF16)
        x = xbf_ref[sb * MOE_SUB:(sb + 1) * MOE_SUB, :]
        gu = _dot(x, wgu_ref[0, 0]) + bgu_ref[0]
        ng = MOE_TF // LANE
        g = jnp.concatenate([gu[:, 2 * b * LANE:(2 * b + 1) * LANE] for b in range(ng)], axis=-1)
        u = jnp.concatenate([gu[:, (2 * b + 1) * LANE:(2 * b + 2) * LANE] for b in range(ng)], axis=-1)
        g = jnp.minimum(g, SWIGLU_LIMIT)
        u = jnp.clip(u, -SWIGLU_LIMIT, SWIGLU_LIMIT)
        act = g * (1.0 / (1.0 + jnp.exp(-(g * SWIGLU_ALPHA)))) * (u + 1.0)
        acc_ref[sb * MOE_SUB:(sb + 1) * MOE_SUB, :] += _dot(act.astype(BF16), wdb_ref[...])

    dma_both = jnp.logical_and(nv_next > 0, nv_prev > 0)
    per_sb = MOE_SUB // nsub
    for sb in range(nsub):
        row0 = j * MOE_SUB + sb * per_sb
        compute = sb * MOE_SUB < nv
        fused = jnp.logical_and(compute, dma_both)

        @pl.when(fused)
        def _():
            for r in range(per_sb):
                gather_row(cs_next, slot_next, row0 + r)
                scatter_row(row0 + r)
            sub_block(sb)

        @pl.when(jnp.logical_not(fused))
        def _():
            @pl.when(nv_next > 0)
            def _():
                for r in range(per_sb):
                    gather_row(cs_next, slot_next, row0 + r)

            @pl.when(nv_prev > 0)
            def _():
                for r in range(per_sb):
                    scatter_row(row0 + r)

            @pl.when(compute)
            def _():
                sub_block(sb)

    def wait_scatter(buf):
        base = pl.multiple_of(buf * ost_rows, ost_rows)
        pltpu.make_async_copy(ost_ref.at[pl.ds(base, ost_rows)], y_ref.at[pl.ds(0, ost_rows)], sem_s.at[buf]).wait()

    @pl.when(j == last_j)
    def _():
        @pl.when(nv_prev2 > 0)
        def _():
            wait_scatter(slot)

        for sb in range(nsub):
            @pl.when(sb * MOE_SUB < nv)
            def _():
                for s in range(OUT_ROWS):
                    dst = pl.ds(slot * ost_rows + sb * MOE_SUB * OUT_ROWS + s, MOE_SUB, stride=OUT_ROWS)
                    ost_ref[dst, :] = acc_ref[sb * MOE_SUB:(sb + 1) * MOE_SUB, s * LANE:(s + 1) * LANE]

        @pl.when(jnp.logical_and(c == last_c, nv_prev > 0))
        def _():
            wait_scatter(slot_next)


def _moe(chunk_expert, chunk_rows, chunk_start, order, h2p, wgu, bgu, w_down, b_down):
    nj = D_FF // MOE_TF
    ngrid = chunk_expert.shape[0]
    nk = order.shape[0]

    def jj(c, j, nv):
        return jnp.where(nv[c] > 0, j, nj - 1)

    return pl.pallas_call(
        _moe_kernel,
        grid_spec=pltpu.PrefetchScalarGridSpec(
            num_scalar_prefetch=4,
            grid=(ngrid, nj),
            in_specs=[
                pl.BlockSpec(memory_space=pl.ANY),
                pl.BlockSpec((1, 1, D_MODEL, 2 * MOE_TF), lambda c, j, ce, nv, cs, od: (ce[c], jj(c, j, nv), 0, 0)),
                pl.BlockSpec((1, 1, 2 * MOE_TF), lambda c, j, ce, nv, cs, od: (ce[c], 0, jj(c, j, nv))),
                pl.BlockSpec((1, MOE_TF, D_MODEL), lambda c, j, ce, nv, cs, od: (ce[c], jj(c, j, nv), 0)),
                pl.BlockSpec((1, 1, D_MODEL), lambda c, j, ce, nv, cs, od: (ce[c], 0, 0))],
            out_specs=pl.BlockSpec(memory_space=pl.ANY),
            scratch_shapes=[pltpu.VMEM((2 * MOE_CHUNK * PACK_ROWS, LANE), U32),
                            pltpu.VMEM((MOE_CHUNK, D_MODEL), BF16),
                            pltpu.VMEM((MOE_CHUNK, D_MODEL), F32),
                            pltpu.VMEM((MOE_TF, D_MODEL), BF16),
                            pltpu.VMEM((2 * MOE_CHUNK * OUT_ROWS, LANE), F32),
                            pltpu.SemaphoreType.DMA((2,)),
                            pltpu.SemaphoreType.DMA((2,))]),
        out_shape=jax.ShapeDtypeStruct(((nk + 2 * MOE_CHUNK) * OUT_ROWS, LANE), F32),
        compiler_params=_cparams(("arbitrary", "arbitrary")),
        name="moe_experts",
    )(chunk_expert, chunk_rows, chunk_start, order, h2p, wgu, bgu, w_down, b_down)


def _combine_kernel(gate_ref, x1_ref, mod_ref, gfin_ref, ye_ref, y_ref):
    gate = gate_ref[...]
    pieces = []
    for s in range(OUT_ROWS):
        acc = None
        for k in range(TOP_K):
            rows = ye_ref[pl.ds(k * OUT_ROWS + s, COMBINE_TOKENS, stride=TOP_K * OUT_ROWS), :]
            term = gate[:, k:k + 1] * rows
            acc = term if acc is None else acc + term
        pieces.append(acc)
    y = jnp.concatenate(pieces, axis=-1)
    gf = mod_ref[0, 5:6, :]
    y_ref[...] = _rms(x1_ref[...] + gf * y, gfin_ref[...])


def _combine(gates, x1, mod, g_final, y_experts, *, tile0, ntiles, mod_row):
    tt = COMBINE_TOKENS
    return pl.pallas_call(
        _combine_kernel,
        grid=(ntiles,),
        in_specs=[pl.BlockSpec((tt, LANE), lambda i: (tile0 + i, 0)),
                  pl.BlockSpec((tt, D_MODEL), lambda i: (tile0 + i, 0)),
                  pl.BlockSpec((1, 6, D_MODEL), lambda i: (mod_row(i), 0, 0)),
                  _const_spec((1, D_MODEL)),
                  pl.BlockSpec((tt * TOP_K * OUT_ROWS, LANE), lambda i: (tile0 + i, 0))],
        out_specs=pl.BlockSpec((tt, D_MODEL), lambda i: (i, 0)),
        out_shape=jax.ShapeDtypeStruct((ntiles * tt, D_MODEL), F32),
        compiler_params=_cparams(("arbitrary",)),
        name="moe_combine_final_norm",
    )(gates, x1, mod, g_final, y_experts)


def _routing(top_idx, nch):
    n = top_idx.shape[0]
    nk = n * TOP_K
    flat_e = top_idx.reshape(nk)
    order = jnp.argsort(flat_e, stable=True).astype(jnp.int32)
    onehot = flat_e[:, None] == jnp.arange(N_EXPERTS, dtype=jnp.int32)[None, :]
    counts = jnp.sum(onehot.astype(jnp.int32), axis=0)
    grp_start = jnp.cumsum(counts) - counts
    chunks_e = (counts + MOE_CHUNK - 1) // MOE_CHUNK
    chunk_end = jnp.cumsum(chunks_e)
    chunk_start = chunk_end - chunks_e

    total = chunk_end[-1]
    cidx = jnp.arange(nch + 1, dtype=jnp.int32)
    active = cidx < total
    ce = jnp.minimum(jnp.searchsorted(chunk_end, cidx, side='right'), N_EXPERTS - 1).astype(jnp.int32)
    local = (cidx - chunk_start[ce]) * MOE_CHUNK
    nv = jnp.where(active, jnp.clip(counts[ce] - local, 0, MOE_CHUNK), 0).astype(jnp.int32)
    last = jnp.maximum(total - 1, 0)
    chunk_expert = jnp.where(active, ce, ce[last]).astype(jnp.int32)
    chunk_first = jnp.where(active, jnp.clip(grp_start[ce] + local, 0, nk - 1), 0).astype(jnp.int32)
    return order, chunk_expert, nv, chunk_first


def _rope_tables(t):
    pos = jnp.arange(t)
    rows = (pos // GRID_W).astype(F32)
    cols = (pos % GRID_W).astype(F32)
    inv = ROPE_THETA ** (-(jnp.arange(ROPE_AXIS // 2, dtype=F32) * 2.0 / ROPE_AXIS))
    ar = rows[:, None] * inv
    ac = cols[:, None] * inv
    ang = jnp.concatenate([ar, ar, ac, ac], axis=-1)
    return jnp.cos(ang), jnp.sin(ang)


def _rot_cols(w):
    half = ROPE_AXIS // 2
    src = np.concatenate([np.arange(half, ROPE_AXIS), np.arange(0, half),
                          np.arange(ROPE_AXIS + half, 2 * ROPE_AXIS), np.arange(ROPE_AXIS, ROPE_AXIS + half)])
    sign = np.concatenate([-np.ones(half), np.ones(half), -np.ones(half), np.ones(half)]).astype(np.float32)
    return w[..., src] * sign


def kernel(x_prompt, x_sample, cache_na_k, cache_na_v, cache_mla_ckv, cache_mla_krope, c, c_ctx, g_attn, g_ffn, g_final, w_mod, b_mod, w_in, w_out, na_rpb, g_q_a, w_q_b, g_kv_a, w_kv_b, w_router, b_router, w_gate_up, b_gate_up, w_down, b_down):
    bp, sp, d = x_prompt.shape
    bd, td, _ = x_sample.shape
    assert d == D_MODEL and w_mod.shape[0] == 1, "one trunk layer of width D_MODEL"
    n_p = bp * sp
    n_s = bd * td
    xp = x_prompt.reshape(n_p, d)
    xs = x_sample.reshape(n_s, d)

    c8 = jnp.zeros((8, d), F32).at[0].set(c_ctx).at[1:1 + bd].set(c)
    mod = _modulation(c8, w_mod[0], b_mod[0].reshape(1, -1)).reshape(8, 6, d)

    w_in0 = w_in[0]
    w_kr = w_in0[:, KR_OFF:KR_OFF + QK_ROPE]
    w_in_p = jnp.concatenate([w_in0, jnp.zeros((d, LANE - QK_ROPE), F32)], axis=1).astype(BF16)
    w_in_s = jnp.concatenate([w_in0, _rot_cols(w_kr)], axis=1).astype(BF16)
    wq = w_q_b[0].reshape(Q_LORA, MLA_HEADS, QK_NOPE + QK_ROPE)
    zpad = jnp.zeros((Q_LORA, MLA_HEADS, Q_PAD - QK_NOPE - QK_ROPE), F32)
    wq_pad = jnp.concatenate([wq, zpad], axis=-1).reshape(Q_LORA, MLA_HEADS * Q_PAD)
    wq_rot = jnp.concatenate([jnp.zeros((Q_LORA, MLA_HEADS, QK_NOPE), F32), _rot_cols(wq[..., QK_NOPE:]), zpad],
                             axis=-1).reshape(Q_LORA, MLA_HEADS * Q_PAD)
    wqb_p = wq_pad.astype(BF16)
    wqb_s = jnp.concatenate([wq_pad, wq_rot], axis=1).astype(BF16)
    w_kvb = w_kv_b[0].astype(BF16)
    cos, sin = _rope_tables(td)
    cosq = jnp.concatenate([jnp.ones((td, QK_NOPE), F32), cos, jnp.ones((td, Q_PAD - QK_NOPE - QK_ROPE), F32)], axis=1)
    sinq = jnp.concatenate([jnp.zeros((td, QK_NOPE), F32), sin, jnp.zeros((td, Q_PAD - QK_NOPE - QK_ROPE), F32)], axis=1)
    cosq = cosq * MLA_QSCALE
    sinq = sinq * MLA_QSCALE
    csk = jnp.concatenate([cos, sin], axis=1)

    tiles_per_seq = td // ROW_TILE
    g_attn2 = g_attn[0].reshape(1, d)
    gq2 = g_q_a[0].reshape(1, Q_LORA)
    gkv2 = g_kv_a[0].reshape(1, KV_LORA)

    naq_p, nak_p, nav_p, q_p, ckv_p, kr_p, krp_p = _pre_attention(
        xp, mod, g_attn2, w_in_p, gq2, wqb_p, gkv2, None, rope=False, mod_row=lambda i: 0)
    ona_p, omla_p = _prompt_attention(naq_p, nak_p, nav_p, q_p, ckv_p, krp_p, w_kvb, sp)

    naq_s, nak_s, nav_s, q_s, ckv_s, krp_s = _pre_attention(
        xs, mod, g_attn2, w_in_s, gq2, wqb_s, gkv2, (cosq, sinq, csk), rope=True,
        mod_row=lambda i: 1 + i // tiles_per_seq)
    past = cache_na_k.shape[2]
    kc = cache_na_k[:, 0].reshape(bd, past, NA_WIDTH)
    vc = cache_na_v[:, 0].reshape(bd, past, NA_WIDTH)
    bias = _na_bias_table(na_rpb[0] * LOG2E, td // GRID_W)
    ona_s = _sample_na(naq_s, nak_s, nav_s, kc, vc, bias, bd, td)
    ckv_all = jnp.concatenate([ckv_s.reshape(bd, td, KV_LORA), cache_mla_ckv[:, 0].astype(BF16)], axis=1)
    krp_c = jnp.concatenate([cache_mla_krope[:, 0], jnp.zeros((bd, past, LANE - QK_ROPE), F32)], axis=-1).astype(BF16)
    krp_all = jnp.concatenate([krp_s.reshape(bd, td, LANE), krp_c], axis=1)
    kf, vv = _kv_expand(ckv_all, krp_all, w_kvb)
    omla_s = _sample_mla(q_s, kf, vv, bd, td)

    npt = n_p // POST_TILE
    post_tiles_per_seq = td // POST_TILE
    x1, h2p, idx128, gate128 = _post_attention(
        xp, xs, ona_p, omla_p, ona_s, omla_s, mod, w_out[0].astype(BF16), g_ffn[0].reshape(1, d),
        w_router[0], b_router[0].reshape(1, N_EXPERTS),
        mod_row=lambda i: jnp.where(i < npt, 0, 1 + jnp.maximum(i - npt, 0) // post_tiles_per_seq))

    n = n_p + n_s
    nch = n * TOP_K // MOE_CHUNK + N_EXPERTS
    order, chunk_expert, chunk_rows, chunk_first = _routing(idx128[:, :TOP_K], nch)
    wgu = _deinterleave_gate_up(w_gate_up[0])
    ng = 2 * D_FF // (2 * LANE)
    bgu = b_gate_up[0].reshape(N_EXPERTS, ng, LANE, 2).transpose(0, 1, 3, 2).reshape(N_EXPERTS, 1, 2 * D_FF)
    y_experts = _moe(chunk_expert, chunk_rows, chunk_first, order, h2p, wgu, bgu, w_down[0],
                     b_down[0].reshape(N_EXPERTS, 1, d))

    tt = COMBINE_TOKENS
    gfin = g_final.reshape(1, d)
    ctiles_seq = td // tt
    y_p = _combine(gate128, x1, mod, gfin, y_experts, tile0=0, ntiles=n_p // tt, mod_row=lambda i: 0)
    y_s = _combine(gate128, x1, mod, gfin, y_experts, tile0=n_p // tt, ntiles=n_s // tt,
                   mod_row=lambda i: 1 + i // ctiles_seq)

    return (y_p.reshape(bp, sp, d), y_s.reshape(bd, td, d),
            nak_p.reshape(bp, 1, sp, NA_HEADS, NA_HEAD_DIM), nav_p.reshape(bp, 1, sp, NA_HEADS, NA_HEAD_DIM),
            ckv_p.reshape(bp, 1, sp, KV_LORA), kr_p.reshape(bp, 1, sp, QK_ROPE))
```

```python
import functools

import numpy as np
import jax
import jax.numpy as jnp
from jax import lax
from jax.experimental import pallas as pl
from jax.experimental.pallas import tpu as pltpu

F32 = jnp.float32
BF16 = jnp.bfloat16
U32 = jnp.uint32

D_MODEL = 2048
GRID_W = 64
NA_HEADS = 8
NA_HEAD_DIM = 128
NA_KR = 8
NA_KC = 16
MLA_HEADS = 8
Q_LORA = 512
KV_LORA = 256
QK_NOPE = 128
QK_ROPE = 64
V_DIM = 128
ROPE_AXIS = QK_ROPE // 2
ROPE_THETA = 10000.0
NA_WIDTH = NA_HEADS * NA_HEAD_DIM
MLA_WIDTH = MLA_HEADS * V_DIM
IN_COLS = 3 * NA_WIDTH + Q_LORA + KV_LORA + QK_ROPE
N_EXPERTS = 32
TOP_K = 4
D_FF = D_MODEL
SWIGLU_ALPHA = 1.702
SWIGLU_LIMIT = 7.0
EPS = 1e-6

LANE = 128
Q_PAD = 2 * LANE
KR_OFF = 3 * NA_WIDTH + Q_LORA + KV_LORA
IN_COLS_PAD = KR_OFF + LANE
VMEM_LIMIT = 56 * 1024 * 1024
NEG_BIG = -1e30

LOG2E = 1.4426950408889634
NA_QSCALE = NA_HEAD_DIM ** -0.5 * LOG2E
MLA_QSCALE = (QK_NOPE + QK_ROPE) ** -0.5 * LOG2E

ROW_TILE = 256
POST_TILE = 2 * ROW_TILE
NA_Q_ROWS = 4
NA_WIN_ROWS = 12
MLA_TQ = 512
MLA_TK = 1152
MOE_CHUNK = 1024
MOE_SUB = 256
MOE_TF = 512
COMBINE_TOKENS = 128
TOP_K_SHIFT = TOP_K.bit_length() - 1
DMA_UNROLL = 8
PACK_ROWS = D_MODEL // (2 * LANE)
OUT_ROWS = D_MODEL // LANE


def _cparams(sem):
    return pltpu.CompilerParams(dimension_semantics=sem, vmem_limit_bytes=VMEM_LIMIT)


def _const_spec(shape):
    nd = len(shape)
    return pl.BlockSpec(shape, lambda *a: (0,) * nd, pipeline_mode=pl.Buffered(1))


def _rms(x, g):
    return x * lax.rsqrt(jnp.mean(x * x, axis=-1, keepdims=True) + EPS) * g


def _dot(a, b):
    return jnp.dot(a, b, preferred_element_type=F32)


def _dot_nt(a, b):
    return lax.dot_general(a, b, (((1,), (1,)), ((), ())), preferred_element_type=F32)


def _mod_kernel(c_ref, w_ref, b_ref, o_ref):
    c = c_ref[...]
    s = c / (1.0 + jnp.exp(-c))
    o_ref[...] = _dot(s.astype(BF16), w_ref[...].astype(BF16)) + b_ref[...]


def _modulation(c8, w_mod, b_mod):
    n = w_mod.shape[1]
    tn = 1024
    return pl.pallas_call(
        _mod_kernel,
        grid=(n // tn,),
        in_specs=[pl.BlockSpec((8, D_MODEL), lambda j: (0, 0)),
                  pl.BlockSpec((D_MODEL, tn), lambda j: (0, j)),
                  pl.BlockSpec((1, tn), lambda j: (0, j))],
        out_specs=pl.BlockSpec((8, tn), lambda j: (0, j)),
        out_shape=jax.ShapeDtypeStruct((8, n), F32),
        compiler_params=_cparams(("arbitrary",)),
        name="modulation",
    )(c8, w_mod, b_mod)


def _pre_kernel(*refs, rope):
    if rope:
        (x_ref, mod_ref, g_ref, win_ref, gq_ref, wqb_ref, gkv_ref, cosq_ref, sinq_ref, csk_ref,
         naq_ref, nak_ref, nav_ref, q_ref, ckv_ref, krp_ref) = refs
    else:
        (x_ref, mod_ref, g_ref, win_ref, gq_ref, wqb_ref, gkv_ref,
         naq_ref, nak_ref, nav_ref, q_ref, ckv_ref, kr_ref, krp_ref) = refs
    x = x_ref[...]
    sa = mod_ref[0, 0:1, :]
    sca = mod_ref[0, 1:2, :]
    h = (_rms(x, g_ref[...]) * (1.0 + sca) + sa).astype(BF16)
    proj = _dot(h, win_ref[...])
    naq_ref[...] = (proj[:, 0:NA_WIDTH] * NA_QSCALE).astype(naq_ref.dtype)
    nak_ref[...] = proj[:, NA_WIDTH:2 * NA_WIDTH].astype(nak_ref.dtype)
    nav_ref[...] = proj[:, 2 * NA_WIDTH:3 * NA_WIDTH].astype(nav_ref.dtype)
    q_a = proj[:, 3 * NA_WIDTH:3 * NA_WIDTH + Q_LORA]
    kv_a = proj[:, 3 * NA_WIDTH + Q_LORA:KR_OFF]
    krx = proj[:, KR_OFF:IN_COLS_PAD]
    qan = _rms(q_a, gq_ref[...]).astype(BF16)
    qq = _dot(qan, wqb_ref[...])
    ckv_ref[...] = _rms(kv_a, gkv_ref[...]).astype(ckv_ref.dtype)
    if rope:
        width = MLA_HEADS * Q_PAD
        cosq = cosq_ref[...]
        sinq = sinq_ref[...]
        for hd in range(MLA_HEADS):
            a = qq[:, hd * Q_PAD:(hd + 1) * Q_PAD]
            b = qq[:, width + hd * Q_PAD:width + (hd + 1) * Q_PAD]
            q_ref[:, hd * Q_PAD:(hd + 1) * Q_PAD] = (a * cosq + b * sinq).astype(BF16)
        y = krx * csk_ref[...]
        y = y + pltpu.roll(y, QK_ROPE, 1)
        lane = lax.broadcasted_iota(jnp.int32, y.shape, 1)
        krp_ref[...] = jnp.where(lane < QK_ROPE, y, 0.0).astype(BF16)
    else:
        q_ref[...] = (qq * MLA_QSCALE).astype(BF16)
        kr_ref[...] = krx[:, 0:QK_ROPE]
        krp_ref[...] = krx.astype(BF16)


def _pre_attention(x, mod, g_attn, w_in, g_q_a, w_qb, g_kv_a, rope_tabs, *, rope, mod_row):
    n = x.shape[0]
    tm = ROW_TILE
    row = lambda i: (i, 0)
    in_specs = [pl.BlockSpec((tm, D_MODEL), row),
                pl.BlockSpec((1, 6, D_MODEL), lambda i: (mod_row(i), 0, 0)),
                _const_spec((1, D_MODEL)),
                _const_spec(w_in.shape),
                _const_spec((1, Q_LORA)),
                _const_spec(w_qb.shape),
                _const_spec((1, KV_LORA))]
    args = [x, mod, g_attn, w_in, g_q_a, w_qb, g_kv_a]
    qw = MLA_HEADS * Q_PAD
    if rope:
        tiles_per_seq = rope_tabs[0].shape[0] // tm
        pos = lambda i: (i % tiles_per_seq, 0)
        in_specs += [pl.BlockSpec((tm, Q_PAD), pos), pl.BlockSpec((tm, Q_PAD), pos),
                     pl.BlockSpec((tm, LANE), pos)]
        args += list(rope_tabs)
        out_shape = [jax.ShapeDtypeStruct((n, NA_WIDTH), BF16)] * 3 + [
            jax.ShapeDtypeStruct((n, qw), BF16),
            jax.ShapeDtypeStruct((n, KV_LORA), BF16),
            jax.ShapeDtypeStruct((n, LANE), BF16)]
        out_specs = [pl.BlockSpec((tm, NA_WIDTH), row)] * 3 + [
            pl.BlockSpec((tm, qw), row), pl.BlockSpec((tm, KV_LORA), row), pl.BlockSpec((tm, LANE), row)]
    else:
        out_shape = [jax.ShapeDtypeStruct((n, NA_WIDTH), BF16),
                     jax.ShapeDtypeStruct((n, NA_WIDTH), F32),
                     jax.ShapeDtypeStruct((n, NA_WIDTH), F32),
                     jax.ShapeDtypeStruct((n, qw), BF16),
                     jax.ShapeDtypeStruct((n, KV_LORA), F32),
                     jax.ShapeDtypeStruct((n, QK_ROPE), F32),
                     jax.ShapeDtypeStruct((n, LANE), BF16)]
        out_specs = [pl.BlockSpec((tm, NA_WIDTH), row)] * 3 + [
            pl.BlockSpec((tm, qw), row), pl.BlockSpec((tm, KV_LORA), row),
            pl.BlockSpec((tm, QK_ROPE), row), pl.BlockSpec((tm, LANE), row)]
    return pl.pallas_call(
        functools.partial(_pre_kernel, rope=rope),
        grid=(n // tm,),
        in_specs=in_specs,
        out_specs=out_specs,
        out_shape=out_shape,
        compiler_params=_cparams(("arbitrary",)),
        name="pre_attention_rope" if rope else "pre_attention",
    )(*args)


def _softmax_pv(s, v):
    m = jnp.max(s, axis=-1, keepdims=True)
    p = jnp.exp2(s - m)
    l = jnp.sum(p, axis=-1, keepdims=True)
    return _dot(p.astype(BF16), v) / l


def _prompt_attn_kernel(naq_ref, nak_ref, nav_ref, q_ref, ckv_ref, krp_ref, wkvb_ref, ona_ref, omla_ref):
    kv = _dot(ckv_ref[...].astype(BF16), wkvb_ref[...])
    krp = krp_ref[...]
    for hd in range(NA_HEADS):
        sl = slice(hd * NA_HEAD_DIM, (hd + 1) * NA_HEAD_DIM)
        s = _dot_nt(naq_ref[:, sl], nak_ref[:, sl].astype(BF16))
        ona_ref[:, sl] = _softmax_pv(s, nav_ref[:, sl].astype(BF16)).astype(BF16)
    kvw = QK_NOPE + V_DIM
    for hd in range(MLA_HEADS):
        kf = jnp.concatenate([kv[:, hd * kvw:hd * kvw + QK_NOPE].astype(BF16), krp], axis=-1)
        s = _dot_nt(q_ref[:, hd * Q_PAD:(hd + 1) * Q_PAD], kf)
        v = kv[:, hd * kvw + QK_NOPE:(hd + 1) * kvw].astype(BF16)
        omla_ref[:, hd * V_DIM:(hd + 1) * V_DIM] = _softmax_pv(s, v).astype(BF16)


def _prompt_attention(naq, nak, nav, q, ckv, krp, w_kvb, seq):
    n = naq.shape[0]
    row = lambda b: (b, 0)
    return pl.pallas_call(
        _prompt_attn_kernel,
        grid=(n // seq,),
        in_specs=[pl.BlockSpec((seq, NA_WIDTH), row)] * 3 + [
            pl.BlockSpec((seq, MLA_HEADS * Q_PAD), row),
            pl.BlockSpec((seq, KV_LORA), row),
            pl.BlockSpec((seq, LANE), row),
            _const_spec(w_kvb.shape)],
        out_specs=[pl.BlockSpec((seq, NA_WIDTH), row), pl.BlockSpec((seq, MLA_WIDTH), row)],
        out_shape=[jax.ShapeDtypeStruct((n, NA_WIDTH), BF16), jax.ShapeDtypeStruct((n, MLA_WIDTH), BF16)],
        compiler_params=_cparams(("arbitrary",)),
        name="prompt_attention",
    )(naq, nak, nav, q, ckv, krp, w_kvb)


def _na_kernel(q_ref, k_ref, v_ref, kc_ref, vc_ref, bias_ref, o_ref, *, rows):
    rt = pl.program_id(2)
    ws = jnp.clip(rt * NA_Q_ROWS - NA_KR // 2, 0, rows - NA_WIN_ROWS)
    start = pl.multiple_of(ws * GRID_W, GRID_W)
    nwin = NA_WIN_ROWS * GRID_W
    q = q_ref[...]
    s_loc = _dot_nt(q, k_ref[pl.ds(start, nwin), :]) + bias_ref[0, 0]
    s_ctx = _dot_nt(q, kc_ref[0].astype(BF16))
    m = jnp.maximum(jnp.max(s_loc, axis=-1, keepdims=True), jnp.max(s_ctx, axis=-1, keepdims=True))
    p_loc = jnp.exp2(s_loc - m)
    p_ctx = jnp.exp2(s_ctx - m)
    l = jnp.sum(p_loc, axis=-1, keepdims=True) + jnp.sum(p_ctx, axis=-1, keepdims=True)
    o = _dot(p_loc.astype(BF16), v_ref[pl.ds(start, nwin), :]) + _dot(p_ctx.astype(BF16), vc_ref[0].astype(BF16))
    o_ref[...] = (o / l).astype(BF16)


def _na_bias_table(rpb, rows):
    nh = rpb.shape[0]
    a = np.arange(NA_Q_ROWS)
    b = np.arange(NA_WIN_ROWS)
    col = np.arange(GRID_W)
    cs = np.clip(col - NA_KC // 2, 0, GRID_W - NA_KC)
    valid_col = (col[None, :] >= cs[:, None]) & (col[None, :] < cs[:, None] + NA_KC)
    padw = GRID_W - NA_KC
    rp = jnp.pad(rpb, ((0, 0), (0, 0), (padw, padw)))
    toep = jnp.stack([rp[:, :, GRID_W - 1 - qc:2 * GRID_W - 1 - qc] for qc in range(GRID_W)], axis=2)
    toep = jnp.where(valid_col[None, None], toep, NEG_BIG)
    masked = jnp.full((nh, GRID_W, GRID_W), NEG_BIG, F32)
    tabs = []
    for r0 in (0, NA_Q_ROWS, rows - NA_Q_ROWS):
        ws = int(np.clip(r0 - NA_KR // 2, 0, rows - NA_WIN_ROWS))
        r = r0 + a
        rs = np.clip(r - NA_KR // 2, 0, rows - NA_KR)
        kr = ws + b
        valid_row = (kr[None, :] >= rs[:, None]) & (kr[None, :] < rs[:, None] + NA_KR)
        dr = kr[None, :] - r[:, None] + NA_KR - 1
        tile_rows = []
        for ai in range(NA_Q_ROWS):
            blocks = [toep[:, int(dr[ai, bi])] if valid_row[ai, bi] else masked for bi in range(NA_WIN_ROWS)]
            tile_rows.append(jnp.concatenate(blocks, axis=-1))
        tabs.append(jnp.concatenate(tile_rows, axis=1))
    return jnp.stack(tabs)


def _sample_na(naq, nak, nav, kc, vc, bias, batch, seq):
    rows = seq // GRID_W
    tq = NA_Q_ROWS * GRID_W
    nt = seq // tq
    last = nt - 1

    def pat(rt):
        return jnp.where(rt == 0, 0, jnp.where(rt == last, 2, 1))

    return pl.pallas_call(
        functools.partial(_na_kernel, rows=rows),
        grid=(batch, NA_HEADS, nt),
        in_specs=[pl.BlockSpec((tq, NA_HEAD_DIM), lambda b, h, r: (b * nt + r, h)),
                  pl.BlockSpec((seq, NA_HEAD_DIM), lambda b, h, r: (b, h)),
                  pl.BlockSpec((seq, NA_HEAD_DIM), lambda b, h, r: (b, h)),
                  pl.BlockSpec((1, kc.shape[1], NA_HEAD_DIM), lambda b, h, r: (b, 0, h)),
                  pl.BlockSpec((1, vc.shape[1], NA_HEAD_DIM), lambda b, h, r: (b, 0, h)),
                  pl.BlockSpec((1, 1, tq, NA_WIN_ROWS * GRID_W), lambda b, h, r: (pat(r), h, 0, 0))],
        out_specs=pl.BlockSpec((tq, NA_HEAD_DIM), lambda b, h, r: (b * nt + r, h)),
        out_shape=jax.ShapeDtypeStruct((batch * seq, NA_WIDTH), BF16),
        compiler_params=_cparams(("arbitrary", "arbitrary", "arbitrary")),
        name="sample_neighbourhood_attention",
    )(naq, nak, nav, kc, vc, bias)


def _kv_expand_kernel(ckv_ref, krp_ref, w_ref, kf_ref, v_ref):
    kv = _dot(ckv_ref[0], w_ref[...])
    krp = krp_ref[0]
    kvw = QK_NOPE + V_DIM
    for hd in range(MLA_HEADS):
        kf_ref[0, hd, :, 0:QK_NOPE] = kv[:, hd * kvw:hd * kvw + QK_NOPE].astype(BF16)
        kf_ref[0, hd, :, QK_NOPE:Q_PAD] = krp
        v_ref[0, hd] = kv[:, hd * kvw + QK_NOPE:(hd + 1) * kvw].astype(BF16)


def _kv_expand(ckv, krp, w_kvb):
    batch, nkeys, _ = ckv.shape
    tm = 512
    return pl.pallas_call(
        _kv_expand_kernel,
        grid=(batch, nkeys // tm),
        in_specs=[pl.BlockSpec((1, tm, KV_LORA), lambda b, t: (b, t, 0)),
                  pl.BlockSpec((1, tm, LANE), lambda b, t: (b, t, 0)),
                  _const_spec(w_kvb.shape)],
        out_specs=[pl.BlockSpec((1, MLA_HEADS, tm, Q_PAD), lambda b, t: (b, 0, t, 0)),
                   pl.BlockSpec((1, MLA_HEADS, tm, V_DIM), lambda b, t: (b, 0, t, 0))],
        out_shape=[jax.ShapeDtypeStruct((batch, MLA_HEADS, nkeys, Q_PAD), BF16),
                   jax.ShapeDtypeStruct((batch, MLA_HEADS, nkeys, V_DIM), BF16)],
        compiler_params=_cparams(("arbitrary", "arbitrary")),
        name="latent_kv_expand",
    )(ckv, krp, w_kvb)


def _mla_kernel(q_ref, kf_ref, v_ref, o_ref, *, nkeys):
    q = q_ref[...]
    tq = q.shape[0]
    m = jnp.full((tq, 1), NEG_BIG, F32)
    l = jnp.zeros((tq, 1), F32)
    acc = jnp.zeros((tq, V_DIM), F32)
    for c in range(nkeys // MLA_TK):
        ks = slice(c * MLA_TK, (c + 1) * MLA_TK)
        s = _dot_nt(q, kf_ref[0, 0, ks, :])
        m_new = jnp.maximum(m, jnp.max(s, axis=-1, keepdims=True))
        alpha = jnp.exp2(m - m_new)
        p = jnp.exp2(s - m_new)
        l = alpha * l + jnp.sum(p, axis=-1, keepdims=True)
        acc = alpha * acc + _dot(p.astype(BF16), v_ref[0, 0, ks, :])
        m = m_new
    o_ref[...] = (acc / l).astype(BF16)


def _sample_mla(q, kf, v, batch, seq):
    nkeys = kf.shape[2]
    nt = seq // MLA_TQ
    return pl.pallas_call(
        functools.partial(_mla_kernel, nkeys=nkeys),
        grid=(batch, MLA_HEADS, nt),
        in_specs=[pl.BlockSpec((MLA_TQ, Q_PAD), lambda b, h, t: (b * nt + t, h)),
                  pl.BlockSpec((1, 1, nkeys, Q_PAD), lambda b, h, t: (b, h, 0, 0)),
                  pl.BlockSpec((1, 1, nkeys, V_DIM), lambda b, h, t: (b, h, 0, 0))],
        out_specs=pl.BlockSpec((MLA_TQ, V_DIM), lambda b, h, t: (b * nt + t, h)),
        out_shape=jax.ShapeDtypeStruct((batch * seq, MLA_WIDTH), BF16),
        compiler_params=_cparams(("arbitrary", "arbitrary", "arbitrary")),
        name="sample_latent_attention",
    )(q, kf, v)


def _post_kernel(xp_ref, xs_ref, onp_ref, omp_ref, ons_ref, oms_ref, mod_ref, wout_ref, gffn_ref, wrh_ref, wrl_ref,
                 br_ref, x1_ref, h2p_ref, idx_ref, gate_ref, *, n_prompt_tiles):
    is_prompt = pl.program_id(0) < n_prompt_tiles
    ga = mod_ref[0, 2:3, :]
    sf = mod_ref[0, 3:4, :]
    scf = mod_ref[0, 4:5, :]
    tm = ROW_TILE
    for r0 in range(0, xp_ref.shape[0], tm):
        rs = slice(r0, r0 + tm)
        x = jnp.where(is_prompt, xp_ref[rs, :], xs_ref[rs, :])
        ona = jnp.where(is_prompt, onp_ref[rs, :], ons_ref[rs, :])
        omla = jnp.where(is_prompt, omp_ref[rs, :], oms_ref[rs, :])
        o = _dot(ona, wout_ref[0:NA_WIDTH, :]) + _dot(omla, wout_ref[NA_WIDTH:NA_WIDTH + MLA_WIDTH, :])
        x1 = x + ga * o
        x1_ref[rs, :] = x1
        h2 = _rms(x1, gffn_ref[...]) * (1.0 + scf) + sf
        h_hi = h2.astype(BF16)
        h_lo = (h2 - h_hi.astype(F32)).astype(BF16)
        logits = _dot(h_hi, wrh_ref[...]) + _dot(h_lo, wrh_ref[...]) + _dot(h_hi, wrl_ref[...]) + br_ref[...]
        lane_e = lax.broadcasted_iota(jnp.int32, logits.shape, 1).astype(F32)
        lane_o = lax.broadcasted_iota(jnp.int32, (tm, LANE), 1)
        idx_out = jnp.zeros((tm, LANE), F32)
        gate_out = jnp.zeros((tm, LANE), F32)
        top0 = None
        denom = jnp.zeros((tm, 1), F32)
        cur = logits
        for k in range(TOP_K):
            mx = jnp.max(cur, axis=-1, keepdims=True)
            ix = jnp.min(jnp.where(cur == mx, lane_e, float(N_EXPERTS)), axis=-1, keepdims=True)
            cur = jnp.where(lane_e == ix, -jnp.inf, cur)
            if k == 0:
                top0 = mx
            e = jnp.exp(mx - top0)
            denom = denom + e
            idx_out = jnp.where(lane_o == k, ix, idx_out)
            gate_out = jnp.where(lane_o == k, e, gate_out)
        idx_ref[rs, :] = idx_out.astype(jnp.int32)
        gate_ref[rs, :] = gate_out / denom
        bits = pltpu.bitcast(h_hi.astype(F32), U32)
        for s in range(PACK_ROWS):
            lo = bits[:, s * LANE:(s + 1) * LANE] >> 16
            hi = bits[:, (s + PACK_ROWS) * LANE:(s + PACK_ROWS + 1) * LANE] & jnp.uint32(0xFFFF0000)
            h2p_ref[pl.ds(r0 * PACK_ROWS + s, tm, stride=PACK_ROWS), :] = hi | lo


def _post_attention(xp, xs, onp, omp, ons, oms, mod, w_out, g_ffn, w_router, b_router, *, mod_row):
    tm = POST_TILE
    wr_hi = w_router.astype(BF16)
    wr_lo = (w_router - wr_hi.astype(F32)).astype(BF16)
    npt = xp.shape[0] // tm
    nst = xs.shape[0] // tm
    n = xp.shape[0] + xs.shape[0]
    pidx = lambda i: (jnp.minimum(i, npt - 1), 0)
    sidx = lambda i: (jnp.maximum(i - npt, 0), 0)
    row = lambda i: (i, 0)
    return pl.pallas_call(
        functools.partial(_post_kernel, n_prompt_tiles=npt),
        grid=(npt + nst,),
        in_specs=[pl.BlockSpec((tm, D_MODEL), pidx), pl.BlockSpec((tm, D_MODEL), sidx),
                  pl.BlockSpec((tm, NA_WIDTH), pidx), pl.BlockSpec((tm, MLA_WIDTH), pidx),
                  pl.BlockSpec((tm, NA_WIDTH), sidx), pl.BlockSpec((tm, MLA_WIDTH), sidx),
                  pl.BlockSpec((1, 6, D_MODEL), lambda i: (mod_row(i), 0, 0)),
                  _const_spec(w_out.shape), _const_spec((1, D_MODEL)),
                  _const_spec(w_router.shape), _const_spec(w_router.shape), _const_spec((1, N_EXPERTS))],
        out_specs=[pl.BlockSpec((tm, D_MODEL), row),
                   pl.BlockSpec((tm * PACK_ROWS, LANE), row),
                   pl.BlockSpec((tm, LANE), row),
                   pl.BlockSpec((tm, LANE), row)],
        out_shape=[jax.ShapeDtypeStruct((n, D_MODEL), F32),
                   jax.ShapeDtypeStruct((n * PACK_ROWS, LANE), U32),
                   jax.ShapeDtypeStruct((n, LANE), jnp.int32),
                   jax.ShapeDtypeStruct((n, LANE), F32)],
        compiler_params=_cparams(("arbitrary",)),
        name="post_attention_router",
    )(xp, xs, onp, omp, ons, oms, mod, w_out, g_ffn, wr_hi, wr_lo, b_router)


def _deinterleave_kernel(w_ref, p_ref, o_ref):
    p = p_ref[...]
    grp = p.shape[0]
    per_tile = o_ref.shape[3] // grp
    for c in range(w_ref.shape[2] // grp):
        w = w_ref[0, :, c * grp:(c + 1) * grp].astype(BF16)
        col = (c % per_tile) * grp
        o_ref[0, c // per_tile, :, col:col + grp] = _dot(w, p).astype(BF16)


def _deinterleave_gate_up(w_gate_up):
    ne, d, n2 = w_gate_up.shape
    nj = n2 // (2 * MOE_TF)
    grp = 2 * LANE
    dst = np.arange(grp)
    src = np.where(dst < LANE, 2 * dst, 2 * (dst - LANE) + 1)
    perm = np.zeros((grp, grp), np.float32)
    perm[src, dst] = 1.0
    tk = 512
    return pl.pallas_call(
        _deinterleave_kernel,
        grid=(ne, d // tk),
        in_specs=[pl.BlockSpec((1, tk, n2), lambda e, k: (e, k, 0)), _const_spec((grp, grp))],
        out_specs=pl.BlockSpec((1, nj, tk, 2 * MOE_TF), lambda e, k: (e, 0, k, 0)),
        out_shape=jax.ShapeDtypeStruct((ne, nj, d, 2 * MOE_TF), BF16),
        compiler_params=_cparams(("arbitrary", "arbitrary")),
        name="deinterleave_gate_up",
    )(w_gate_up, jnp.asarray(perm, BF16))


def _moe_kernel(ce_ref, nv_ref, cs_ref, order_ref, h_ref, wgu_ref, bgu_ref, wd_ref, bd_ref, y_ref,
                xg_ref, xbf_ref, acc_ref, wdb_ref, ost_ref, sem_g, sem_s):
    c = pl.program_id(0)
    j = pl.program_id(1)
    last_c = pl.num_programs(0) - 1
    last_j = pl.num_programs(1) - 1
    nk = order_ref.shape[0]
    nsub = MOE_CHUNK // MOE_SUB
    xg_rows = MOE_CHUNK * PACK_ROWS

    nv = nv_ref[c]
    c_next = jnp.minimum(c + 1, last_c)
    nv_next = jnp.where(c < last_c, nv_ref[c_next], 0)
    cs_next = cs_ref[c_next]
    c_prev = jnp.maximum(c - 1, 0)
    nv_prev = jnp.where(c > 0, nv_ref[c_prev], 0)
    cs_prev = cs_ref[c_prev]
    nv_prev2 = jnp.where(c > 1, nv_ref[jnp.maximum(c - 2, 0)], 0)
    ost_rows = MOE_CHUNK * OUT_ROWS
    slot = c % 2
    slot_next = (c + 1) % 2

    def gather_row(cs, dst_slot, row):
        tok = jnp.right_shift(order_ref[jnp.minimum(cs + row, nk - 1)], TOP_K_SHIFT)
        src = pl.multiple_of(tok * PACK_ROWS, PACK_ROWS)
        dst = pl.multiple_of(dst_slot * xg_rows + row * PACK_ROWS, PACK_ROWS)
        pltpu.make_async_copy(h_ref.at[pl.ds(src, PACK_ROWS)], xg_ref.at[pl.ds(dst, PACK_ROWS)],
                              sem_g.at[dst_slot]).start()

    def scatter_row(row):
        flat = order_ref[jnp.minimum(cs_prev + row, nk - 1)]
        dest = jnp.where(row < nv_prev, flat, nk + slot_next * MOE_CHUNK + row)
        src = pl.multiple_of(slot_next * ost_rows + row * OUT_ROWS, OUT_ROWS)
        dst = pl.multiple_of(dest * OUT_ROWS, OUT_ROWS)
        pltpu.make_async_copy(ost_ref.at[pl.ds(src, OUT_ROWS)], y_ref.at[pl.ds(dst, OUT_ROWS)],
                              sem_s.at[slot_next]).start()

    def rolled(fn, row0, nrows):
        def body(r0, carry):
            for u in range(DMA_UNROLL):
                fn(row0 + r0 * DMA_UNROLL + u)
            return carry

        lax.fori_loop(0, nrows // DMA_UNROLL, body, 0)

    @pl.when(jnp.logical_and(jnp.logical_and(c == 0, j == 0), nv > 0))
    def _():
        rolled(lambda row: gather_row(cs_ref[0], 0, row), 0, MOE_CHUNK)

    def unpack(sb):
        for s in range(PACK_ROWS):
            w = xg_ref[pl.ds(slot * xg_rows + sb * MOE_SUB * PACK_ROWS + s, MOE_SUB, stride=PACK_ROWS), :]
            lo = pltpu.bitcast(w << 16, F32)
            hi = pltpu.bitcast(w & jnp.uint32(0xFFFF0000), F32)
            xbf_ref[sb * MOE_SUB:(sb + 1) * MOE_SUB, s * LANE:(s + 1) * LANE] = lo.astype(BF16)
            xbf_ref[sb * MOE_SUB:(sb + 1) * MOE_SUB,
                    (s + PACK_ROWS) * LANE:(s + PACK_ROWS + 1) * LANE] = hi.astype(BF16)

    @pl.when(jnp.logical_and(j == 0, nv > 0))
    def _():
        base = pl.multiple_of(slot * xg_rows, xg_rows)
        pltpu.make_async_copy(h_ref.at[pl.ds(0, xg_rows)], xg_ref.at[pl.ds(base, xg_rows)], sem_g.at[slot]).wait()

        @pl.when(c == 0)
        def _():
            ost_ref[...] = jnp.zeros(ost_ref.shape, F32)

    dma_both = jnp.logical_and(nv_next > 0, nv_prev > 0)
    per_sb = MOE_SUB // nsub

    def computes(sb):
        return sb * MOE_SUB < nv if sb < nsub else jnp.bool_(False)

    def is_fused(sb):
        return jnp.logical_and(computes(sb), dma_both)

    def stage(sb):
        for s in range(OUT_ROWS):
            dst = pl.ds(slot * ost_rows + sb * MOE_SUB * OUT_ROWS + s, MOE_SUB, stride=OUT_ROWS)
            ost_ref[dst, :] = acc_ref[sb * MOE_SUB:(sb + 1) * MOE_SUB, s * LANE:(s + 1) * LANE]

    def sub_block(sb, seed):
        if sb == 0:
            wdb_ref[...] = wd_ref[0].astype(BF16)
        x = xbf_ref[sb * MOE_SUB:(sb + 1) * MOE_SUB, :]
        gu = _dot(x, wgu_ref[0, 0]) + bgu_ref[0]
        ng = MOE_TF // LANE
        g = jnp.concatenate([gu[:, 2 * b * LANE:(2 * b + 1) * LANE] for b in range(ng)], axis=-1)
        u = jnp.concatenate([gu[:, (2 * b + 1) * LANE:(2 * b + 2) * LANE] for b in range(ng)], axis=-1)
        g = jnp.minimum(g, SWIGLU_LIMIT)
        u = jnp.clip(u, -SWIGLU_LIMIT, SWIGLU_LIMIT)
        act = g * (1.0 / (1.0 + jnp.exp(-(g * SWIGLU_ALPHA)))) * (u + 1.0)
        part = _dot(act.astype(BF16), wdb_ref[...])
        rows = slice(sb * MOE_SUB, (sb + 1) * MOE_SUB)
        if seed:
            acc_ref[rows, :] = part + bd_ref[0]
        else:
            acc_ref[rows, :] += part

    def fused_region(sb, row0, mode):
        for r in range(per_sb):
            gather_row(cs_next, slot_next, row0 + r)
            scatter_row(row0 + r)
        if mode == "first":
            if sb == 0:
                unpack(0)
            if sb + 1 < nsub:
                unpack(sb + 1)
        sub_block(sb, seed=(mode == "first"))
        if mode == "last" and sb > 0:
            stage(sb - 1)

    def wait_scatter(buf):
        base = pl.multiple_of(buf * ost_rows, ost_rows)
        pltpu.make_async_copy(ost_ref.at[pl.ds(base, ost_rows)], y_ref.at[pl.ds(0, ost_rows)], sem_s.at[buf]).wait()

    @pl.when(jnp.logical_and(j == last_j, nv_prev2 > 0))
    def _():
        wait_scatter(slot)

    for sb in range(nsub):
        row0 = j * MOE_SUB + sb * per_sb
        compute = computes(sb)
        fused = is_fused(sb)

        @pl.when(jnp.logical_and(fused, j == 0))
        def _():
            fused_region(sb, row0, "first")

        @pl.when(jnp.logical_and(fused, jnp.logical_and(j > 0, j < last_j)))
        def _():
            fused_region(sb, row0, "mid")

        @pl.when(jnp.logical_and(fused, j == last_j))
        def _():
            fused_region(sb, row0, "last")

        @pl.when(jnp.logical_not(fused))
        def _():
            @pl.when(nv_next > 0)
            def _():
                for r in range(per_sb):
                    gather_row(cs_next, slot_next, row0 + r)

            @pl.when(nv_prev > 0)
            def _():
                for r in range(per_sb):
                    scatter_row(row0 + r)

            @pl.when(compute)
            def _():
                @pl.when(j == 0)
                def _():
                    unpack(sb)
                    acc_ref[sb * MOE_SUB:(sb + 1) * MOE_SUB, :] = jnp.broadcast_to(bd_ref[0], (MOE_SUB, D_MODEL))

                sub_block(sb, seed=False)

    @pl.when(j == last_j)
    def _():
        for sb in range(nsub):
            @pl.when(jnp.logical_and(computes(sb), jnp.logical_not(is_fused(sb + 1))))
            def _():
                stage(sb)

        @pl.when(jnp.logical_and(c == last_c, nv_prev > 0))
        def _():
            wait_scatter(slot_next)


def _moe(chunk_expert, chunk_rows, chunk_start, order, h2p, wgu, bgu, w_down, b_down):
    nj = D_FF // MOE_TF
    ngrid = chunk_expert.shape[0]
    nk = order.shape[0]

    def jj(c, j, nv):
        return jnp.where(nv[c] > 0, j, nj - 1)

    return pl.pallas_call(
        _moe_kernel,
        grid_spec=pltpu.PrefetchScalarGridSpec(
            num_scalar_prefetch=4,
            grid=(ngrid, nj),
            in_specs=[
                pl.BlockSpec(memory_space=pl.ANY),
                pl.BlockSpec((1, 1, D_MODEL, 2 * MOE_TF), lambda c, j, ce, nv, cs, od: (ce[c], jj(c, j, nv), 0, 0)),
                pl.BlockSpec((1, 1, 2 * MOE_TF), lambda c, j, ce, nv, cs, od: (ce[c], 0, jj(c, j, nv))),
                pl.BlockSpec((1, MOE_TF, D_MODEL), lambda c, j, ce, nv, cs, od: (ce[c], jj(c, j, nv), 0)),
                pl.BlockSpec((1, 1, D_MODEL), lambda c, j, ce, nv, cs, od: (ce[c], 0, 0))],
            out_specs=pl.BlockSpec(memory_space=pl.ANY),
            scratch_shapes=[pltpu.VMEM((2 * MOE_CHUNK * PACK_ROWS, LANE), U32),
                            pltpu.VMEM((MOE_CHUNK, D_MODEL), BF16),
                            pltpu.VMEM((MOE_CHUNK, D_MODEL), F32),
                            pltpu.VMEM((MOE_TF, D_MODEL), BF16),
                            pltpu.VMEM((2 * MOE_CHUNK * OUT_ROWS, LANE), F32),
                            pltpu.SemaphoreType.DMA((2,)),
                            pltpu.SemaphoreType.DMA((2,))]),
        out_shape=jax.ShapeDtypeStruct(((nk + 2 * MOE_CHUNK) * OUT_ROWS, LANE), F32),
        compiler_params=_cparams(("arbitrary", "arbitrary")),
        name="moe_experts",
    )(chunk_expert, chunk_rows, chunk_start, order, h2p, wgu, bgu, w_down, b_down)


def _combine_kernel(gate_ref, x1_ref, mod_ref, gfin_ref, ye_ref, y_ref):
    gate = gate_ref[...]
    pieces = []
    for s in range(OUT_ROWS):
        acc = None
        for k in range(TOP_K):
            rows = ye_ref[pl.ds(k * OUT_ROWS + s, COMBINE_TOKENS, stride=TOP_K * OUT_ROWS), :]
            term = gate[:, k:k + 1] * rows
            acc = term if acc is None else acc + term
        pieces.append(acc)
    y = jnp.concatenate(pieces, axis=-1)
    gf = mod_ref[0, 5:6, :]
    y_ref[...] = _rms(x1_ref[...] + gf * y, gfin_ref[...])


def _combine(gates, x1, mod, g_final, y_experts, *, tile0, ntiles, mod_row):
    tt = COMBINE_TOKENS
    return pl.pallas_call(
        _combine_kernel,
        grid=(ntiles,),
        in_specs=[pl.BlockSpec((tt, LANE), lambda i: (tile0 + i, 0)),
                  pl.BlockSpec((tt, D_MODEL), lambda i: (tile0 + i, 0)),
                  pl.BlockSpec((1, 6, D_MODEL), lambda i: (mod_row(i), 0, 0)),
                  _const_spec((1, D_MODEL)),
                  pl.BlockSpec((tt * TOP_K * OUT_ROWS, LANE), lambda i: (tile0 + i, 0))],
        out_specs=pl.BlockSpec((tt, D_MODEL), lambda i: (i, 0)),
        out_shape=jax.ShapeDtypeStruct((ntiles * tt, D_MODEL), F32),
        compiler_params=_cparams(("arbitrary",)),
        name="moe_combine_final_norm",
    )(gates, x1, mod, g_final, y_experts)


def _routing(top_idx, nch):
    n = top_idx.shape[0]
    nk = n * TOP_K
    flat_e = top_idx.reshape(nk)
    order = jnp.argsort(flat_e, stable=True).astype(jnp.int32)
    onehot = flat_e[:, None] == jnp.arange(N_EXPERTS, dtype=jnp.int32)[None, :]
    counts = jnp.sum(onehot.astype(jnp.int32), axis=0)
    grp_start = jnp.cumsum(counts) - counts
    chunks_e = (counts + MOE_CHUNK - 1) // MOE_CHUNK
    chunk_end = jnp.cumsum(chunks_e)
    chunk_start = chunk_end - chunks_e

    total = chunk_end[-1]
    cidx = jnp.arange(nch + 1, dtype=jnp.int32)
    active = cidx < total
    ce = jnp.minimum(jnp.searchsorted(chunk_end, cidx, side='right'), N_EXPERTS - 1).astype(jnp.int32)
    local = (cidx - chunk_start[ce]) * MOE_CHUNK
    nv = jnp.where(active, jnp.clip(counts[ce] - local, 0, MOE_CHUNK), 0).astype(jnp.int32)
    last = jnp.maximum(total - 1, 0)
    chunk_expert = jnp.where(active, ce, ce[last]).astype(jnp.int32)
    chunk_first = jnp.where(active, jnp.clip(grp_start[ce] + local, 0, nk - 1), 0).astype(jnp.int32)
    return order, chunk_expert, nv, chunk_first


def _rope_tables(t):
    pos = jnp.arange(t)
    rows = (pos // GRID_W).astype(F32)
    cols = (pos % GRID_W).astype(F32)
    inv = ROPE_THETA ** (-(jnp.arange(ROPE_AXIS // 2, dtype=F32) * 2.0 / ROPE_AXIS))
    ar = rows[:, None] * inv
    ac = cols[:, None] * inv
    ang = jnp.concatenate([ar, ar, ac, ac], axis=-1)
    return jnp.cos(ang), jnp.sin(ang)


def _rot_cols(w):
    half = ROPE_AXIS // 2
    src = np.concatenate([np.arange(half, ROPE_AXIS), np.arange(0, half),
                          np.arange(ROPE_AXIS + half, 2 * ROPE_AXIS), np.arange(ROPE_AXIS, ROPE_AXIS + half)])
    sign = np.concatenate([-np.ones(half), np.ones(half), -np.ones(half), np.ones(half)]).astype(np.float32)
    return w[..., src] * sign


def kernel(x_prompt, x_sample, cache_na_k, cache_na_v, cache_mla_ckv, cache_mla_krope, c, c_ctx, g_attn, g_ffn, g_final, w_mod, b_mod, w_in, w_out, na_rpb, g_q_a, w_q_b, g_kv_a, w_kv_b, w_router, b_router, w_gate_up, b_gate_up, w_down, b_down):
    bp, sp, d = x_prompt.shape
    bd, td, _ = x_sample.shape
    assert d == D_MODEL and w_mod.shape[0] == 1, "one trunk layer of width D_MODEL"
    n_p = bp * sp
    n_s = bd * td
    xp = x_prompt.reshape(n_p, d)
    xs = x_sample.reshape(n_s, d)

    c8 = jnp.zeros((8, d), F32).at[0].set(c_ctx).at[1:1 + bd].set(c)
    mod = _modulation(c8, w_mod[0], b_mod[0].reshape(1, -1)).reshape(8, 6, d)

    w_in0 = w_in[0]
    w_kr = w_in0[:, KR_OFF:KR_OFF + QK_ROPE]
    w_in_p = jnp.concatenate([w_in0, jnp.zeros((d, LANE - QK_ROPE), F32)], axis=1).astype(BF16)
    w_in_s = jnp.concatenate([w_in0, _rot_cols(w_kr)], axis=1).astype(BF16)
    wq = w_q_b[0].reshape(Q_LORA, MLA_HEADS, QK_NOPE + QK_ROPE)
    zpad = jnp.zeros((Q_LORA, MLA_HEADS, Q_PAD - QK_NOPE - QK_ROPE), F32)
    wq_pad = jnp.concatenate([wq, zpad], axis=-1).reshape(Q_LORA, MLA_HEADS * Q_PAD)
    wq_rot = jnp.concatenate([jnp.zeros((Q_LORA, MLA_HEADS, QK_NOPE), F32), _rot_cols(wq[..., QK_NOPE:]), zpad],
                             axis=-1).reshape(Q_LORA, MLA_HEADS * Q_PAD)
    wqb_p = wq_pad.astype(BF16)
    wqb_s = jnp.concatenate([wq_pad, wq_rot], axis=1).astype(BF16)
    w_kvb = w_kv_b[0].astype(BF16)
    cos, sin = _rope_tables(td)
    cosq = jnp.concatenate([jnp.ones((td, QK_NOPE), F32), cos, jnp.ones((td, Q_PAD - QK_NOPE - QK_ROPE), F32)], axis=1)
    sinq = jnp.concatenate([jnp.zeros((td, QK_NOPE), F32), sin, jnp.zeros((td, Q_PAD - QK_NOPE - QK_ROPE), F32)], axis=1)
    cosq = cosq * MLA_QSCALE
    sinq = sinq * MLA_QSCALE
    csk = jnp.concatenate([cos, sin], axis=1)

    tiles_per_seq = td // ROW_TILE
    g_attn2 = g_attn[0].reshape(1, d)
    gq2 = g_q_a[0].reshape(1, Q_LORA)
    gkv2 = g_kv_a[0].reshape(1, KV_LORA)

    naq_p, nak_p, nav_p, q_p, ckv_p, kr_p, krp_p = _pre_attention(
        xp, mod, g_attn2, w_in_p, gq2, wqb_p, gkv2, None, rope=False, mod_row=lambda i: 0)
    ona_p, omla_p = _prompt_attention(naq_p, nak_p, nav_p, q_p, ckv_p, krp_p, w_kvb, sp)

    naq_s, nak_s, nav_s, q_s, ckv_s, krp_s = _pre_attention(
        xs, mod, g_attn2, w_in_s, gq2, wqb_s, gkv2, (cosq, sinq, csk), rope=True,
        mod_row=lambda i: 1 + i // tiles_per_seq)
    past = cache_na_k.shape[2]
    kc = cache_na_k[:, 0].reshape(bd, past, NA_WIDTH)
    vc = cache_na_v[:, 0].reshape(bd, past, NA_WIDTH)
    bias = _na_bias_table(na_rpb[0] * LOG2E, td // GRID_W)
    ona_s = _sample_na(naq_s, nak_s, nav_s, kc, vc, bias, bd, td)
    ckv_all = jnp.concatenate([ckv_s.reshape(bd, td, KV_LORA), cache_mla_ckv[:, 0].astype(BF16)], axis=1)
    krp_c = jnp.concatenate([cache_mla_krope[:, 0], jnp.zeros((bd, past, LANE - QK_ROPE), F32)], axis=-1).astype(BF16)
    krp_all = jnp.concatenate([krp_s.reshape(bd, td, LANE), krp_c], axis=1)
    kf, vv = _kv_expand(ckv_all, krp_all, w_kvb)
    omla_s = _sample_mla(q_s, kf, vv, bd, td)

    npt = n_p // POST_TILE
    post_tiles_per_seq = td // POST_TILE
    x1, h2p, idx128, gate128 = _post_attention(
        xp, xs, ona_p, omla_p, ona_s, omla_s, mod, w_out[0].astype(BF16), g_ffn[0].reshape(1, d),
        w_router[0], b_router[0].reshape(1, N_EXPERTS),
        mod_row=lambda i: jnp.where(i < npt, 0, 1 + jnp.maximum(i - npt, 0) // post_tiles_per_seq))

    n = n_p + n_s
    nch = n * TOP_K // MOE_CHUNK + N_EXPERTS
    order, chunk_expert, chunk_rows, chunk_first = _routing(idx128[:, :TOP_K], nch)
    wgu = _deinterleave_gate_up(w_gate_up[0])
    ng = 2 * D_FF // (2 * LANE)
    bgu = b_gate_up[0].reshape(N_EXPERTS, ng, LANE, 2).transpose(0, 1, 3, 2).reshape(N_EXPERTS, 1, 2 * D_FF)
    y_experts = _moe(chunk_expert, chunk_rows, chunk_first, order, h2p, wgu, bgu, w_down[0],
                     b_down[0].reshape(N_EXPERTS, 1, d))

    tt = COMBINE_TOKENS
    gfin = g_final.reshape(1, d)
    ctiles_seq = td // tt
    y_p = _combine(gate128, x1, mod, gfin, y_experts, tile0=0, ntiles=n_p // tt, mod_row=lambda i: 0)
    y_s = _combine(gate128, x1, mod, gfin, y_experts, tile0=n_p // tt, ntiles=n_s // tt,
                   mod_row=lambda i: 1 + i // ctiles_seq)

    return (y_p.reshape(bp, sp, d), y_s.reshape(bd, td, d),
            nak_p.reshape(bp, 1, sp, NA_HEADS, NA_HEAD_DIM), nav_p.reshape(bp, 1, sp, NA_HEADS, NA_HEAD_DIM),
            ckv_p.reshape(bp, 1, sp, KV_LORA), kr_p.reshape(bp, 1, sp, QK_ROPE))
```

```python
import functools

import numpy as np
import jax
import jax.numpy as jnp
from jax import lax
from jax.experimental import pallas as pl
from jax.experimental.pallas import tpu as pltpu

F32 = jnp.float32
BF16 = jnp.bfloat16
U32 = jnp.uint32

D_MODEL = 2048
GRID_W = 64
NA_HEADS = 8
NA_HEAD_DIM = 128
NA_KR = 8
NA_KC = 16
MLA_HEADS = 8
Q_LORA = 512
KV_LORA = 256
QK_NOPE = 128
QK_ROPE = 64
V_DIM = 128
ROPE_AXIS = QK_ROPE // 2
ROPE_THETA = 10000.0
NA_WIDTH = NA_HEADS * NA_HEAD_DIM
MLA_WIDTH = MLA_HEADS * V_DIM
IN_COLS = 3 * NA_WIDTH + Q_LORA + KV_LORA + QK_ROPE
N_EXPERTS = 32
TOP_K = 4
D_FF = D_MODEL
SWIGLU_ALPHA = 1.702
SWIGLU_LIMIT = 7.0
EPS = 1e-6

LANE = 128
Q_PAD = 2 * LANE
KR_OFF = 3 * NA_WIDTH + Q_LORA + KV_LORA
IN_COLS_PAD = KR_OFF + LANE
VMEM_LIMIT = 56 * 1024 * 1024
NEG_BIG = -1e30

LOG2E = 1.4426950408889634
NA_QSCALE = NA_HEAD_DIM ** -0.5 * LOG2E
MLA_QSCALE = (QK_NOPE + QK_ROPE) ** -0.5 * LOG2E

ROW_TILE = 256
POST_TILE = 2 * ROW_TILE
NA_Q_ROWS = 4
NA_WIN_ROWS = 12
NA_HEADS_PER_STEP = 2
MLA_TQ = 512
MLA_TK = 2304
MOE_CHUNK = 1024
MOE_SUB = 256
MOE_TF = 512
COMBINE_TOKENS = 256
TOP_K_SHIFT = TOP_K.bit_length() - 1
DMA_UNROLL = 8
PACK_ROWS = D_MODEL // (2 * LANE)
OUT_ROWS = D_MODEL // LANE


def _cparams(sem):
    return pltpu.CompilerParams(dimension_semantics=sem, vmem_limit_bytes=VMEM_LIMIT)


def _const_spec(shape):
    nd = len(shape)
    return pl.BlockSpec(shape, lambda *a: (0,) * nd, pipeline_mode=pl.Buffered(1))


def _rms(x, g):
    return x * lax.rsqrt(jnp.mean(x * x, axis=-1, keepdims=True) + EPS) * g


def _dot(a, b):
    return jnp.dot(a, b, preferred_element_type=F32)


def _dot_nt(a, b):
    return lax.dot_general(a, b, (((1,), (1,)), ((), ())), preferred_element_type=F32)


def _mod_kernel(c_ref, w_ref, b_ref, o_ref):
    c = c_ref[...]
    s = c / (1.0 + jnp.exp(-c))
    o_ref[...] = _dot(s.astype(BF16), w_ref[...].astype(BF16)) + b_ref[...]


def _modulation(c8, w_mod, b_mod):
    n = w_mod.shape[1]
    tn = 1024
    return pl.pallas_call(
        _mod_kernel,
        grid=(n // tn,),
        in_specs=[pl.BlockSpec((8, D_MODEL), lambda j: (0, 0)),
                  pl.BlockSpec((D_MODEL, tn), lambda j: (0, j)),
                  pl.BlockSpec((1, tn), lambda j: (0, j))],
        out_specs=pl.BlockSpec((8, tn), lambda j: (0, j)),
        out_shape=jax.ShapeDtypeStruct((8, n), F32),
        compiler_params=_cparams(("arbitrary",)),
        name="modulation",
    )(c8, w_mod, b_mod)


def _pre_kernel(*refs, rope):
    if rope:
        (x_ref, mod_ref, g_ref, win_ref, gq_ref, wqb_ref, gkv_ref, cosq_ref, sinq_ref, csk_ref,
         naq_ref, nak_ref, nav_ref, q_ref, ckv_ref, krp_ref) = refs
    else:
        (x_ref, mod_ref, g_ref, win_ref, gq_ref, wqb_ref, gkv_ref,
         naq_ref, nak_ref, nav_ref, q_ref, ckv_ref, kr_ref, krp_ref) = refs
    x = x_ref[...]
    sa = mod_ref[0, 0:1, :]
    sca = mod_ref[0, 1:2, :]
    h = (_rms(x, g_ref[...]) * (1.0 + sca) + sa).astype(BF16)
    proj = _dot(h, win_ref[...])
    naq_ref[...] = (proj[:, 0:NA_WIDTH] * NA_QSCALE).astype(naq_ref.dtype)
    nak_ref[...] = proj[:, NA_WIDTH:2 * NA_WIDTH].astype(nak_ref.dtype)
    nav_ref[...] = proj[:, 2 * NA_WIDTH:3 * NA_WIDTH].astype(nav_ref.dtype)
    q_a = proj[:, 3 * NA_WIDTH:3 * NA_WIDTH + Q_LORA]
    kv_a = proj[:, 3 * NA_WIDTH + Q_LORA:KR_OFF]
    krx = proj[:, KR_OFF:IN_COLS_PAD]
    qan = _rms(q_a, gq_ref[...]).astype(BF16)
    qq = _dot(qan, wqb_ref[...])
    ckv_ref[...] = _rms(kv_a, gkv_ref[...]).astype(ckv_ref.dtype)
    if rope:
        width = MLA_HEADS * Q_PAD
        cosq = cosq_ref[...]
        sinq = sinq_ref[...]
        for hd in range(MLA_HEADS):
            a = qq[:, hd * Q_PAD:(hd + 1) * Q_PAD]
            b = qq[:, width + hd * Q_PAD:width + (hd + 1) * Q_PAD]
            q_ref[:, hd * Q_PAD:(hd + 1) * Q_PAD] = (a * cosq + b * sinq).astype(BF16)
        y = krx * csk_ref[...]
        y = y + pltpu.roll(y, QK_ROPE, 1)
        lane = lax.broadcasted_iota(jnp.int32, y.shape, 1)
        krp_ref[...] = jnp.where(lane < QK_ROPE, y, 0.0).astype(BF16)
    else:
        q_ref[...] = (qq * MLA_QSCALE).astype(BF16)
        kr_ref[...] = krx[:, 0:QK_ROPE]
        krp_ref[...] = krx.astype(BF16)


def _pre_attention(x, mod, g_attn, w_in, g_q_a, w_qb, g_kv_a, rope_tabs, *, rope, mod_row):
    n = x.shape[0]
    tm = ROW_TILE
    row = lambda i: (i, 0)
    in_specs = [pl.BlockSpec((tm, D_MODEL), row),
                pl.BlockSpec((1, 6, D_MODEL), lambda i: (mod_row(i), 0, 0)),
                _const_spec((1, D_MODEL)),
                _const_spec(w_in.shape),
                _const_spec((1, Q_LORA)),
                _const_spec(w_qb.shape),
                _const_spec((1, KV_LORA))]
    args = [x, mod, g_attn, w_in, g_q_a, w_qb, g_kv_a]
    qw = MLA_HEADS * Q_PAD
    if rope:
        tiles_per_seq = rope_tabs[0].shape[0] // tm
        pos = lambda i: (i % tiles_per_seq, 0)
        in_specs += [pl.BlockSpec((tm, Q_PAD), pos), pl.BlockSpec((tm, Q_PAD), pos),
                     pl.BlockSpec((tm, LANE), pos)]
        args += list(rope_tabs)
        out_shape = [jax.ShapeDtypeStruct((n, NA_WIDTH), BF16)] * 3 + [
            jax.ShapeDtypeStruct((n, qw), BF16),
            jax.ShapeDtypeStruct((n, KV_LORA), BF16),
            jax.ShapeDtypeStruct((n, LANE), BF16)]
        out_specs = [pl.BlockSpec((tm, NA_WIDTH), row)] * 3 + [
            pl.BlockSpec((tm, qw), row), pl.BlockSpec((tm, KV_LORA), row), pl.BlockSpec((tm, LANE), row)]
    else:
        out_shape = [jax.ShapeDtypeStruct((n, NA_WIDTH), BF16),
                     jax.ShapeDtypeStruct((n, NA_WIDTH), F32),
                     jax.ShapeDtypeStruct((n, NA_WIDTH), F32),
                     jax.ShapeDtypeStruct((n, qw), BF16),
                     jax.ShapeDtypeStruct((n, KV_LORA), F32),
                     jax.ShapeDtypeStruct((n, QK_ROPE), F32),
                     jax.ShapeDtypeStruct((n, LANE), BF16)]
        out_specs = [pl.BlockSpec((tm, NA_WIDTH), row)] * 3 + [
            pl.BlockSpec((tm, qw), row), pl.BlockSpec((tm, KV_LORA), row),
            pl.BlockSpec((tm, QK_ROPE), row), pl.BlockSpec((tm, LANE), row)]
    return pl.pallas_call(
        functools.partial(_pre_kernel, rope=rope),
        grid=(n // tm,),
        in_specs=in_specs,
        out_specs=out_specs,
        out_shape=out_shape,
        compiler_params=_cparams(("arbitrary",)),
        name="pre_attention_rope" if rope else "pre_attention",
    )(*args)


def _softmax_pv(s, v):
    m = jnp.max(s, axis=-1, keepdims=True)
    p = jnp.exp2(s - m)
    l = jnp.sum(p, axis=-1, keepdims=True)
    return _dot(p.astype(BF16), v) / l


def _prompt_attn_kernel(naq_ref, nak_ref, nav_ref, q_ref, ckv_ref, krp_ref, wkvb_ref, ona_ref, omla_ref):
    kv = _dot(ckv_ref[...].astype(BF16), wkvb_ref[...])
    krp = krp_ref[...]
    for hd in range(NA_HEADS):
        sl = slice(hd * NA_HEAD_DIM, (hd + 1) * NA_HEAD_DIM)
        s = _dot_nt(naq_ref[:, sl], nak_ref[:, sl].astype(BF16))
        ona_ref[:, sl] = _softmax_pv(s, nav_ref[:, sl].astype(BF16)).astype(BF16)
    kvw = QK_NOPE + V_DIM
    for hd in range(MLA_HEADS):
        kf = jnp.concatenate([kv[:, hd * kvw:hd * kvw + QK_NOPE].astype(BF16), krp], axis=-1)
        s = _dot_nt(q_ref[:, hd * Q_PAD:(hd + 1) * Q_PAD], kf)
        v = kv[:, hd * kvw + QK_NOPE:(hd + 1) * kvw].astype(BF16)
        omla_ref[:, hd * V_DIM:(hd + 1) * V_DIM] = _softmax_pv(s, v).astype(BF16)


def _prompt_attention(naq, nak, nav, q, ckv, krp, w_kvb, seq):
    n = naq.shape[0]
    row = lambda b: (b, 0)
    return pl.pallas_call(
        _prompt_attn_kernel,
        grid=(n // seq,),
        in_specs=[pl.BlockSpec((seq, NA_WIDTH), row)] * 3 + [
            pl.BlockSpec((seq, MLA_HEADS * Q_PAD), row),
            pl.BlockSpec((seq, KV_LORA), row),
            pl.BlockSpec((seq, LANE), row),
            _const_spec(w_kvb.shape)],
        out_specs=[pl.BlockSpec((seq, NA_WIDTH), row), pl.BlockSpec((seq, MLA_WIDTH), row)],
        out_shape=[jax.ShapeDtypeStruct((n, NA_WIDTH), BF16), jax.ShapeDtypeStruct((n, MLA_WIDTH), BF16)],
        compiler_params=_cparams(("arbitrary",)),
        name="prompt_attention",
    )(naq, nak, nav, q, ckv, krp, w_kvb)


def _na_kernel(q_ref, k_ref, v_ref, kc_ref, vc_ref, bias_ref, o_ref, *, rows):
    rt = pl.program_id(2)
    ws = jnp.clip(rt * NA_Q_ROWS - NA_KR // 2, 0, rows - NA_WIN_ROWS)
    start = pl.multiple_of(ws * GRID_W, GRID_W)
    nwin = NA_WIN_ROWS * GRID_W
    for hh in range(NA_HEADS_PER_STEP):
        hs = slice(hh * NA_HEAD_DIM, (hh + 1) * NA_HEAD_DIM)
        q = q_ref[:, hs]
        s_loc = _dot_nt(q, k_ref[pl.ds(start, nwin), hs]) + bias_ref[0, hh]
        s_ctx = _dot_nt(q, kc_ref[0, :, hs].astype(BF16))
        m = jnp.maximum(jnp.max(s_loc, axis=-1, keepdims=True), jnp.max(s_ctx, axis=-1, keepdims=True))
        p_loc = jnp.exp2(s_loc - m)
        p_ctx = jnp.exp2(s_ctx - m)
        l = jnp.sum(p_loc, axis=-1, keepdims=True) + jnp.sum(p_ctx, axis=-1, keepdims=True)
        o = (_dot(p_loc.astype(BF16), v_ref[pl.ds(start, nwin), hs])
             + _dot(p_ctx.astype(BF16), vc_ref[0, :, hs].astype(BF16)))
        o_ref[:, hs] = (o / l).astype(BF16)


def _na_bias_table(rpb, rows):
    nh = rpb.shape[0]
    a = np.arange(NA_Q_ROWS)
    b = np.arange(NA_WIN_ROWS)
    col = np.arange(GRID_W)
    cs = np.clip(col - NA_KC // 2, 0, GRID_W - NA_KC)
    valid_col = (col[None, :] >= cs[:, None]) & (col[None, :] < cs[:, None] + NA_KC)
    padw = GRID_W - NA_KC
    rp = jnp.pad(rpb, ((0, 0), (0, 0), (padw, padw)))
    toep = jnp.stack([rp[:, :, GRID_W - 1 - qc:2 * GRID_W - 1 - qc] for qc in range(GRID_W)], axis=2)
    toep = jnp.where(valid_col[None, None], toep, NEG_BIG)
    masked = jnp.full((nh, GRID_W, GRID_W), NEG_BIG, F32)
    tabs = []
    for r0 in (0, NA_Q_ROWS, rows - NA_Q_ROWS):
        ws = int(np.clip(r0 - NA_KR // 2, 0, rows - NA_WIN_ROWS))
        r = r0 + a
        rs = np.clip(r - NA_KR // 2, 0, rows - NA_KR)
        kr = ws + b
        valid_row = (kr[None, :] >= rs[:, None]) & (kr[None, :] < rs[:, None] + NA_KR)
        dr = kr[None, :] - r[:, None] + NA_KR - 1
        tile_rows = []
        for ai in range(NA_Q_ROWS):
            blocks = [toep[:, int(dr[ai, bi])] if valid_row[ai, bi] else masked for bi in range(NA_WIN_ROWS)]
            tile_rows.append(jnp.concatenate(blocks, axis=-1))
        tabs.append(jnp.concatenate(tile_rows, axis=1))
    return jnp.stack(tabs)


def _sample_na(naq, nak, nav, kc, vc, bias, batch, seq):
    rows = seq // GRID_W
    tq = NA_Q_ROWS * GRID_W
    nt = seq // tq
    last = nt - 1
    hw = NA_HEADS_PER_STEP * NA_HEAD_DIM

    def pat(rt):
        return jnp.where(rt == 0, 0, jnp.where(rt == last, 2, 1))

    return pl.pallas_call(
        functools.partial(_na_kernel, rows=rows),
        grid=(batch, NA_HEADS // NA_HEADS_PER_STEP, nt),
        in_specs=[pl.BlockSpec((tq, hw), lambda b, h, r: (b * nt + r, h)),
                  pl.BlockSpec((seq, hw), lambda b, h, r: (b, h)),
                  pl.BlockSpec((seq, hw), lambda b, h, r: (b, h)),
                  pl.BlockSpec((1, kc.shape[1], hw), lambda b, h, r: (b, 0, h)),
                  pl.BlockSpec((1, vc.shape[1], hw), lambda b, h, r: (b, 0, h)),
                  pl.BlockSpec((1, NA_HEADS_PER_STEP, tq, NA_WIN_ROWS * GRID_W), lambda b, h, r: (pat(r), h, 0, 0))],
        out_specs=pl.BlockSpec((tq, hw), lambda b, h, r: (b * nt + r, h)),
        out_shape=jax.ShapeDtypeStruct((batch * seq, NA_WIDTH), BF16),
        compiler_params=_cparams(("arbitrary", "arbitrary", "arbitrary")),
        name="sample_neighbourhood_attention",
    )(naq, nak, nav, kc, vc, bias)


def _kv_expand_kernel(ckv_ref, krp_ref, w_ref, kf_ref, v_ref):
    kv = _dot(ckv_ref[0], w_ref[...])
    krp = krp_ref[0]
    kvw = QK_NOPE + V_DIM
    for hd in range(MLA_HEADS):
        kf_ref[0, hd, :, 0:QK_NOPE] = kv[:, hd * kvw:hd * kvw + QK_NOPE].astype(BF16)
        kf_ref[0, hd, :, QK_NOPE:Q_PAD] = krp
        v_ref[0, hd] = kv[:, hd * kvw + QK_NOPE:(hd + 1) * kvw].astype(BF16)


def _kv_expand(ckv, krp, w_kvb):
    batch, nkeys, _ = ckv.shape
    tm = 512
    return pl.pallas_call(
        _kv_expand_kernel,
        grid=(batch, nkeys // tm),
        in_specs=[pl.BlockSpec((1, tm, KV_LORA), lambda b, t: (b, t, 0)),
                  pl.BlockSpec((1, tm, LANE), lambda b, t: (b, t, 0)),
                  _const_spec(w_kvb.shape)],
        out_specs=[pl.BlockSpec((1, MLA_HEADS, tm, Q_PAD), lambda b, t: (b, 0, t, 0)),
                   pl.BlockSpec((1, MLA_HEADS, tm, V_DIM), lambda b, t: (b, 0, t, 0))],
        out_shape=[jax.ShapeDtypeStruct((batch, MLA_HEADS, nkeys, Q_PAD), BF16),
                   jax.ShapeDtypeStruct((batch, MLA_HEADS, nkeys, V_DIM), BF16)],
        compiler_params=_cparams(("arbitrary", "arbitrary")),
        name="latent_kv_expand",
    )(ckv, krp, w_kvb)


def _mla_kernel(q_ref, kf_ref, v_ref, o_ref, *, nkeys):
    q = q_ref[...]
    tq = q.shape[0]
    m = jnp.full((tq, 1), NEG_BIG, F32)
    l = jnp.zeros((tq, 1), F32)
    acc = jnp.zeros((tq, V_DIM), F32)
    for c in range(nkeys // MLA_TK):
        ks = slice(c * MLA_TK, (c + 1) * MLA_TK)
        s = _dot_nt(q, kf_ref[0, 0, ks, :])
        m_new = jnp.maximum(m, jnp.max(s, axis=-1, keepdims=True))
        alpha = jnp.exp2(m - m_new)
        p = jnp.exp2(s - m_new)
        l = alpha * l + jnp.sum(p, axis=-1, keepdims=True)
        acc = alpha * acc + _dot(p.astype(BF16), v_ref[0, 0, ks, :])
        m = m_new
    o_ref[...] = (acc / l).astype(BF16)


def _sample_mla(q, kf, v, batch, seq):
    nkeys = kf.shape[2]
    nt = seq // MLA_TQ
    return pl.pallas_call(
        functools.partial(_mla_kernel, nkeys=nkeys),
        grid=(batch, MLA_HEADS, nt),
        in_specs=[pl.BlockSpec((MLA_TQ, Q_PAD), lambda b, h, t: (b * nt + t, h)),
                  pl.BlockSpec((1, 1, nkeys, Q_PAD), lambda b, h, t: (b, h, 0, 0)),
                  pl.BlockSpec((1, 1, nkeys, V_DIM), lambda b, h, t: (b, h, 0, 0))],
        out_specs=pl.BlockSpec((MLA_TQ, V_DIM), lambda b, h, t: (b * nt + t, h)),
        out_shape=jax.ShapeDtypeStruct((batch * seq, MLA_WIDTH), BF16),
        compiler_params=_cparams(("arbitrary", "arbitrary", "arbitrary")),
        name="sample_latent_attention",
    )(q, kf, v)


def _post_kernel(xp_ref, xs_ref, onp_ref, omp_ref, ons_ref, oms_ref, mod_ref, wout_ref, gffn_ref, wrh_ref, wrl_ref,
                 br_ref, x1_ref, h2p_ref, idx_ref, gate_ref, *, n_prompt_tiles):
    is_prompt = pl.program_id(0) < n_prompt_tiles
    ga = mod_ref[0, 2:3, :]
    sf = mod_ref[0, 3:4, :]
    scf = mod_ref[0, 4:5, :]
    tm = ROW_TILE
    for r0 in range(0, xp_ref.shape[0], tm):
        rs = slice(r0, r0 + tm)
        x = jnp.where(is_prompt, xp_ref[rs, :], xs_ref[rs, :])
        ona = jnp.where(is_prompt, onp_ref[rs, :], ons_ref[rs, :])
        omla = jnp.where(is_prompt, omp_ref[rs, :], oms_ref[rs, :])
        o = _dot(ona, wout_ref[0:NA_WIDTH, :]) + _dot(omla, wout_ref[NA_WIDTH:NA_WIDTH + MLA_WIDTH, :])
        x1 = x + ga * o
        x1_ref[rs, :] = x1
        h2 = _rms(x1, gffn_ref[...]) * (1.0 + scf) + sf
        h_hi = h2.astype(BF16)
        h_lo = (h2 - h_hi.astype(F32)).astype(BF16)
        logits = _dot(h_hi, wrh_ref[...]) + _dot(h_lo, wrh_ref[...]) + _dot(h_hi, wrl_ref[...]) + br_ref[...]
        lane_e = lax.broadcasted_iota(jnp.int32, logits.shape, 1).astype(F32)
        lane_o = lax.broadcasted_iota(jnp.int32, (tm, LANE), 1)
        idx_out = jnp.zeros((tm, LANE), F32)
        gate_out = jnp.zeros((tm, LANE), F32)
        top0 = None
        denom = jnp.zeros((tm, 1), F32)
        cur = logits
        for k in range(TOP_K):
            mx = jnp.max(cur, axis=-1, keepdims=True)
            ix = jnp.min(jnp.where(cur == mx, lane_e, float(N_EXPERTS)), axis=-1, keepdims=True)
            cur = jnp.where(lane_e == ix, -jnp.inf, cur)
            if k == 0:
                top0 = mx
            e = jnp.exp(mx - top0)
            denom = denom + e
            idx_out = jnp.where(lane_o == k, ix, idx_out)
            gate_out = jnp.where(lane_o == k, e, gate_out)
        idx_ref[rs, :] = idx_out.astype(jnp.int32)
        gate_ref[rs, :] = gate_out / denom
        bits = pltpu.bitcast(h_hi.astype(F32), U32)
        for s in range(PACK_ROWS):
            lo = bits[:, s * LANE:(s + 1) * LANE] >> 16
            hi = bits[:, (s + PACK_ROWS) * LANE:(s + PACK_ROWS + 1) * LANE] & jnp.uint32(0xFFFF0000)
            h2p_ref[pl.ds(r0 * PACK_ROWS + s, tm, stride=PACK_ROWS), :] = hi | lo


def _post_attention(xp, xs, onp, omp, ons, oms, mod, w_out, g_ffn, w_router, b_router, *, mod_row):
    tm = POST_TILE
    wr_hi = w_router.astype(BF16)
    wr_lo = (w_router - wr_hi.astype(F32)).astype(BF16)
    npt = xp.shape[0] // tm
    nst = xs.shape[0] // tm
    n = xp.shape[0] + xs.shape[0]
    pidx = lambda i: (jnp.minimum(i, npt - 1), 0)
    sidx = lambda i: (jnp.maximum(i - npt, 0), 0)
    row = lambda i: (i, 0)
    return pl.pallas_call(
        functools.partial(_post_kernel, n_prompt_tiles=npt),
        grid=(npt + nst,),
        in_specs=[pl.BlockSpec((tm, D_MODEL), pidx), pl.BlockSpec((tm, D_MODEL), sidx),
                  pl.BlockSpec((tm, NA_WIDTH), pidx), pl.BlockSpec((tm, MLA_WIDTH), pidx),
                  pl.BlockSpec((tm, NA_WIDTH), sidx), pl.BlockSpec((tm, MLA_WIDTH), sidx),
                  pl.BlockSpec((1, 6, D_MODEL), lambda i: (mod_row(i), 0, 0)),
                  _const_spec(w_out.shape), _const_spec((1, D_MODEL)),
                  _const_spec(w_router.shape), _const_spec(w_router.shape), _const_spec((1, N_EXPERTS))],
        out_specs=[pl.BlockSpec((tm, D_MODEL), row),
                   pl.BlockSpec((tm * PACK_ROWS, LANE), row),
                   pl.BlockSpec((tm, LANE), row),
                   pl.BlockSpec((tm, LANE), row)],
        out_shape=[jax.ShapeDtypeStruct((n, D_MODEL), F32),
                   jax.ShapeDtypeStruct((n * PACK_ROWS, LANE), U32),
                   jax.ShapeDtypeStruct((n, LANE), jnp.int32),
                   jax.ShapeDtypeStruct((n, LANE), F32)],
        compiler_params=_cparams(("arbitrary",)),
        name="post_attention_router",
    )(xp, xs, onp, omp, ons, oms, mod, w_out, g_ffn, wr_hi, wr_lo, b_router)


def _deinterleave_kernel(w_ref, p_ref, o_ref):
    p = p_ref[...]
    grp = p.shape[0]
    per_tile = o_ref.shape[3] // grp
    for c in range(w_ref.shape[2] // grp):
        w = w_ref[0, :, c * grp:(c + 1) * grp].astype(BF16)
        col = (c % per_tile) * grp
        o_ref[0, c // per_tile, :, col:col + grp] = _dot(w, p).astype(BF16)


def _deinterleave_gate_up(w_gate_up):
    ne, d, n2 = w_gate_up.shape
    nj = n2 // (2 * MOE_TF)
    grp = 2 * LANE
    dst = np.arange(grp)
    src = np.where(dst < LANE, 2 * dst, 2 * (dst - LANE) + 1)
    perm = np.zeros((grp, grp), np.float32)
    perm[src, dst] = 1.0
    tk = 512
    return pl.pallas_call(
        _deinterleave_kernel,
        grid=(ne, d // tk),
        in_specs=[pl.BlockSpec((1, tk, n2), lambda e, k: (e, k, 0)), _const_spec((grp, grp))],
        out_specs=pl.BlockSpec((1, nj, tk, 2 * MOE_TF), lambda e, k: (e, 0, k, 0)),
        out_shape=jax.ShapeDtypeStruct((ne, nj, d, 2 * MOE_TF), BF16),
        compiler_params=_cparams(("arbitrary", "arbitrary")),
        name="deinterleave_gate_up",
    )(w_gate_up, jnp.asarray(perm, BF16))


def _moe_kernel(ce_ref, nv_ref, cs_ref, order_ref, h_ref, wgu_ref, bgu_ref, wd_ref, bd_ref, y_ref,
                xg_ref, xbf_ref, acc_ref, wdb_ref, ost_ref, sem_g, sem_s):
    c = pl.program_id(0)
    j = pl.program_id(1)
    last_c = pl.num_programs(0) - 1
    last_j = pl.num_programs(1) - 1
    nk = order_ref.shape[0]
    nsub = MOE_CHUNK // MOE_SUB
    xg_rows = MOE_CHUNK * PACK_ROWS

    nv = nv_ref[c]
    c_next = jnp.minimum(c + 1, last_c)
    nv_next = jnp.where(c < last_c, nv_ref[c_next], 0)
    cs_next = cs_ref[c_next]
    c_prev = jnp.maximum(c - 1, 0)
    nv_prev = jnp.where(c > 0, nv_ref[c_prev], 0)
    cs_prev = cs_ref[c_prev]
    nv_prev2 = jnp.where(c > 1, nv_ref[jnp.maximum(c - 2, 0)], 0)
    ost_rows = MOE_CHUNK * OUT_ROWS
    slot = c % 2
    slot_next = (c + 1) % 2

    def gather_row(cs, dst_slot, row):
        tok = jnp.right_shift(order_ref[jnp.minimum(cs + row, nk - 1)], TOP_K_SHIFT)
        src = pl.multiple_of(tok * PACK_ROWS, PACK_ROWS)
        dst = pl.multiple_of(dst_slot * xg_rows + row * PACK_ROWS, PACK_ROWS)
        pltpu.make_async_copy(h_ref.at[pl.ds(src, PACK_ROWS)], xg_ref.at[pl.ds(dst, PACK_ROWS)],
                              sem_g.at[dst_slot]).start()

    def scatter_row(row):
        flat = order_ref[jnp.minimum(cs_prev + row, nk - 1)]
        dest = jnp.where(row < nv_prev, flat, nk + slot_next * MOE_CHUNK + row)
        src = pl.multiple_of(slot_next * ost_rows + row * OUT_ROWS, OUT_ROWS)
        dst = pl.multiple_of(dest * OUT_ROWS, OUT_ROWS)
        pltpu.make_async_copy(ost_ref.at[pl.ds(src, OUT_ROWS)], y_ref.at[pl.ds(dst, OUT_ROWS)],
                              sem_s.at[slot_next]).start()

    def rolled(fn, row0, nrows):
        def body(r0, carry):
            for u in range(DMA_UNROLL):
                fn(row0 + r0 * DMA_UNROLL + u)
            return carry

        lax.fori_loop(0, nrows // DMA_UNROLL, body, 0)

    @pl.when(jnp.logical_and(jnp.logical_and(c == 0, j == 0), nv > 0))
    def _():
        rolled(lambda row: gather_row(cs_ref[0], 0, row), 0, MOE_CHUNK)

    def unpack(sb):
        for s in range(PACK_ROWS):
            w = xg_ref[pl.ds(slot * xg_rows + sb * MOE_SUB * PACK_ROWS + s, MOE_SUB, stride=PACK_ROWS), :]
            lo = pltpu.bitcast(w << 16, F32)
            hi = pltpu.bitcast(w & jnp.uint32(0xFFFF0000), F32)
            xbf_ref[sb * MOE_SUB:(sb + 1) * MOE_SUB, s * LANE:(s + 1) * LANE] = lo.astype(BF16)
            xbf_ref[sb * MOE_SUB:(sb + 1) * MOE_SUB,
                    (s + PACK_ROWS) * LANE:(s + PACK_ROWS + 1) * LANE] = hi.astype(BF16)

    @pl.when(jnp.logical_and(j == 0, nv > 0))
    def _():
        base = pl.multiple_of(slot * xg_rows, xg_rows)
        pltpu.make_async_copy(h_ref.at[pl.ds(0, xg_rows)], xg_ref.at[pl.ds(base, xg_rows)], sem_g.at[slot]).wait()

        acc_ref[...] = jnp.broadcast_to(bd_ref[0], (MOE_CHUNK, D_MODEL))

        @pl.when(c == 0)
        def _():
            ost_ref[...] = jnp.zeros(ost_ref.shape, F32)

        @pl.when(nv == MOE_CHUNK)
        def _():
            for sb in range(nsub):
                unpack(sb)

        @pl.when(nv < MOE_CHUNK)
        def _():
            for sb in range(nsub):
                @pl.when(sb * MOE_SUB < nv)
                def _():
                    unpack(sb)

    def sub_block(sb):
        if sb == 0:
            wdb_ref[...] = wd_ref[0].astype(BF16)
        x = xbf_ref[sb * MOE_SUB:(sb + 1) * MOE_SUB, :]
        gu = _dot(x, wgu_ref[0, 0]) + bgu_ref[0]
        ng = MOE_TF // LANE
        g = jnp.concatenate([gu[:, 2 * b * LANE:(2 * b + 1) * LANE] for b in range(ng)], axis=-1)
        u = jnp.concatenate([gu[:, (2 * b + 1) * LANE:(2 * b + 2) * LANE] for b in range(ng)], axis=-1)
        g = jnp.minimum(g, SWIGLU_LIMIT)
        u = jnp.clip(u, -SWIGLU_LIMIT, SWIGLU_LIMIT)
        act = g * (1.0 / (1.0 + jnp.exp(-(g * SWIGLU_ALPHA)))) * (u + 1.0)
        acc_ref[sb * MOE_SUB:(sb + 1) * MOE_SUB, :] += _dot(act.astype(BF16), wdb_ref[...])

    dma_both = jnp.logical_and(nv_next > 0, nv_prev > 0)
    per_sb = MOE_SUB // nsub
    for sb in range(nsub):
        row0 = j * MOE_SUB + sb * per_sb
        compute = sb * MOE_SUB < nv
        fused = jnp.logical_and(compute, dma_both)

        @pl.when(fused)
        def _():
            for r in range(per_sb):
                gather_row(cs_next, slot_next, row0 + r)
                scatter_row(row0 + r)
            sub_block(sb)

        @pl.when(jnp.logical_not(fused))
        def _():
            @pl.when(nv_next > 0)
            def _():
                for r in range(per_sb):
                    gather_row(cs_next, slot_next, row0 + r)

            @pl.when(nv_prev > 0)
            def _():
                for r in range(per_sb):
                    scatter_row(row0 + r)

            @pl.when(compute)
            def _():
                sub_block(sb)

    def wait_scatter(buf):
        base = pl.multiple_of(buf * ost_rows, ost_rows)
        pltpu.make_async_copy(ost_ref.at[pl.ds(base, ost_rows)], y_ref.at[pl.ds(0, ost_rows)], sem_s.at[buf]).wait()

    @pl.when(j == last_j)
    def _():
        @pl.when(nv_prev2 > 0)
        def _():
            wait_scatter(slot)

        for sb in range(nsub):
            @pl.when(sb * MOE_SUB < nv)
            def _():
                for s in range(OUT_ROWS):
                    dst = pl.ds(slot * ost_rows + sb * MOE_SUB * OUT_ROWS + s, MOE_SUB, stride=OUT_ROWS)
                    ost_ref[dst, :] = acc_ref[sb * MOE_SUB:(sb + 1) * MOE_SUB, s * LANE:(s + 1) * LANE]

        @pl.when(jnp.logical_and(c == last_c, nv_prev > 0))
        def _():
            wait_scatter(slot_next)


def _moe(chunk_expert, chunk_rows, chunk_start, order, h2p, wgu, bgu, w_down, b_down):
    nj = D_FF // MOE_TF
    ngrid = chunk_expert.shape[0]
    nk = order.shape[0]

    def jj(c, j, nv):
        return jnp.where(nv[c] > 0, j, nj - 1)

    return pl.pallas_call(
        _moe_kernel,
        grid_spec=pltpu.PrefetchScalarGridSpec(
            num_scalar_prefetch=4,
            grid=(ngrid, nj),
            in_specs=[
                pl.BlockSpec(memory_space=pl.ANY),
                pl.BlockSpec((1, 1, D_MODEL, 2 * MOE_TF), lambda c, j, ce, nv, cs, od: (ce[c], jj(c, j, nv), 0, 0)),
                pl.BlockSpec((1, 1, 2 * MOE_TF), lambda c, j, ce, nv, cs, od: (ce[c], 0, jj(c, j, nv))),
                pl.BlockSpec((1, MOE_TF, D_MODEL), lambda c, j, ce, nv, cs, od: (ce[c], jj(c, j, nv), 0)),
                pl.BlockSpec((1, 1, D_MODEL), lambda c, j, ce, nv, cs, od: (ce[c], 0, 0))],
            out_specs=pl.BlockSpec(memory_space=pl.ANY),
            scratch_shapes=[pltpu.VMEM((2 * MOE_CHUNK * PACK_ROWS, LANE), U32),
                            pltpu.VMEM((MOE_CHUNK, D_MODEL), BF16),
                            pltpu.VMEM((MOE_CHUNK, D_MODEL), F32),
                            pltpu.VMEM((MOE_TF, D_MODEL), BF16),
                            pltpu.VMEM((2 * MOE_CHUNK * OUT_ROWS, LANE), F32),
                            pltpu.SemaphoreType.DMA((2,)),
                            pltpu.SemaphoreType.DMA((2,))]),
        out_shape=jax.ShapeDtypeStruct(((nk + 2 * MOE_CHUNK) * OUT_ROWS, LANE), F32),
        compiler_params=_cparams(("arbitrary", "arbitrary")),
        name="moe_experts",
    )(chunk_expert, chunk_rows, chunk_start, order, h2p, wgu, bgu, w_down, b_down)


def _combine_kernel(gate_ref, x1_ref, mod_ref, gfin_ref, ye_ref, y_ref):
    gate = gate_ref[...]
    pieces = []
    for s in range(OUT_ROWS):
        acc = None
        for k in range(TOP_K):
            rows = ye_ref[pl.ds(k * OUT_ROWS + s, COMBINE_TOKENS, stride=TOP_K * OUT_ROWS), :]
            term = gate[:, k:k + 1] * rows
            acc = term if acc is None else acc + term
        pieces.append(acc)
    y = jnp.concatenate(pieces, axis=-1)
    gf = mod_ref[0, 5:6, :]
    y_ref[...] = _rms(x1_ref[...] + gf * y, gfin_ref[...])


def _combine(gates, x1, mod, g_final, y_experts, *, tile0, ntiles, mod_row):
    tt = COMBINE_TOKENS
    return pl.pallas_call(
        _combine_kernel,
        grid=(ntiles,),
        in_specs=[pl.BlockSpec((tt, LANE), lambda i: (tile0 + i, 0)),
                  pl.BlockSpec((tt, D_MODEL), lambda i: (tile0 + i, 0)),
                  pl.BlockSpec((1, 6, D_MODEL), lambda i: (mod_row(i), 0, 0)),
                  _const_spec((1, D_MODEL)),
                  pl.BlockSpec((tt * TOP_K * OUT_ROWS, LANE), lambda i: (tile0 + i, 0))],
        out_specs=pl.BlockSpec((tt, D_MODEL), lambda i: (i, 0)),
        out_shape=jax.ShapeDtypeStruct((ntiles * tt, D_MODEL), F32),
        compiler_params=_cparams(("arbitrary",)),
        name="moe_combine_final_norm",
    )(gates, x1, mod, g_final, y_experts)


def _routing(top_idx, nch):
    n = top_idx.shape[0]
    nk = n * TOP_K
    flat_e = top_idx.reshape(nk)
    order = jnp.argsort(flat_e, stable=True).astype(jnp.int32)
    onehot = flat_e[:, None] == jnp.arange(N_EXPERTS, dtype=jnp.int32)[None, :]
    counts = jnp.sum(onehot.astype(jnp.int32), axis=0)
    grp_start = jnp.cumsum(counts) - counts
    chunks_e = (counts + MOE_CHUNK - 1) // MOE_CHUNK
    chunk_end = jnp.cumsum(chunks_e)
    chunk_start = chunk_end - chunks_e

    total = chunk_end[-1]
    cidx = jnp.arange(nch + 1, dtype=jnp.int32)
    active = cidx < total
    ce = jnp.minimum(jnp.searchsorted(chunk_end, cidx, side='right'), N_EXPERTS - 1).astype(jnp.int32)
    local = (cidx - chunk_start[ce]) * MOE_CHUNK
    nv = jnp.where(active, jnp.clip(counts[ce] - local, 0, MOE_CHUNK), 0).astype(jnp.int32)
    last = jnp.maximum(total - 1, 0)
    chunk_expert = jnp.where(active, ce, ce[last]).astype(jnp.int32)
    chunk_first = jnp.where(active, jnp.clip(grp_start[ce] + local, 0, nk - 1), 0).astype(jnp.int32)
    return order, chunk_expert, nv, chunk_first


def _rope_tables(t):
    pos = jnp.arange(t)
    rows = (pos // GRID_W).astype(F32)
    cols = (pos % GRID_W).astype(F32)
    inv = ROPE_THETA ** (-(jnp.arange(ROPE_AXIS // 2, dtype=F32) * 2.0 / ROPE_AXIS))
    ar = rows[:, None] * inv
    ac = cols[:, None] * inv
    ang = jnp.concatenate([ar, ar, ac, ac], axis=-1)
    return jnp.cos(ang), jnp.sin(ang)


def _rot_cols(w):
    half = ROPE_AXIS // 2
    src = np.concatenate([np.arange(half, ROPE_AXIS), np.arange(0, half),
                          np.arange(ROPE_AXIS + half, 2 * ROPE_AXIS), np.arange(ROPE_AXIS, ROPE_AXIS + half)])
    sign = np.concatenate([-np.ones(half), np.ones(half), -np.ones(half), np.ones(half)]).astype(np.float32)
    return w[..., src] * sign


def kernel(x_prompt, x_sample, cache_na_k, cache_na_v, cache_mla_ckv, cache_mla_krope, c, c_ctx, g_attn, g_ffn, g_final, w_mod, b_mod, w_in, w_out, na_rpb, g_q_a, w_q_b, g_kv_a, w_kv_b, w_router, b_router, w_gate_up, b_gate_up, w_down, b_down):
    bp, sp, d = x_prompt.shape
    bd, td, _ = x_sample.shape
    assert d == D_MODEL and w_mod.shape[0] == 1, "one trunk layer of width D_MODEL"
    n_p = bp * sp
    n_s = bd * td
    xp = x_prompt.reshape(n_p, d)
    xs = x_sample.reshape(n_s, d)

    c8 = jnp.zeros((8, d), F32).at[0].set(c_ctx).at[1:1 + bd].set(c)
    mod = _modulation(c8, w_mod[0], b_mod[0].reshape(1, -1)).reshape(8, 6, d)

    w_in0 = w_in[0]
    w_kr = w_in0[:, KR_OFF:KR_OFF + QK_ROPE]
    w_in_p = jnp.concatenate([w_in0, jnp.zeros((d, LANE - QK_ROPE), F32)], axis=1).astype(BF16)
    w_in_s = jnp.concatenate([w_in0, _rot_cols(w_kr)], axis=1).astype(BF16)
    wq = w_q_b[0].reshape(Q_LORA, MLA_HEADS, QK_NOPE + QK_ROPE)
    zpad = jnp.zeros((Q_LORA, MLA_HEADS, Q_PAD - QK_NOPE - QK_ROPE), F32)
    wq_pad = jnp.concatenate([wq, zpad], axis=-1).reshape(Q_LORA, MLA_HEADS * Q_PAD)
    wq_rot = jnp.concatenate([jnp.zeros((Q_LORA, MLA_HEADS, QK_NOPE), F32), _rot_cols(wq[..., QK_NOPE:]), zpad],
                             axis=-1).reshape(Q_LORA, MLA_HEADS * Q_PAD)
    wqb_p = wq_pad.astype(BF16)
    wqb_s = jnp.concatenate([wq_pad, wq_rot], axis=1).astype(BF16)
    w_kvb = w_kv_b[0].astype(BF16)
    cos, sin = _rope_tables(td)
    cosq = jnp.concatenate([jnp.ones((td, QK_NOPE), F32), cos, jnp.ones((td, Q_PAD - QK_NOPE - QK_ROPE), F32)], axis=1)
    sinq = jnp.concatenate([jnp.zeros((td, QK_NOPE), F32), sin, jnp.zeros((td, Q_PAD - QK_NOPE - QK_ROPE), F32)], axis=1)
    cosq = cosq * MLA_QSCALE
    sinq = sinq * MLA_QSCALE
    csk = jnp.concatenate([cos, sin], axis=1)

    tiles_per_seq = td // ROW_TILE
    g_attn2 = g_attn[0].reshape(1, d)
    gq2 = g_q_a[0].reshape(1, Q_LORA)
    gkv2 = g_kv_a[0].reshape(1, KV_LORA)

    naq_p, nak_p, nav_p, q_p, ckv_p, kr_p, krp_p = _pre_attention(
        xp, mod, g_attn2, w_in_p, gq2, wqb_p, gkv2, None, rope=False, mod_row=lambda i: 0)
    ona_p, omla_p = _prompt_attention(naq_p, nak_p, nav_p, q_p, ckv_p, krp_p, w_kvb, sp)

    naq_s, nak_s, nav_s, q_s, ckv_s, krp_s = _pre_attention(
        xs, mod, g_attn2, w_in_s, gq2, wqb_s, gkv2, (cosq, sinq, csk), rope=True,
        mod_row=lambda i: 1 + i // tiles_per_seq)
    past = cache_na_k.shape[2]
    kc = cache_na_k[:, 0].reshape(bd, past, NA_WIDTH)
    vc = cache_na_v[:, 0].reshape(bd, past, NA_WIDTH)
    bias = _na_bias_table(na_rpb[0] * LOG2E, td // GRID_W)
    ona_s = _sample_na(naq_s, nak_s, nav_s, kc, vc, bias, bd, td)
    ckv_all = jnp.concatenate([ckv_s.reshape(bd, td, KV_LORA), cache_mla_ckv[:, 0].astype(BF16)], axis=1)
    krp_c = jnp.concatenate([cache_mla_krope[:, 0], jnp.zeros((bd, past, LANE - QK_ROPE), F32)], axis=-1).astype(BF16)
    krp_all = jnp.concatenate([krp_s.reshape(bd, td, LANE), krp_c], axis=1)
    kf, vv = _kv_expand(ckv_all, krp_all, w_kvb)
    omla_s = _sample_mla(q_s, kf, vv, bd, td)

    npt = n_p // POST_TILE
    post_tiles_per_seq = td // POST_TILE
    x1, h2p, idx128, gate128 = _post_attention(
        xp, xs, ona_p, omla_p, ona_s, omla_s, mod, w_out[0].astype(BF16), g_ffn[0].reshape(1, d),
        w_router[0], b_router[0].reshape(1, N_EXPERTS),
        mod_row=lambda i: jnp.where(i < npt, 0, 1 + jnp.maximum(i - npt, 0) // post_tiles_per_seq))

    n = n_p + n_s
    nch = n * TOP_K // MOE_CHUNK + N_EXPERTS
    order, chunk_expert, chunk_rows, chunk_first = _routing(idx128[:, :TOP_K], nch)
    wgu = _deinterleave_gate_up(w_gate_up[0])
    ng = 2 * D_FF // (2 * LANE)
    bgu = b_gate_up[0].reshape(N_EXPERTS, ng, LANE, 2).transpose(0, 1, 3, 2).reshape(N_EXPERTS, 1, 2 * D_FF)
    y_experts = _moe(chunk_expert, chunk_rows, chunk_first, order, h2p, wgu, bgu, w_down[0],
                     b_down[0].reshape(N_EXPERTS, 1, d))

    tt = COMBINE_TOKENS
    gfin = g_final.reshape(1, d)
    ctiles_seq = td // tt
    y_p = _combine(gate128, x1, mod, gfin, y_experts, tile0=0, ntiles=n_p // tt, mod_row=lambda i: 0)
    y_s = _combine(gate128, x1, mod, gfin, y_experts, tile0=n_p // tt, ntiles=n_s // tt,
                   mod_row=lambda i: 1 + i // ctiles_seq)

    return (y_p.reshape(bp, sp, d), y_s.reshape(bd, td, d),
            nak_p.reshape(bp, 1, sp, NA_HEADS, NA_HEAD_DIM), nav_p.reshape(bp, 1, sp, NA_HEADS, NA_HEAD_DIM),
            ckv_p.reshape(bp, 1, sp, KV_LORA), kr_p.reshape(bp, 1, sp, QK_ROPE))
```

```python
import functools

import numpy as np
import jax
import jax.numpy as jnp
from jax import lax
from jax.experimental import pallas as pl
from jax.experimental.pallas import tpu as pltpu

F32 = jnp.float32
BF16 = jnp.bfloat16
U32 = jnp.uint32

D_MODEL = 2048
GRID_W = 64
NA_HEADS = 8
NA_HEAD_DIM = 128
NA_KR = 8
NA_KC = 16
MLA_HEADS = 8
Q_LORA = 512
KV_LORA = 256
QK_NOPE = 128
QK_ROPE = 64
V_DIM = 128
ROPE_AXIS = QK_ROPE // 2
ROPE_THETA = 10000.0
NA_WIDTH = NA_HEADS * NA_HEAD_DIM
MLA_WIDTH = MLA_HEADS * V_DIM
IN_COLS = 3 * NA_WIDTH + Q_LORA + KV_LORA + QK_ROPE
N_EXPERTS = 32
TOP_K = 4
D_FF = D_MODEL
SWIGLU_ALPHA = 1.702
SWIGLU_LIMIT = 7.0
EPS = 1e-6

LANE = 128
Q_PAD = 2 * LANE
KR_OFF = 3 * NA_WIDTH + Q_LORA + KV_LORA
IN_COLS_PAD = KR_OFF + LANE
VMEM_LIMIT = 56 * 1024 * 1024
NEG_BIG = -1e30

LOG2E = 1.4426950408889634
NA_QSCALE = NA_HEAD_DIM ** -0.5 * LOG2E
MLA_QSCALE = (QK_NOPE + QK_ROPE) ** -0.5 * LOG2E

ROW_TILE = 256
POST_TILE = 2 * ROW_TILE
NA_Q_ROWS = 4
NA_WIN_ROWS = 12
NA_HEADS_PER_STEP = 4
MLA_TQ = 512
MLA_TK = 4608
MOE_CHUNK = 1024
MOE_SUB = 256
MOE_TF = 512
COMBINE_TOKENS = 256
TOP_K_SHIFT = TOP_K.bit_length() - 1
DMA_UNROLL = 8
PACK_ROWS = D_MODEL // (2 * LANE)
OUT_ROWS = D_MODEL // LANE


def _cparams(sem):
    return pltpu.CompilerParams(dimension_semantics=sem, vmem_limit_bytes=VMEM_LIMIT)


def _const_spec(shape):
    nd = len(shape)
    return pl.BlockSpec(shape, lambda *a: (0,) * nd, pipeline_mode=pl.Buffered(1))


def _rms(x, g):
    return x * lax.rsqrt(jnp.mean(x * x, axis=-1, keepdims=True) + EPS) * g


def _dot(a, b):
    return jnp.dot(a, b, preferred_element_type=F32)


def _dot_nt(a, b):
    return lax.dot_general(a, b, (((1,), (1,)), ((), ())), preferred_element_type=F32)


def _mod_kernel(c_ref, w_ref, b_ref, o_ref):
    c = c_ref[...]
    s = c / (1.0 + jnp.exp(-c))
    o_ref[...] = _dot(s.astype(BF16), w_ref[...].astype(BF16)) + b_ref[...]


def _modulation(c8, w_mod, b_mod):
    n = w_mod.shape[1]
    tn = 1024
    return pl.pallas_call(
        _mod_kernel,
        grid=(n // tn,),
        in_specs=[pl.BlockSpec((8, D_MODEL), lambda j: (0, 0)),
                  pl.BlockSpec((D_MODEL, tn), lambda j: (0, j)),
                  pl.BlockSpec((1, tn), lambda j: (0, j))],
        out_specs=pl.BlockSpec((8, tn), lambda j: (0, j)),
        out_shape=jax.ShapeDtypeStruct((8, n), F32),
        compiler_params=_cparams(("arbitrary",)),
        name="modulation",
    )(c8, w_mod, b_mod)


def _pre_kernel(*refs, rope):
    if rope:
        (x_ref, mod_ref, g_ref, win_ref, gq_ref, wqb_ref, gkv_ref, cosq_ref, sinq_ref, csk_ref,
         naq_ref, nak_ref, nav_ref, q_ref, ckv_ref, krp_ref) = refs
    else:
        (x_ref, mod_ref, g_ref, win_ref, gq_ref, wqb_ref, gkv_ref,
         naq_ref, nak_ref, nav_ref, q_ref, ckv_ref, kr_ref, krp_ref) = refs
    x = x_ref[...]
    sa = mod_ref[0, 0:1, :]
    sca = mod_ref[0, 1:2, :]
    h = (_rms(x, g_ref[...]) * (1.0 + sca) + sa).astype(BF16)
    proj = _dot(h, win_ref[...])
    naq_ref[...] = (proj[:, 0:NA_WIDTH] * NA_QSCALE).astype(naq_ref.dtype)
    nak_ref[...] = proj[:, NA_WIDTH:2 * NA_WIDTH].astype(nak_ref.dtype)
    nav_ref[...] = proj[:, 2 * NA_WIDTH:3 * NA_WIDTH].astype(nav_ref.dtype)
    q_a = proj[:, 3 * NA_WIDTH:3 * NA_WIDTH + Q_LORA]
    kv_a = proj[:, 3 * NA_WIDTH + Q_LORA:KR_OFF]
    krx = proj[:, KR_OFF:IN_COLS_PAD]
    qan = _rms(q_a, gq_ref[...]).astype(BF16)
    qq = _dot(qan, wqb_ref[...])
    ckv_ref[...] = _rms(kv_a, gkv_ref[...]).astype(ckv_ref.dtype)
    if rope:
        width = MLA_HEADS * Q_PAD
        cosq = cosq_ref[...]
        sinq = sinq_ref[...]
        for hd in range(MLA_HEADS):
            a = qq[:, hd * Q_PAD:(hd + 1) * Q_PAD]
            b = qq[:, width + hd * Q_PAD:width + (hd + 1) * Q_PAD]
            q_ref[:, hd * Q_PAD:(hd + 1) * Q_PAD] = (a * cosq + b * sinq).astype(BF16)
        y = krx * csk_ref[...]
        y = y + pltpu.roll(y, QK_ROPE, 1)
        lane = lax.broadcasted_iota(jnp.int32, y.shape, 1)
        krp_ref[...] = jnp.where(lane < QK_ROPE, y, 0.0).astype(BF16)
    else:
        q_ref[...] = (qq * MLA_QSCALE).astype(BF16)
        kr_ref[...] = krx[:, 0:QK_ROPE]
        krp_ref[...] = krx.astype(BF16)


def _pre_attention(x, mod, g_attn, w_in, g_q_a, w_qb, g_kv_a, rope_tabs, *, rope, mod_row):
    n = x.shape[0]
    tm = ROW_TILE
    row = lambda i: (i, 0)
    in_specs = [pl.BlockSpec((tm, D_MODEL), row),
                pl.BlockSpec((1, 6, D_MODEL), lambda i: (mod_row(i), 0, 0)),
                _const_spec((1, D_MODEL)),
                _const_spec(w_in.shape),
                _const_spec((1, Q_LORA)),
                _const_spec(w_qb.shape),
                _const_spec((1, KV_LORA))]
    args = [x, mod, g_attn, w_in, g_q_a, w_qb, g_kv_a]
    qw = MLA_HEADS * Q_PAD
    if rope:
        tiles_per_seq = rope_tabs[0].shape[0] // tm
        pos = lambda i: (i % tiles_per_seq, 0)
        in_specs += [pl.BlockSpec((tm, Q_PAD), pos), pl.BlockSpec((tm, Q_PAD), pos),
                     pl.BlockSpec((tm, LANE), pos)]
        args += list(rope_tabs)
        out_shape = [jax.ShapeDtypeStruct((n, NA_WIDTH), BF16)] * 3 + [
            jax.ShapeDtypeStruct((n, qw), BF16),
            jax.ShapeDtypeStruct((n, KV_LORA), BF16),
            jax.ShapeDtypeStruct((n, LANE), BF16)]
        out_specs = [pl.BlockSpec((tm, NA_WIDTH), row)] * 3 + [
            pl.BlockSpec((tm, qw), row), pl.BlockSpec((tm, KV_LORA), row), pl.BlockSpec((tm, LANE), row)]
    else:
        out_shape = [jax.ShapeDtypeStruct((n, NA_WIDTH), BF16),
                     jax.ShapeDtypeStruct((n, NA_WIDTH), F32),
                     jax.ShapeDtypeStruct((n, NA_WIDTH), F32),
                     jax.ShapeDtypeStruct((n, qw), BF16),
                     jax.ShapeDtypeStruct((n, KV_LORA), F32),
                     jax.ShapeDtypeStruct((n, QK_ROPE), F32),
                     jax.ShapeDtypeStruct((n, LANE), BF16)]
        out_specs = [pl.BlockSpec((tm, NA_WIDTH), row)] * 3 + [
            pl.BlockSpec((tm, qw), row), pl.BlockSpec((tm, KV_LORA), row),
            pl.BlockSpec((tm, QK_ROPE), row), pl.BlockSpec((tm, LANE), row)]
    return pl.pallas_call(
        functools.partial(_pre_kernel, rope=rope),
        grid=(n // tm,),
        in_specs=in_specs,
        out_specs=out_specs,
        out_shape=out_shape,
        compiler_params=_cparams(("arbitrary",)),
        name="pre_attention_rope" if rope else "pre_attention",
    )(*args)


def _softmax_pv(s, v):
    m = jnp.max(s, axis=-1, keepdims=True)
    p = jnp.exp2(s - m)
    l = jnp.sum(p, axis=-1, keepdims=True)
    return _dot(p.astype(BF16), v) / l


def _prompt_attn_kernel(naq_ref, nak_ref, nav_ref, q_ref, ckv_ref, krp_ref, wkvb_ref, ona_ref, omla_ref):
    kv = _dot(ckv_ref[...].astype(BF16), wkvb_ref[...])
    krp = krp_ref[...]
    for hd in range(NA_HEADS):
        sl = slice(hd * NA_HEAD_DIM, (hd + 1) * NA_HEAD_DIM)
        s = _dot_nt(naq_ref[:, sl], nak_ref[:, sl].astype(BF16))
        ona_ref[:, sl] = _softmax_pv(s, nav_ref[:, sl].astype(BF16)).astype(BF16)
    kvw = QK_NOPE + V_DIM
    for hd in range(MLA_HEADS):
        kf = jnp.concatenate([kv[:, hd * kvw:hd * kvw + QK_NOPE].astype(BF16), krp], axis=-1)
        s = _dot_nt(q_ref[:, hd * Q_PAD:(hd + 1) * Q_PAD], kf)
        v = kv[:, hd * kvw + QK_NOPE:(hd + 1) * kvw].astype(BF16)
        omla_ref[:, hd * V_DIM:(hd + 1) * V_DIM] = _softmax_pv(s, v).astype(BF16)


def _prompt_attention(naq, nak, nav, q, ckv, krp, w_kvb, seq):
    n = naq.shape[0]
    row = lambda b: (b, 0)
    return pl.pallas_call(
        _prompt_attn_kernel,
        grid=(n // seq,),
        in_specs=[pl.BlockSpec((seq, NA_WIDTH), row)] * 3 + [
            pl.BlockSpec((seq, MLA_HEADS * Q_PAD), row),
            pl.BlockSpec((seq, KV_LORA), row),
            pl.BlockSpec((seq, LANE), row),
            _const_spec(w_kvb.shape)],
        out_specs=[pl.BlockSpec((seq, NA_WIDTH), row), pl.BlockSpec((seq, MLA_WIDTH), row)],
        out_shape=[jax.ShapeDtypeStruct((n, NA_WIDTH), BF16), jax.ShapeDtypeStruct((n, MLA_WIDTH), BF16)],
        compiler_params=_cparams(("arbitrary",)),
        name="prompt_attention",
    )(naq, nak, nav, q, ckv, krp, w_kvb)


def _na_kernel(q_ref, k_ref, v_ref, kc_ref, vc_ref, bias_ref, o_ref, *, rows):
    rt = pl.program_id(2)
    ws = jnp.clip(rt * NA_Q_ROWS - NA_KR // 2, 0, rows - NA_WIN_ROWS)
    start = pl.multiple_of(ws * GRID_W, GRID_W)
    nwin = NA_WIN_ROWS * GRID_W
    for hh in range(NA_HEADS_PER_STEP):
        hs = slice(hh * NA_HEAD_DIM, (hh + 1) * NA_HEAD_DIM)
        q = q_ref[:, hs]
        s_loc = _dot_nt(q, k_ref[pl.ds(start, nwin), hs]) + bias_ref[0, hh]
        s_ctx = _dot_nt(q, kc_ref[0, :, hs].astype(BF16))
        m = jnp.maximum(jnp.max(s_loc, axis=-1, keepdims=True), jnp.max(s_ctx, axis=-1, keepdims=True))
        p_loc = jnp.exp2(s_loc - m)
        p_ctx = jnp.exp2(s_ctx - m)
        l = jnp.sum(p_loc, axis=-1, keepdims=True) + jnp.sum(p_ctx, axis=-1, keepdims=True)
        o = (_dot(p_loc.astype(BF16), v_ref[pl.ds(start, nwin), hs])
             + _dot(p_ctx.astype(BF16), vc_ref[0, :, hs].astype(BF16)))
        o_ref[:, hs] = (o / l).astype(BF16)


def _na_bias_table(rpb, rows):
    nh = rpb.shape[0]
    a = np.arange(NA_Q_ROWS)
    b = np.arange(NA_WIN_ROWS)
    col = np.arange(GRID_W)
    cs = np.clip(col - NA_KC // 2, 0, GRID_W - NA_KC)
    valid_col = (col[None, :] >= cs[:, None]) & (col[None, :] < cs[:, None] + NA_KC)
    padw = GRID_W - NA_KC
    rp = jnp.pad(rpb, ((0, 0), (0, 0), (padw, padw)))
    toep = jnp.stack([rp[:, :, GRID_W - 1 - qc:2 * GRID_W - 1 - qc] for qc in range(GRID_W)], axis=2)
    toep = jnp.where(valid_col[None, None], toep, NEG_BIG)
    masked = jnp.full((nh, GRID_W, GRID_W), NEG_BIG, F32)
    tabs = []
    for r0 in (0, NA_Q_ROWS, rows - NA_Q_ROWS):
        ws = int(np.clip(r0 - NA_KR // 2, 0, rows - NA_WIN_ROWS))
        r = r0 + a
        rs = np.clip(r - NA_KR // 2, 0, rows - NA_KR)
        kr = ws + b
        valid_row = (kr[None, :] >= rs[:, None]) & (kr[None, :] < rs[:, None] + NA_KR)
        dr = kr[None, :] - r[:, None] + NA_KR - 1
        tile_rows = []
        for ai in range(NA_Q_ROWS):
            blocks = [toep[:, int(dr[ai, bi])] if valid_row[ai, bi] else masked for bi in range(NA_WIN_ROWS)]
            tile_rows.append(jnp.concatenate(blocks, axis=-1))
        tabs.append(jnp.concatenate(tile_rows, axis=1))
    return jnp.stack(tabs)


def _sample_na(naq, nak, nav, kc, vc, bias, batch, seq):
    rows = seq // GRID_W
    tq = NA_Q_ROWS * GRID_W
    nt = seq // tq
    last = nt - 1
    hw = NA_HEADS_PER_STEP * NA_HEAD_DIM

    def pat(rt):
        return jnp.where(rt == 0, 0, jnp.where(rt == last, 2, 1))

    return pl.pallas_call(
        functools.partial(_na_kernel, rows=rows),
        grid=(batch, NA_HEADS // NA_HEADS_PER_STEP, nt),
        in_specs=[pl.BlockSpec((tq, hw), lambda b, h, r: (b * nt + r, h)),
                  pl.BlockSpec((seq, hw), lambda b, h, r: (b, h)),
                  pl.BlockSpec((seq, hw), lambda b, h, r: (b, h)),
                  pl.BlockSpec((1, kc.shape[1], hw), lambda b, h, r: (b, 0, h)),
                  pl.BlockSpec((1, vc.shape[1], hw), lambda b, h, r: (b, 0, h)),
                  pl.BlockSpec((1, NA_HEADS_PER_STEP, tq, NA_WIN_ROWS * GRID_W), lambda b, h, r: (pat(r), h, 0, 0))],
        out_specs=pl.BlockSpec((tq, hw), lambda b, h, r: (b * nt + r, h)),
        out_shape=jax.ShapeDtypeStruct((batch * seq, NA_WIDTH), BF16),
        compiler_params=_cparams(("arbitrary", "arbitrary", "arbitrary")),
        name="sample_neighbourhood_attention",
    )(naq, nak, nav, kc, vc, bias)


def _kv_expand_kernel(ckv_ref, krp_ref, w_ref, kf_ref, v_ref):
    kv = _dot(ckv_ref[0], w_ref[...])
    krp = krp_ref[0]
    kvw = QK_NOPE + V_DIM
    for hd in range(MLA_HEADS):
        kf_ref[0, hd, :, 0:QK_NOPE] = kv[:, hd * kvw:hd * kvw + QK_NOPE].astype(BF16)
        kf_ref[0, hd, :, QK_NOPE:Q_PAD] = krp
        v_ref[0, hd] = kv[:, hd * kvw + QK_NOPE:(hd + 1) * kvw].astype(BF16)


def _kv_expand(ckv, krp, w_kvb):
    batch, nkeys, _ = ckv.shape
    tm = 512
    return pl.pallas_call(
        _kv_expand_kernel,
        grid=(batch, nkeys // tm),
        in_specs=[pl.BlockSpec((1, tm, KV_LORA), lambda b, t: (b, t, 0)),
                  pl.BlockSpec((1, tm, LANE), lambda b, t: (b, t, 0)),
                  _const_spec(w_kvb.shape)],
        out_specs=[pl.BlockSpec((1, MLA_HEADS, tm, Q_PAD), lambda b, t: (b, 0, t, 0)),
                   pl.BlockSpec((1, MLA_HEADS, tm, V_DIM), lambda b, t: (b, 0, t, 0))],
        out_shape=[jax.ShapeDtypeStruct((batch, MLA_HEADS, nkeys, Q_PAD), BF16),
                   jax.ShapeDtypeStruct((batch, MLA_HEADS, nkeys, V_DIM), BF16)],
        compiler_params=_cparams(("arbitrary", "arbitrary")),
        name="latent_kv_expand",
    )(ckv, krp, w_kvb)


def _mla_kernel(q_ref, kf_ref, v_ref, o_ref, *, nkeys):
    q = q_ref[...]
    tq = q.shape[0]
    m = jnp.full((tq, 1), NEG_BIG, F32)
    l = jnp.zeros((tq, 1), F32)
    acc = jnp.zeros((tq, V_DIM), F32)
    for c in range(nkeys // MLA_TK):
        ks = slice(c * MLA_TK, (c + 1) * MLA_TK)
        s = _dot_nt(q, kf_ref[0, 0, ks, :])
        m_new = jnp.maximum(m, jnp.max(s, axis=-1, keepdims=True))
        alpha = jnp.exp2(m - m_new)
        p = jnp.exp2(s - m_new)
        l = alpha * l + jnp.sum(p, axis=-1, keepdims=True)
        acc = alpha * acc + _dot(p.astype(BF16), v_ref[0, 0, ks, :])
        m = m_new
    o_ref[...] = (acc / l).astype(BF16)


def _sample_mla(q, kf, v, batch, seq):
    nkeys = kf.shape[2]
    nt = seq // MLA_TQ
    assert nkeys % MLA_TK == 0, "latent key count must be a whole number of softmax passes"
    return pl.pallas_call(
        functools.partial(_mla_kernel, nkeys=nkeys),
        grid=(batch, MLA_HEADS, nt),
        in_specs=[pl.BlockSpec((MLA_TQ, Q_PAD), lambda b, h, t: (b * nt + t, h)),
                  pl.BlockSpec((1, 1, nkeys, Q_PAD), lambda b, h, t: (b, h, 0, 0)),
                  pl.BlockSpec((1, 1, nkeys, V_DIM), lambda b, h, t: (b, h, 0, 0))],
        out_specs=pl.BlockSpec((MLA_TQ, V_DIM), lambda b, h, t: (b * nt + t, h)),
        out_shape=jax.ShapeDtypeStruct((batch * seq, MLA_WIDTH), BF16),
        compiler_params=_cparams(("arbitrary", "arbitrary", "arbitrary")),
        name="sample_latent_attention",
    )(q, kf, v)


def _post_kernel(xp_ref, xs_ref, onp_ref, omp_ref, ons_ref, oms_ref, mod_ref, wout_ref, gffn_ref, wrh_ref, wrl_ref,
                 br_ref, x1_ref, h2p_ref, idx_ref, gate_ref, *, n_prompt_tiles):
    is_prompt = pl.program_id(0) < n_prompt_tiles
    ga = mod_ref[0, 2:3, :]
    sf = mod_ref[0, 3:4, :]
    scf = mod_ref[0, 4:5, :]
    tm = ROW_TILE
    for r0 in range(0, xp_ref.shape[0], tm):
        rs = slice(r0, r0 + tm)
        x = jnp.where(is_prompt, xp_ref[rs, :], xs_ref[rs, :])
        ona = jnp.where(is_prompt, onp_ref[rs, :], ons_ref[rs, :])
        omla = jnp.where(is_prompt, omp_ref[rs, :], oms_ref[rs, :])
        o = _dot(ona, wout_ref[0:NA_WIDTH, :]) + _dot(omla, wout_ref[NA_WIDTH:NA_WIDTH + MLA_WIDTH, :])
        x1 = x + ga * o
        x1_ref[rs, :] = x1
        h2 = _rms(x1, gffn_ref[...]) * (1.0 + scf) + sf
        h_hi = h2.astype(BF16)
        h_lo = (h2 - h_hi.astype(F32)).astype(BF16)
        logits = _dot(h_hi, wrh_ref[...]) + _dot(h_lo, wrh_ref[...]) + _dot(h_hi, wrl_ref[...]) + br_ref[...]
        lane_e = lax.broadcasted_iota(jnp.int32, logits.shape, 1).astype(F32)
        lane_o = lax.broadcasted_iota(jnp.int32, (tm, LANE), 1)
        idx_out = jnp.zeros((tm, LANE), F32)
        gate_out = jnp.zeros((tm, LANE), F32)
        top0 = None
        denom = jnp.zeros((tm, 1), F32)
        cur = logits
        for k in range(TOP_K):
            mx = jnp.max(cur, axis=-1, keepdims=True)
            ix = jnp.min(jnp.where(cur == mx, lane_e, float(N_EXPERTS)), axis=-1, keepdims=True)
            cur = jnp.where(lane_e == ix, -jnp.inf, cur)
            if k == 0:
                top0 = mx
            e = jnp.exp(mx - top0)
            denom = denom + e
            idx_out = jnp.where(lane_o == k, ix, idx_out)
            gate_out = jnp.where(lane_o == k, e, gate_out)
        idx_ref[rs, :] = idx_out.astype(jnp.int32)
        gate_ref[rs, :] = gate_out / denom
        bits = pltpu.bitcast(h_hi.astype(F32), U32)
        for s in range(PACK_ROWS):
            lo = bits[:, s * LANE:(s + 1) * LANE] >> 16
            hi = bits[:, (s + PACK_ROWS) * LANE:(s + PACK_ROWS + 1) * LANE] & jnp.uint32(0xFFFF0000)
            h2p_ref[pl.ds(r0 * PACK_ROWS + s, tm, stride=PACK_ROWS), :] = hi | lo


def _post_attention(xp, xs, onp, omp, ons, oms, mod, w_out, g_ffn, w_router, b_router, *, mod_row):
    tm = POST_TILE
    wr_hi = w_router.astype(BF16)
    wr_lo = (w_router - wr_hi.astype(F32)).astype(BF16)
    npt = xp.shape[0] // tm
    nst = xs.shape[0] // tm
    n = xp.shape[0] + xs.shape[0]
    pidx = lambda i: (jnp.minimum(i, npt - 1), 0)
    sidx = lambda i: (jnp.maximum(i - npt, 0), 0)
    row = lambda i: (i, 0)
    return pl.pallas_call(
        functools.partial(_post_kernel, n_prompt_tiles=npt),
        grid=(npt + nst,),
        in_specs=[pl.BlockSpec((tm, D_MODEL), pidx), pl.BlockSpec((tm, D_MODEL), sidx),
                  pl.BlockSpec((tm, NA_WIDTH), pidx), pl.BlockSpec((tm, MLA_WIDTH), pidx),
                  pl.BlockSpec((tm, NA_WIDTH), sidx), pl.BlockSpec((tm, MLA_WIDTH), sidx),
                  pl.BlockSpec((1, 6, D_MODEL), lambda i: (mod_row(i), 0, 0)),
                  _const_spec(w_out.shape), _const_spec((1, D_MODEL)),
                  _const_spec(w_router.shape), _const_spec(w_router.shape), _const_spec((1, N_EXPERTS))],
        out_specs=[pl.BlockSpec((tm, D_MODEL), row),
                   pl.BlockSpec((tm * PACK_ROWS, LANE), row),
                   pl.BlockSpec((tm, LANE), row),
                   pl.BlockSpec((tm, LANE), row)],
        out_shape=[jax.ShapeDtypeStruct((n, D_MODEL), F32),
                   jax.ShapeDtypeStruct((n * PACK_ROWS, LANE), U32),
                   jax.ShapeDtypeStruct((n, LANE), jnp.int32),
                   jax.ShapeDtypeStruct((n, LANE), F32)],
        compiler_params=_cparams(("arbitrary",)),
        name="post_attention_router",
    )(xp, xs, onp, omp, ons, oms, mod, w_out, g_ffn, wr_hi, wr_lo, b_router)


def _deinterleave_kernel(w_ref, p_ref, o_ref):
    p = p_ref[...]
    grp = p.shape[0]
    per_tile = o_ref.shape[3] // grp
    for c in range(w_ref.shape[2] // grp):
        w = w_ref[0, :, c * grp:(c + 1) * grp].astype(BF16)
        col = (c % per_tile) * grp
        o_ref[0, c // per_tile, :, col:col + grp] = _dot(w, p).astype(BF16)


def _deinterleave_gate_up(w_gate_up):
    ne, d, n2 = w_gate_up.shape
    nj = n2 // (2 * MOE_TF)
    grp = 2 * LANE
    dst = np.arange(grp)
    src = np.where(dst < LANE, 2 * dst, 2 * (dst - LANE) + 1)
    perm = np.zeros((grp, grp), np.float32)
    perm[src, dst] = 1.0
    tk = 512
    return pl.pallas_call(
        _deinterleave_kernel,
        grid=(ne, d // tk),
        in_specs=[pl.BlockSpec((1, tk, n2), lambda e, k: (e, k, 0)), _const_spec((grp, grp))],
        out_specs=pl.BlockSpec((1, nj, tk, 2 * MOE_TF), lambda e, k: (e, 0, k, 0)),
        out_shape=jax.ShapeDtypeStruct((ne, nj, d, 2 * MOE_TF), BF16),
        compiler_params=_cparams(("arbitrary", "arbitrary")),
        name="deinterleave_gate_up",
    )(w_gate_up, jnp.asarray(perm, BF16))


def _moe_kernel(ce_ref, nv_ref, cs_ref, order_ref, h_ref, wgu_ref, bgu_ref, wd_ref, bd_ref, y_ref,
                xg_ref, xbf_ref, acc_ref, wdb_ref, ost_ref, sem_g, sem_s):
    c = pl.program_id(0)
    j = pl.program_id(1)
    last_c = pl.num_programs(0) - 1
    last_j = pl.num_programs(1) - 1
    nk = order_ref.shape[0]
    nsub = MOE_CHUNK // MOE_SUB
    xg_rows = MOE_CHUNK * PACK_ROWS

    nv = nv_ref[c]
    c_next = jnp.minimum(c + 1, last_c)
    nv_next = jnp.where(c < last_c, nv_ref[c_next], 0)
    cs_next = cs_ref[c_next]
    c_prev = jnp.maximum(c - 1, 0)
    nv_prev = jnp.where(c > 0, nv_ref[c_prev], 0)
    cs_prev = cs_ref[c_prev]
    nv_prev2 = jnp.where(c > 1, nv_ref[jnp.maximum(c - 2, 0)], 0)
    ost_rows = MOE_CHUNK * OUT_ROWS
    slot = c % 2
    slot_next = (c + 1) % 2

    def gather_row(cs, dst_slot, row):
        tok = jnp.right_shift(order_ref[jnp.minimum(cs + row, nk - 1)], TOP_K_SHIFT)
        src = pl.multiple_of(tok * PACK_ROWS, PACK_ROWS)
        dst = pl.multiple_of(dst_slot * xg_rows + row * PACK_ROWS, PACK_ROWS)
        pltpu.make_async_copy(h_ref.at[pl.ds(src, PACK_ROWS)], xg_ref.at[pl.ds(dst, PACK_ROWS)],
                              sem_g.at[dst_slot]).start()

    def scatter_row(row):
        flat = order_ref[jnp.minimum(cs_prev + row, nk - 1)]
        dest = jnp.where(row < nv_prev, flat, nk + slot_next * MOE_CHUNK + row)
        src = pl.multiple_of(slot_next * ost_rows + row * OUT_ROWS, OUT_ROWS)
        dst = pl.multiple_of(dest * OUT_ROWS, OUT_ROWS)
        pltpu.make_async_copy(ost_ref.at[pl.ds(src, OUT_ROWS)], y_ref.at[pl.ds(dst, OUT_ROWS)],
                              sem_s.at[slot_next]).start()

    def rolled(fn, row0, nrows):
        def body(r0, carry):
            for u in range(DMA_UNROLL):
                fn(row0 + r0 * DMA_UNROLL + u)
            return carry

        lax.fori_loop(0, nrows // DMA_UNROLL, body, 0)

    @pl.when(jnp.logical_and(jnp.logical_and(c == 0, j == 0), nv > 0))
    def _():
        rolled(lambda row: gather_row(cs_ref[0], 0, row), 0, MOE_CHUNK)

    def unpack(sb):
        for s in range(PACK_ROWS):
            w = xg_ref[pl.ds(slot * xg_rows + sb * MOE_SUB * PACK_ROWS + s, MOE_SUB, stride=PACK_ROWS), :]
            lo = pltpu.bitcast(w << 16, F32)
            hi = pltpu.bitcast(w & jnp.uint32(0xFFFF0000), F32)
            xbf_ref[sb * MOE_SUB:(sb + 1) * MOE_SUB, s * LANE:(s + 1) * LANE] = lo.astype(BF16)
            xbf_ref[sb * MOE_SUB:(sb + 1) * MOE_SUB,
                    (s + PACK_ROWS) * LANE:(s + PACK_ROWS + 1) * LANE] = hi.astype(BF16)

    @pl.when(jnp.logical_and(j == 0, nv > 0))
    def _():
        base = pl.multiple_of(slot * xg_rows, xg_rows)
        pltpu.make_async_copy(h_ref.at[pl.ds(0, xg_rows)], xg_ref.at[pl.ds(base, xg_rows)], sem_g.at[slot]).wait()

        acc_ref[...] = jnp.broadcast_to(bd_ref[0], (MOE_CHUNK, D_MODEL))

        @pl.when(c == 0)
        def _():
            ost_ref[...] = jnp.zeros(ost_ref.shape, F32)

        @pl.when(nv == MOE_CHUNK)
        def _():
            for sb in range(nsub):
                unpack(sb)

        @pl.when(nv < MOE_CHUNK)
        def _():
            for sb in range(nsub):
                @pl.when(sb * MOE_SUB < nv)
                def _():
                    unpack(sb)

    def sub_block(sb):
        if sb == 0:
            wdb_ref[...] = wd_ref[0].astype(BF16)
        x = xbf_ref[sb * MOE_SUB:(sb + 1) * MOE_SUB, :]
        gu = _dot(x, wgu_ref[0, 0]) + bgu_ref[0]
        ng = MOE_TF // LANE
        g = jnp.concatenate([gu[:, 2 * b * LANE:(2 * b + 1) * LANE] for b in range(ng)], axis=-1)
        u = jnp.concatenate([gu[:, (2 * b + 1) * LANE:(2 * b + 2) * LANE] for b in range(ng)], axis=-1)
        g = jnp.minimum(g, SWIGLU_LIMIT)
        u = jnp.clip(u, -SWIGLU_LIMIT, SWIGLU_LIMIT)
        act = g * (1.0 / (1.0 + jnp.exp(-(g * SWIGLU_ALPHA)))) * (u + 1.0)
        acc_ref[sb * MOE_SUB:(sb + 1) * MOE_SUB, :] += _dot(act.astype(BF16), wdb_ref[...])

    dma_both = jnp.logical_and(nv_next > 0, nv_prev > 0)
    per_sb = MOE_SUB // nsub
    for sb in range(nsub):
        row0 = j * MOE_SUB + sb * per_sb
        compute = sb * MOE_SUB < nv
        fused = jnp.logical_and(compute, dma_both)

        @pl.when(fused)
        def _():
            for r in range(per_sb):
                gather_row(cs_next, slot_next, row0 + r)
                scatter_row(row0 + r)
            sub_block(sb)

        @pl.when(jnp.logical_not(fused))
        def _():
            @pl.when(nv_next > 0)
            def _():
                for r in range(per_sb):
                    gather_row(cs_next, slot_next, row0 + r)

            @pl.when(nv_prev > 0)
            def _():
                for r in range(per_sb):
                    scatter_row(row0 + r)

            @pl.when(compute)
            def _():
                sub_block(sb)

    def wait_scatter(buf):
        base = pl.multiple_of(buf * ost_rows, ost_rows)
        pltpu.make_async_copy(ost_ref.at[pl.ds(base, ost_rows)], y_ref.at[pl.ds(0, ost_rows)], sem_s.at[buf]).wait()

    @pl.when(j == last_j)
    def _():
        @pl.when(nv_prev2 > 0)
        def _():
            wait_scatter(slot)

        for sb in range(nsub):
            @pl.when(sb * MOE_SUB < nv)
            def _():
                for s in range(OUT_ROWS):
                    dst = pl.ds(slot * ost_rows + sb * MOE_SUB * OUT_ROWS + s, MOE_SUB, stride=OUT_ROWS)
                    ost_ref[dst, :] = acc_ref[sb * MOE_SUB:(sb + 1) * MOE_SUB, s * LANE:(s + 1) * LANE]

        @pl.when(jnp.logical_and(c == last_c, nv_prev > 0))
        def _():
            wait_scatter(slot_next)


def _moe(chunk_expert, chunk_rows, chunk_start, order, h2p, wgu, bgu, w_down, b_down):
    nj = D_FF // MOE_TF
    ngrid = chunk_expert.shape[0]
    nk = order.shape[0]

    def jj(c, j, nv):
        return jnp.where(nv[c] > 0, j, nj - 1)

    return pl.pallas_call(
        _moe_kernel,
        grid_spec=pltpu.PrefetchScalarGridSpec(
            num_scalar_prefetch=4,
            grid=(ngrid, nj),
            in_specs=[
                pl.BlockSpec(memory_space=pl.ANY),
                pl.BlockSpec((1, 1, D_MODEL, 2 * MOE_TF), lambda c, j, ce, nv, cs, od: (ce[c], jj(c, j, nv), 0, 0)),
                pl.BlockSpec((1, 1, 2 * MOE_TF), lambda c, j, ce, nv, cs, od: (ce[c], 0, jj(c, j, nv))),
                pl.BlockSpec((1, MOE_TF, D_MODEL), lambda c, j, ce, nv, cs, od: (ce[c], jj(c, j, nv), 0)),
                pl.BlockSpec((1, 1, D_MODEL), lambda c, j, ce, nv, cs, od: (ce[c], 0, 0))],
            out_specs=pl.BlockSpec(memory_space=pl.ANY),
            scratch_shapes=[pltpu.VMEM((2 * MOE_CHUNK * PACK_ROWS, LANE), U32),
                            pltpu.VMEM((MOE_CHUNK, D_MODEL), BF16),
                            pltpu.VMEM((MOE_CHUNK, D_MODEL), F32),
                            pltpu.VMEM((MOE_TF, D_MODEL), BF16),
                            pltpu.VMEM((2 * MOE_CHUNK * OUT_ROWS, LANE), F32),
                            pltpu.SemaphoreType.DMA((2,)),
                            pltpu.SemaphoreType.DMA((2,))]),
        out_shape=jax.ShapeDtypeStruct(((nk + 2 * MOE_CHUNK) * OUT_ROWS, LANE), F32),
        compiler_params=_cparams(("arbitrary", "arbitrary")),
        name="moe_experts",
    )(chunk_expert, chunk_rows, chunk_start, order, h2p, wgu, bgu, w_down, b_down)


def _combine_kernel(gate_ref, x1_ref, mod_ref, gfin_ref, ye_ref, y_ref):
    gate = gate_ref[...]
    pieces = []
    for s in range(OUT_ROWS):
        acc = None
        for k in range(TOP_K):
            rows = ye_ref[pl.ds(k * OUT_ROWS + s, COMBINE_TOKENS, stride=TOP_K * OUT_ROWS), :]
            term = gate[:, k:k + 1] * rows
            acc = term if acc is None else acc + term
        pieces.append(acc)
    y = jnp.concatenate(pieces, axis=-1)
    gf = mod_ref[0, 5:6, :]
    y_ref[...] = _rms(x1_ref[...] + gf * y, gfin_ref[...])


def _combine(gates, x1, mod, g_final, y_experts, *, tile0, ntiles, mod_row):
    tt = COMBINE_TOKENS
    return pl.pallas_call(
        _combine_kernel,
        grid=(ntiles,),
        in_specs=[pl.BlockSpec((tt, LANE), lambda i: (tile0 + i, 0)),
                  pl.BlockSpec((tt, D_MODEL), lambda i: (tile0 + i, 0)),
                  pl.BlockSpec((1, 6, D_MODEL), lambda i: (mod_row(i), 0, 0)),
                  _const_spec((1, D_MODEL)),
                  pl.BlockSpec((tt * TOP_K * OUT_ROWS, LANE), lambda i: (tile0 + i, 0))],
        out_specs=pl.BlockSpec((tt, D_MODEL), lambda i: (i, 0)),
        out_shape=jax.ShapeDtypeStruct((ntiles * tt, D_MODEL), F32),
        compiler_params=_cparams(("arbitrary",)),
        name="moe_combine_final_norm",
    )(gates, x1, mod, g_final, y_experts)


def _routing(top_idx, nch):
    n = top_idx.shape[0]
    nk = n * TOP_K
    flat_e = top_idx.reshape(nk)
    order = jnp.argsort(flat_e, stable=True).astype(jnp.int32)
    onehot = flat_e[:, None] == jnp.arange(N_EXPERTS, dtype=jnp.int32)[None, :]
    counts = jnp.sum(onehot.astype(jnp.int32), axis=0)
    grp_start = jnp.cumsum(counts) - counts
    chunks_e = (counts + MOE_CHUNK - 1) // MOE_CHUNK
    chunk_end = jnp.cumsum(chunks_e)
    chunk_start = chunk_end - chunks_e

    total = chunk_end[-1]
    cidx = jnp.arange(nch + 1, dtype=jnp.int32)
    active = cidx < total
    ce = jnp.minimum(jnp.searchsorted(chunk_end, cidx, side='right'), N_EXPERTS - 1).astype(jnp.int32)
    local = (cidx - chunk_start[ce]) * MOE_CHUNK
    nv = jnp.where(active, jnp.clip(counts[ce] - local, 0, MOE_CHUNK), 0).astype(jnp.int32)
    last = jnp.maximum(total - 1, 0)
    chunk_expert = jnp.where(active, ce, ce[last]).astype(jnp.int32)
    chunk_first = jnp.where(active, jnp.clip(grp_start[ce] + local, 0, nk - 1), 0).astype(jnp.int32)
    return order, chunk_expert, nv, chunk_first


def _rope_tables(t):
    pos = jnp.arange(t)
    rows = (pos // GRID_W).astype(F32)
    cols = (pos % GRID_W).astype(F32)
    inv = ROPE_THETA ** (-(jnp.arange(ROPE_AXIS // 2, dtype=F32) * 2.0 / ROPE_AXIS))
    ar = rows[:, None] * inv
    ac = cols[:, None] * inv
    ang = jnp.concatenate([ar, ar, ac, ac], axis=-1)
    return jnp.cos(ang), jnp.sin(ang)


def _rot_cols(w):
    half = ROPE_AXIS // 2
    src = np.concatenate([np.arange(half, ROPE_AXIS), np.arange(0, half),
                          np.arange(ROPE_AXIS + half, 2 * ROPE_AXIS), np.arange(ROPE_AXIS, ROPE_AXIS + half)])
    sign = np.concatenate([-np.ones(half), np.ones(half), -np.ones(half), np.ones(half)]).astype(np.float32)
    return w[..., src] * sign


def kernel(x_prompt, x_sample, cache_na_k, cache_na_v, cache_mla_ckv, cache_mla_krope, c, c_ctx, g_attn, g_ffn, g_final, w_mod, b_mod, w_in, w_out, na_rpb, g_q_a, w_q_b, g_kv_a, w_kv_b, w_router, b_router, w_gate_up, b_gate_up, w_down, b_down):
    bp, sp, d = x_prompt.shape
    bd, td, _ = x_sample.shape
    assert d == D_MODEL and w_mod.shape[0] == 1, "one trunk layer of width D_MODEL"
    n_p = bp * sp
    n_s = bd * td
    xp = x_prompt.reshape(n_p, d)
    xs = x_sample.reshape(n_s, d)

    c8 = jnp.zeros((8, d), F32).at[0].set(c_ctx).at[1:1 + bd].set(c)
    mod = _modulation(c8, w_mod[0], b_mod[0].reshape(1, -1)).reshape(8, 6, d)

    w_in0 = w_in[0]
    w_kr = w_in0[:, KR_OFF:KR_OFF + QK_ROPE]
    w_in_p = jnp.concatenate([w_in0, jnp.zeros((d, LANE - QK_ROPE), F32)], axis=1).astype(BF16)
    w_in_s = jnp.concatenate([w_in0, _rot_cols(w_kr)], axis=1).astype(BF16)
    wq = w_q_b[0].reshape(Q_LORA, MLA_HEADS, QK_NOPE + QK_ROPE)
    zpad = jnp.zeros((Q_LORA, MLA_HEADS, Q_PAD - QK_NOPE - QK_ROPE), F32)
    wq_pad = jnp.concatenate([wq, zpad], axis=-1).reshape(Q_LORA, MLA_HEADS * Q_PAD)
    wq_rot = jnp.concatenate([jnp.zeros((Q_LORA, MLA_HEADS, QK_NOPE), F32), _rot_cols(wq[..., QK_NOPE:]), zpad],
                             axis=-1).reshape(Q_LORA, MLA_HEADS * Q_PAD)
    wqb_p = wq_pad.astype(BF16)
    wqb_s = jnp.concatenate([wq_pad, wq_rot], axis=1).astype(BF16)
    w_kvb = w_kv_b[0].astype(BF16)
    cos, sin = _rope_tables(td)
    cosq = jnp.concatenate([jnp.ones((td, QK_NOPE), F32), cos, jnp.ones((td, Q_PAD - QK_NOPE - QK_ROPE), F32)], axis=1)
    sinq = jnp.concatenate([jnp.zeros((td, QK_NOPE), F32), sin, jnp.zeros((td, Q_PAD - QK_NOPE - QK_ROPE), F32)], axis=1)
    cosq = cosq * MLA_QSCALE
    sinq = sinq * MLA_QSCALE
    csk = jnp.concatenate([cos, sin], axis=1)

    tiles_per_seq = td // ROW_TILE
    g_attn2 = g_attn[0].reshape(1, d)
    gq2 = g_q_a[0].reshape(1, Q_LORA)
    gkv2 = g_kv_a[0].reshape(1, KV_LORA)

    naq_p, nak_p, nav_p, q_p, ckv_p, kr_p, krp_p = _pre_attention(
        xp, mod, g_attn2, w_in_p, gq2, wqb_p, gkv2, None, rope=False, mod_row=lambda i: 0)
    ona_p, omla_p = _prompt_attention(naq_p, nak_p, nav_p, q_p, ckv_p, krp_p, w_kvb, sp)

    naq_s, nak_s, nav_s, q_s, ckv_s, krp_s = _pre_attention(
        xs, mod, g_attn2, w_in_s, gq2, wqb_s, gkv2, (cosq, sinq, csk), rope=True,
        mod_row=lambda i: 1 + i // tiles_per_seq)
    past = cache_na_k.shape[2]
    kc = cache_na_k[:, 0].reshape(bd, past, NA_WIDTH)
    vc = cache_na_v[:, 0].reshape(bd, past, NA_WIDTH)
    bias = _na_bias_table(na_rpb[0] * LOG2E, td // GRID_W)
    ona_s = _sample_na(naq_s, nak_s, nav_s, kc, vc, bias, bd, td)
    ckv_all = jnp.concatenate([ckv_s.reshape(bd, td, KV_LORA), cache_mla_ckv[:, 0].astype(BF16)], axis=1)
    krp_c = jnp.concatenate([cache_mla_krope[:, 0], jnp.zeros((bd, past, LANE - QK_ROPE), F32)], axis=-1).astype(BF16)
    krp_all = jnp.concatenate([krp_s.reshape(bd, td, LANE), krp_c], axis=1)
    kf, vv = _kv_expand(ckv_all, krp_all, w_kvb)
    omla_s = _sample_mla(q_s, kf, vv, bd, td)

    npt = n_p // POST_TILE
    post_tiles_per_seq = td // POST_TILE
    x1, h2p, idx128, gate128 = _post_attention(
        xp, xs, ona_p, omla_p, ona_s, omla_s, mod, w_out[0].astype(BF16), g_ffn[0].reshape(1, d),
        w_router[0], b_router[0].reshape(1, N_EXPERTS),
        mod_row=lambda i: jnp.where(i < npt, 0, 1 + jnp.maximum(i - npt, 0) // post_tiles_per_seq))

    n = n_p + n_s
    nch = n * TOP_K // MOE_CHUNK + N_EXPERTS
    order, chunk_expert, chunk_rows, chunk_first = _routing(idx128[:, :TOP_K], nch)
    wgu = _deinterleave_gate_up(w_gate_up[0])
    ng = 2 * D_FF // (2 * LANE)
    bgu = b_gate_up[0].reshape(N_EXPERTS, ng, LANE, 2).transpose(0, 1, 3, 2).reshape(N_EXPERTS, 1, 2 * D_FF)
    y_experts = _moe(chunk_expert, chunk_rows, chunk_first, order, h2p, wgu, bgu, w_down[0],
                     b_down[0].reshape(N_EXPERTS, 1, d))

    tt = COMBINE_TOKENS
    gfin = g_final.reshape(1, d)
    ctiles_seq = td // tt
    y_p = _combine(gate128, x1, mod, gfin, y_experts, tile0=0, ntiles=n_p // tt, mod_row=lambda i: 0)
    y_s = _combine(gate128, x1, mod, gfin, y_experts, tile0=n_p // tt, ntiles=n_s // tt,
                   mod_row=lambda i: 1 + i // ctiles_seq)

    return (y_p.reshape(bp, sp, d), y_s.reshape(bd, td, d),
            nak_p.reshape(bp, 1, sp, NA_HEADS, NA_HEAD_DIM), nav_p.reshape(bp, 1, sp, NA_HEADS, NA_HEAD_DIM),
            ckv_p.reshape(bp, 1, sp, KV_LORA), kr_p.reshape(bp, 1, sp, QK_ROPE))
```

```python
import functools

import numpy as np
import jax
import jax.numpy as jnp
from jax import lax
from jax.experimental import pallas as pl
from jax.experimental.pallas import tpu as pltpu

F32 = jnp.float32
BF16 = jnp.bfloat16
U32 = jnp.uint32

D_MODEL = 2048
GRID_W = 64
NA_HEADS = 8
NA_HEAD_DIM = 128
NA_KR = 8
NA_KC = 16
MLA_HEADS = 8
Q_LORA = 512
KV_LORA = 256
QK_NOPE = 128
QK_ROPE = 64
V_DIM = 128
ROPE_AXIS = QK_ROPE // 2
ROPE_THETA = 10000.0
NA_WIDTH = NA_HEADS * NA_HEAD_DIM
MLA_WIDTH = MLA_HEADS * V_DIM
IN_COLS = 3 * NA_WIDTH + Q_LORA + KV_LORA + QK_ROPE
N_EXPERTS = 32
TOP_K = 4
D_FF = D_MODEL
SWIGLU_ALPHA = 1.702
SWIGLU_LIMIT = 7.0
EPS = 1e-6

LANE = 128
Q_PAD = 2 * LANE
KR_OFF = 3 * NA_WIDTH + Q_LORA + KV_LORA
IN_COLS_PAD = KR_OFF + LANE
VMEM_LIMIT = 56 * 1024 * 1024
NEG_BIG = -1e30

LOG2E = 1.4426950408889634
NA_QSCALE = NA_HEAD_DIM ** -0.5 * LOG2E
MLA_QSCALE = (QK_NOPE + QK_ROPE) ** -0.5 * LOG2E

ROW_TILE = 256
POST_TILE = 2 * ROW_TILE
NA_Q_ROWS = 4
NA_WIN_ROWS = 12
NA_HEADS_PER_STEP = 4
MLA_TQ = 512
MLA_TK = 2304
MOE_CHUNK = 1024
MOE_SUB = 256
MOE_TF = 512
COMBINE_TOKENS = 256
TOP_K_SHIFT = TOP_K.bit_length() - 1
DMA_UNROLL = 8
PACK_ROWS = D_MODEL // (2 * LANE)
OUT_ROWS = D_MODEL // LANE


def _cparams(sem):
    return pltpu.CompilerParams(dimension_semantics=sem, vmem_limit_bytes=VMEM_LIMIT)


def _const_spec(shape):
    nd = len(shape)
    return pl.BlockSpec(shape, lambda *a: (0,) * nd, pipeline_mode=pl.Buffered(1))


def _rms(x, g):
    return x * lax.rsqrt(jnp.mean(x * x, axis=-1, keepdims=True) + EPS) * g


def _dot(a, b):
    return jnp.dot(a, b, preferred_element_type=F32)


def _dot_nt(a, b):
    return lax.dot_general(a, b, (((1,), (1,)), ((), ())), preferred_element_type=F32)


def _mod_kernel(c_ref, w_ref, b_ref, o_ref):
    c = c_ref[...]
    s = c / (1.0 + jnp.exp(-c))
    o_ref[...] = _dot(s.astype(BF16), w_ref[...].astype(BF16)) + b_ref[...]


def _modulation(c8, w_mod, b_mod):
    n = w_mod.shape[1]
    tn = 1024
    return pl.pallas_call(
        _mod_kernel,
        grid=(n // tn,),
        in_specs=[pl.BlockSpec((8, D_MODEL), lambda j: (0, 0)),
                  pl.BlockSpec((D_MODEL, tn), lambda j: (0, j)),
                  pl.BlockSpec((1, tn), lambda j: (0, j))],
        out_specs=pl.BlockSpec((8, tn), lambda j: (0, j)),
        out_shape=jax.ShapeDtypeStruct((8, n), F32),
        compiler_params=_cparams(("arbitrary",)),
        name="modulation",
    )(c8, w_mod, b_mod)


def _pre_kernel(*refs, rope):
    if rope:
        (x_ref, mod_ref, g_ref, win_ref, gq_ref, wqb_ref, gkv_ref, cosq_ref, sinq_ref, csk_ref,
         naq_ref, nak_ref, nav_ref, q_ref, ckv_ref, krp_ref) = refs
    else:
        (x_ref, mod_ref, g_ref, win_ref, gq_ref, wqb_ref, gkv_ref,
         naq_ref, nak_ref, nav_ref, q_ref, ckv_ref, kr_ref, krp_ref) = refs
    x = x_ref[...]
    sa = mod_ref[0, 0:1, :]
    sca = mod_ref[0, 1:2, :]
    h = (_rms(x, g_ref[...]) * (1.0 + sca) + sa).astype(BF16)
    proj = _dot(h, win_ref[...])
    naq_ref[...] = (proj[:, 0:NA_WIDTH] * NA_QSCALE).astype(naq_ref.dtype)
    nak_ref[...] = proj[:, NA_WIDTH:2 * NA_WIDTH].astype(nak_ref.dtype)
    nav_ref[...] = proj[:, 2 * NA_WIDTH:3 * NA_WIDTH].astype(nav_ref.dtype)
    q_a = proj[:, 3 * NA_WIDTH:3 * NA_WIDTH + Q_LORA]
    kv_a = proj[:, 3 * NA_WIDTH + Q_LORA:KR_OFF]
    krx = proj[:, KR_OFF:IN_COLS_PAD]
    qan = _rms(q_a, gq_ref[...]).astype(BF16)
    qq = _dot(qan, wqb_ref[...])
    ckv_ref[...] = _rms(kv_a, gkv_ref[...]).astype(ckv_ref.dtype)
    if rope:
        width = MLA_HEADS * Q_PAD
        cosq = cosq_ref[...]
        sinq = sinq_ref[...]
        for hd in range(MLA_HEADS):
            a = qq[:, hd * Q_PAD:(hd + 1) * Q_PAD]
            b = qq[:, width + hd * Q_PAD:width + (hd + 1) * Q_PAD]
            q_ref[:, hd * Q_PAD:(hd + 1) * Q_PAD] = (a * cosq + b * sinq).astype(BF16)
        y = krx * csk_ref[...]
        y = y + pltpu.roll(y, QK_ROPE, 1)
        lane = lax.broadcasted_iota(jnp.int32, y.shape, 1)
        krp_ref[...] = jnp.where(lane < QK_ROPE, y, 0.0).astype(BF16)
    else:
        q_ref[...] = (qq * MLA_QSCALE).astype(BF16)
        kr_ref[...] = krx[:, 0:QK_ROPE]
        krp_ref[...] = krx.astype(BF16)


def _pre_attention(x, mod, g_attn, w_in, g_q_a, w_qb, g_kv_a, rope_tabs, *, rope, mod_row):
    n = x.shape[0]
    tm = ROW_TILE
    row = lambda i: (i, 0)
    in_specs = [pl.BlockSpec((tm, D_MODEL), row),
                pl.BlockSpec((1, 6, D_MODEL), lambda i: (mod_row(i), 0, 0)),
                _const_spec((1, D_MODEL)),
                _const_spec(w_in.shape),
                _const_spec((1, Q_LORA)),
                _const_spec(w_qb.shape),
                _const_spec((1, KV_LORA))]
    args = [x, mod, g_attn, w_in, g_q_a, w_qb, g_kv_a]
    qw = MLA_HEADS * Q_PAD
    if rope:
        tiles_per_seq = rope_tabs[0].shape[0] // tm
        pos = lambda i: (i % tiles_per_seq, 0)
        in_specs += [pl.BlockSpec((tm, Q_PAD), pos), pl.BlockSpec((tm, Q_PAD), pos),
                     pl.BlockSpec((tm, LANE), pos)]
        args += list(rope_tabs)
        out_shape = [jax.ShapeDtypeStruct((n, NA_WIDTH), BF16)] * 3 + [
            jax.ShapeDtypeStruct((n, qw), BF16),
            jax.ShapeDtypeStruct((n, KV_LORA), BF16),
            jax.ShapeDtypeStruct((n, LANE), BF16)]
        out_specs = [pl.BlockSpec((tm, NA_WIDTH), row)] * 3 + [
            pl.BlockSpec((tm, qw), row), pl.BlockSpec((tm, KV_LORA), row), pl.BlockSpec((tm, LANE), row)]
    else:
        out_shape = [jax.ShapeDtypeStruct((n, NA_WIDTH), BF16),
                     jax.ShapeDtypeStruct((n, NA_WIDTH), F32),
                     jax.ShapeDtypeStruct((n, NA_WIDTH), F32),
                     jax.ShapeDtypeStruct((n, qw), BF16),
                     jax.ShapeDtypeStruct((n, KV_LORA), F32),
                     jax.ShapeDtypeStruct((n, QK_ROPE), F32),
                     jax.ShapeDtypeStruct((n, LANE), BF16)]
        out_specs = [pl.BlockSpec((tm, NA_WIDTH), row)] * 3 + [
            pl.BlockSpec((tm, qw), row), pl.BlockSpec((tm, KV_LORA), row),
            pl.BlockSpec((tm, QK_ROPE), row), pl.BlockSpec((tm, LANE), row)]
    return pl.pallas_call(
        functools.partial(_pre_kernel, rope=rope),
        grid=(n // tm,),
        in_specs=in_specs,
        out_specs=out_specs,
        out_shape=out_shape,
        compiler_params=_cparams(("arbitrary",)),
        name="pre_attention_rope" if rope else "pre_attention",
    )(*args)


def _softmax_pv(s, v):
    m = jnp.max(s, axis=-1, keepdims=True)
    p = jnp.exp2(s - m)
    l = jnp.sum(p, axis=-1, keepdims=True)
    return _dot(p.astype(BF16), v) / l


def _prompt_attn_kernel(naq_ref, nak_ref, nav_ref, q_ref, ckv_ref, krp_ref, wkvb_ref, ona_ref, omla_ref):
    kv = _dot(ckv_ref[...].astype(BF16), wkvb_ref[...])
    krp = krp_ref[...]
    for hd in range(NA_HEADS):
        sl = slice(hd * NA_HEAD_DIM, (hd + 1) * NA_HEAD_DIM)
        s = _dot_nt(naq_ref[:, sl], nak_ref[:, sl].astype(BF16))
        ona_ref[:, sl] = _softmax_pv(s, nav_ref[:, sl].astype(BF16)).astype(BF16)
    kvw = QK_NOPE + V_DIM
    for hd in range(MLA_HEADS):
        kf = jnp.concatenate([kv[:, hd * kvw:hd * kvw + QK_NOPE].astype(BF16), krp], axis=-1)
        s = _dot_nt(q_ref[:, hd * Q_PAD:(hd + 1) * Q_PAD], kf)
        v = kv[:, hd * kvw + QK_NOPE:(hd + 1) * kvw].astype(BF16)
        omla_ref[:, hd * V_DIM:(hd + 1) * V_DIM] = _softmax_pv(s, v).astype(BF16)


def _prompt_attention(naq, nak, nav, q, ckv, krp, w_kvb, seq):
    n = naq.shape[0]
    row = lambda b: (b, 0)
    return pl.pallas_call(
        _prompt_attn_kernel,
        grid=(n // seq,),
        in_specs=[pl.BlockSpec((seq, NA_WIDTH), row)] * 3 + [
            pl.BlockSpec((seq, MLA_HEADS * Q_PAD), row),
            pl.BlockSpec((seq, KV_LORA), row),
            pl.BlockSpec((seq, LANE), row),
            _const_spec(w_kvb.shape)],
        out_specs=[pl.BlockSpec((seq, NA_WIDTH), row), pl.BlockSpec((seq, MLA_WIDTH), row)],
        out_shape=[jax.ShapeDtypeStruct((n, NA_WIDTH), BF16), jax.ShapeDtypeStruct((n, MLA_WIDTH), BF16)],
        compiler_params=_cparams(("arbitrary",)),
        name="prompt_attention",
    )(naq, nak, nav, q, ckv, krp, w_kvb)


def _na_kernel(q_ref, k_ref, v_ref, kc_ref, vc_ref, bias_ref, o_ref, *, rows):
    rt = pl.program_id(2)
    ws = jnp.clip(rt * NA_Q_ROWS - NA_KR // 2, 0, rows - NA_WIN_ROWS)
    start = pl.multiple_of(ws * GRID_W, GRID_W)
    nwin = NA_WIN_ROWS * GRID_W
    for hh in range(NA_HEADS_PER_STEP):
        hs = slice(hh * NA_HEAD_DIM, (hh + 1) * NA_HEAD_DIM)
        q = q_ref[:, hs]
        s_loc = _dot_nt(q, k_ref[pl.ds(start, nwin), hs]) + bias_ref[0, hh]
        s_ctx = _dot_nt(q, kc_ref[0, :, hs].astype(BF16))
        m = jnp.maximum(jnp.max(s_loc, axis=-1, keepdims=True), jnp.max(s_ctx, axis=-1, keepdims=True))
        p_loc = jnp.exp2(s_loc - m)
        p_ctx = jnp.exp2(s_ctx - m)
        l = jnp.sum(p_loc, axis=-1, keepdims=True) + jnp.sum(p_ctx, axis=-1, keepdims=True)
        o = (_dot(p_loc.astype(BF16), v_ref[pl.ds(start, nwin), hs])
             + _dot(p_ctx.astype(BF16), vc_ref[0, :, hs].astype(BF16)))
        o_ref[:, hs] = (o / l).astype(BF16)


def _na_bias_table(rpb, rows):
    nh = rpb.shape[0]
    a = np.arange(NA_Q_ROWS)
    b = np.arange(NA_WIN_ROWS)
    col = np.arange(GRID_W)
    cs = np.clip(col - NA_KC // 2, 0, GRID_W - NA_KC)
    valid_col = (col[None, :] >= cs[:, None]) & (col[None, :] < cs[:, None] + NA_KC)
    padw = GRID_W - NA_KC
    rp = jnp.pad(rpb, ((0, 0), (0, 0), (padw, padw)))
    toep = jnp.stack([rp[:, :, GRID_W - 1 - qc:2 * GRID_W - 1 - qc] for qc in range(GRID_W)], axis=2)
    toep = jnp.where(valid_col[None, None], toep, NEG_BIG)
    masked = jnp.full((nh, GRID_W, GRID_W), NEG_BIG, F32)
    tabs = []
    for r0 in (0, NA_Q_ROWS, rows - NA_Q_ROWS):
        ws = int(np.clip(r0 - NA_KR // 2, 0, rows - NA_WIN_ROWS))
        r = r0 + a
        rs = np.clip(r - NA_KR // 2, 0, rows - NA_KR)
        kr = ws + b
        valid_row = (kr[None, :] >= rs[:, None]) & (kr[None, :] < rs[:, None] + NA_KR)
        dr = kr[None, :] - r[:, None] + NA_KR - 1
        tile_rows = []
        for ai in range(NA_Q_ROWS):
            blocks = [toep[:, int(dr[ai, bi])] if valid_row[ai, bi] else masked for bi in range(NA_WIN_ROWS)]
            tile_rows.append(jnp.concatenate(blocks, axis=-1))
        tabs.append(jnp.concatenate(tile_rows, axis=1))
    return jnp.stack(tabs)


def _sample_na(naq, nak, nav, kc, vc, bias, batch, seq):
    rows = seq // GRID_W
    tq = NA_Q_ROWS * GRID_W
    nt = seq // tq
    last = nt - 1
    hw = NA_HEADS_PER_STEP * NA_HEAD_DIM

    def pat(rt):
        return jnp.where(rt == 0, 0, jnp.where(rt == last, 2, 1))

    return pl.pallas_call(
        functools.partial(_na_kernel, rows=rows),
        grid=(batch, NA_HEADS // NA_HEADS_PER_STEP, nt),
        in_specs=[pl.BlockSpec((tq, hw), lambda b, h, r: (b * nt + r, h)),
                  pl.BlockSpec((seq, hw), lambda b, h, r: (b, h)),
                  pl.BlockSpec((seq, hw), lambda b, h, r: (b, h)),
                  pl.BlockSpec((1, kc.shape[1], hw), lambda b, h, r: (b, 0, h)),
                  pl.BlockSpec((1, vc.shape[1], hw), lambda b, h, r: (b, 0, h)),
                  pl.BlockSpec((1, NA_HEADS_PER_STEP, tq, NA_WIN_ROWS * GRID_W), lambda b, h, r: (pat(r), h, 0, 0))],
        out_specs=pl.BlockSpec((tq, hw), lambda b, h, r: (b * nt + r, h)),
        out_shape=jax.ShapeDtypeStruct((batch * seq, NA_WIDTH), BF16),
        compiler_params=_cparams(("arbitrary", "arbitrary", "arbitrary")),
        name="sample_neighbourhood_attention",
    )(naq, nak, nav, kc, vc, bias)


def _kv_expand_kernel(ckv_ref, krp_ref, w_ref, kf_ref, v_ref):
    kv = _dot(ckv_ref[0], w_ref[...])
    krp = krp_ref[0]
    kvw = QK_NOPE + V_DIM
    for hd in range(MLA_HEADS):
        kf_ref[0, hd, :, 0:QK_NOPE] = kv[:, hd * kvw:hd * kvw + QK_NOPE].astype(BF16)
        kf_ref[0, hd, :, QK_NOPE:Q_PAD] = krp
        v_ref[0, hd] = kv[:, hd * kvw + QK_NOPE:(hd + 1) * kvw].astype(BF16)


def _kv_expand(ckv, krp, w_kvb):
    batch, nkeys, _ = ckv.shape
    tm = 512
    return pl.pallas_call(
        _kv_expand_kernel,
        grid=(batch, nkeys // tm),
        in_specs=[pl.BlockSpec((1, tm, KV_LORA), lambda b, t: (b, t, 0)),
                  pl.BlockSpec((1, tm, LANE), lambda b, t: (b, t, 0)),
                  _const_spec(w_kvb.shape)],
        out_specs=[pl.BlockSpec((1, MLA_HEADS, tm, Q_PAD), lambda b, t: (b, 0, t, 0)),
                   pl.BlockSpec((1, MLA_HEADS, tm, V_DIM), lambda b, t: (b, 0, t, 0))],
        out_shape=[jax.ShapeDtypeStruct((batch, MLA_HEADS, nkeys, Q_PAD), BF16),
                   jax.ShapeDtypeStruct((batch, MLA_HEADS, nkeys, V_DIM), BF16)],
        compiler_params=_cparams(("arbitrary", "arbitrary")),
        name="latent_kv_expand",
    )(ckv, krp, w_kvb)


def _mla_kernel(q_ref, kf_ref, v_ref, o_ref, *, nkeys):
    q = q_ref[...]
    tq = q.shape[0]
    m = jnp.full((tq, 1), NEG_BIG, F32)
    l = jnp.zeros((tq, 1), F32)
    acc = jnp.zeros((tq, V_DIM), F32)
    for c in range(nkeys // MLA_TK):
        ks = slice(c * MLA_TK, (c + 1) * MLA_TK)
        s = _dot_nt(q, kf_ref[0, 0, ks, :])
        m_new = jnp.maximum(m, jnp.max(s, axis=-1, keepdims=True))
        alpha = jnp.exp2(m - m_new)
        p = jnp.exp2(s - m_new)
        l = alpha * l + jnp.sum(p, axis=-1, keepdims=True)
        acc = alpha * acc + _dot(p.astype(BF16), v_ref[0, 0, ks, :])
        m = m_new
    o_ref[...] = (acc / l).astype(BF16)


def _sample_mla(q, kf, v, batch, seq):
    nkeys = kf.shape[2]
    nt = seq // MLA_TQ
    assert nkeys % MLA_TK == 0, "latent key count must be a whole number of softmax passes"
    return pl.pallas_call(
        functools.partial(_mla_kernel, nkeys=nkeys),
        grid=(batch, MLA_HEADS, nt),
        in_specs=[pl.BlockSpec((MLA_TQ, Q_PAD), lambda b, h, t: (b * nt + t, h)),
                  pl.BlockSpec((1, 1, nkeys, Q_PAD), lambda b, h, t: (b, h, 0, 0)),
                  pl.BlockSpec((1, 1, nkeys, V_DIM), lambda b, h, t: (b, h, 0, 0))],
        out_specs=pl.BlockSpec((MLA_TQ, V_DIM), lambda b, h, t: (b * nt + t, h)),
        out_shape=jax.ShapeDtypeStruct((batch * seq, MLA_WIDTH), BF16),
        compiler_params=_cparams(("arbitrary", "arbitrary", "arbitrary")),
        name="sample_latent_attention",
    )(q, kf, v)


def _post_kernel(xp_ref, xs_ref, onp_ref, omp_ref, ons_ref, oms_ref, mod_ref, wout_ref, gffn_ref, wrh_ref, wrl_ref,
                 br_ref, x1_ref, h2p_ref, idx_ref, gate_ref, *, n_prompt_tiles):
    is_prompt = pl.program_id(0) < n_prompt_tiles
    ga = mod_ref[0, 2:3, :]
    sf = mod_ref[0, 3:4, :]
    scf = mod_ref[0, 4:5, :]
    tm = ROW_TILE
    for r0 in range(0, xp_ref.shape[0], tm):
        rs = slice(r0, r0 + tm)
        x = jnp.where(is_prompt, xp_ref[rs, :], xs_ref[rs, :])
        ona = jnp.where(is_prompt, onp_ref[rs, :], ons_ref[rs, :])
        omla = jnp.where(is_prompt, omp_ref[rs, :], oms_ref[rs, :])
        o = _dot(ona, wout_ref[0:NA_WIDTH, :]) + _dot(omla, wout_ref[NA_WIDTH:NA_WIDTH + MLA_WIDTH, :])
        x1 = x + ga * o
        x1_ref[rs, :] = x1
        h2 = _rms(x1, gffn_ref[...]) * (1.0 + scf) + sf
        h_hi = h2.astype(BF16)
        h_lo = (h2 - h_hi.astype(F32)).astype(BF16)
        logits = _dot(h_hi, wrh_ref[...]) + _dot(h_lo, wrh_ref[...]) + _dot(h_hi, wrl_ref[...]) + br_ref[...]
        lane_e = lax.broadcasted_iota(jnp.int32, logits.shape, 1).astype(F32)
        lane_o = lax.broadcasted_iota(jnp.int32, (tm, LANE), 1)
        idx_out = jnp.zeros((tm, LANE), F32)
        gate_out = jnp.zeros((tm, LANE), F32)
        top0 = None
        denom = jnp.zeros((tm, 1), F32)
        cur = logits
        for k in range(TOP_K):
            mx = jnp.max(cur, axis=-1, keepdims=True)
            ix = jnp.min(jnp.where(cur == mx, lane_e, float(N_EXPERTS)), axis=-1, keepdims=True)
            cur = jnp.where(lane_e == ix, -jnp.inf, cur)
            if k == 0:
                top0 = mx
            e = jnp.exp(mx - top0)
            denom = denom + e
            idx_out = jnp.where(lane_o == k, ix, idx_out)
            gate_out = jnp.where(lane_o == k, e, gate_out)
        idx_ref[rs, :] = idx_out.astype(jnp.int32)
        gate_ref[rs, :] = gate_out / denom
        bits = pltpu.bitcast(h_hi.astype(F32), U32)
        for s in range(PACK_ROWS):
            lo = bits[:, s * LANE:(s + 1) * LANE] >> 16
            hi = bits[:, (s + PACK_ROWS) * LANE:(s + PACK_ROWS + 1) * LANE] & jnp.uint32(0xFFFF0000)
            h2p_ref[pl.ds(r0 * PACK_ROWS + s, tm, stride=PACK_ROWS), :] = hi | lo


def _post_attention(xp, xs, onp, omp, ons, oms, mod, w_out, g_ffn, w_router, b_router, *, mod_row):
    tm = POST_TILE
    wr_hi = w_router.astype(BF16)
    wr_lo = (w_router - wr_hi.astype(F32)).astype(BF16)
    npt = xp.shape[0] // tm
    nst = xs.shape[0] // tm
    n = xp.shape[0] + xs.shape[0]
    pidx = lambda i: (jnp.minimum(i, npt - 1), 0)
    sidx = lambda i: (jnp.maximum(i - npt, 0), 0)
    row = lambda i: (i, 0)
    return pl.pallas_call(
        functools.partial(_post_kernel, n_prompt_tiles=npt),
        grid=(npt + nst,),
        in_specs=[pl.BlockSpec((tm, D_MODEL), pidx), pl.BlockSpec((tm, D_MODEL), sidx),
                  pl.BlockSpec((tm, NA_WIDTH), pidx), pl.BlockSpec((tm, MLA_WIDTH), pidx),
                  pl.BlockSpec((tm, NA_WIDTH), sidx), pl.BlockSpec((tm, MLA_WIDTH), sidx),
                  pl.BlockSpec((1, 6, D_MODEL), lambda i: (mod_row(i), 0, 0)),
                  _const_spec(w_out.shape), _const_spec((1, D_MODEL)),
                  _const_spec(w_router.shape), _const_spec(w_router.shape), _const_spec((1, N_EXPERTS))],
        out_specs=[pl.BlockSpec((tm, D_MODEL), row),
                   pl.BlockSpec((tm * PACK_ROWS, LANE), row),
                   pl.BlockSpec((tm, LANE), row),
                   pl.BlockSpec((tm, LANE), row)],
        out_shape=[jax.ShapeDtypeStruct((n, D_MODEL), F32),
                   jax.ShapeDtypeStruct((n * PACK_ROWS, LANE), U32),
                   jax.ShapeDtypeStruct((n, LANE), jnp.int32),
                   jax.ShapeDtypeStruct((n, LANE), F32)],
        compiler_params=_cparams(("arbitrary",)),
        name="post_attention_router",
    )(xp, xs, onp, omp, ons, oms, mod, w_out, g_ffn, wr_hi, wr_lo, b_router)


def _deinterleave_kernel(w_ref, p_ref, o_ref):
    p = p_ref[...]
    grp = p.shape[0]
    per_tile = o_ref.shape[3] // grp
    for c in range(w_ref.shape[2] // grp):
        w = w_ref[0, :, c * grp:(c + 1) * grp].astype(BF16)
        col = (c % per_tile) * grp
        o_ref[0, c // per_tile, :, col:col + grp] = _dot(w, p).astype(BF16)


def _deinterleave_gate_up(w_gate_up):
    ne, d, n2 = w_gate_up.shape
    nj = n2 // (2 * MOE_TF)
    grp = 2 * LANE
    dst = np.arange(grp)
    src = np.where(dst < LANE, 2 * dst, 2 * (dst - LANE) + 1)
    perm = np.zeros((grp, grp), np.float32)
    perm[src, dst] = 1.0
    tk = 512
    return pl.pallas_call(
        _deinterleave_kernel,
        grid=(ne, d // tk),
        in_specs=[pl.BlockSpec((1, tk, n2), lambda e, k: (e, k, 0)), _const_spec((grp, grp))],
        out_specs=pl.BlockSpec((1, nj, tk, 2 * MOE_TF), lambda e, k: (e, 0, k, 0)),
        out_shape=jax.ShapeDtypeStruct((ne, nj, d, 2 * MOE_TF), BF16),
        compiler_params=_cparams(("arbitrary", "arbitrary")),
        name="deinterleave_gate_up",
    )(w_gate_up, jnp.asarray(perm, BF16))


def _moe_kernel(ce_ref, nv_ref, cs_ref, order_ref, h_ref, wgu_ref, bgu_ref, wd_ref, bd_ref, y_ref,
                xg_ref, xbf_ref, acc_ref, wdb_ref, ost_ref, sem_g, sem_s):
    c = pl.program_id(0)
    j = pl.program_id(1)
    last_c = pl.num_programs(0) - 1
    last_j = pl.num_programs(1) - 1
    nk = order_ref.shape[0]
    nsub = MOE_CHUNK // MOE_SUB
    xg_rows = MOE_CHUNK * PACK_ROWS

    nv = nv_ref[c]
    c_next = jnp.minimum(c + 1, last_c)
    nv_next = jnp.where(c < last_c, nv_ref[c_next], 0)
    cs_next = cs_ref[c_next]
    c_prev = jnp.maximum(c - 1, 0)
    nv_prev = jnp.where(c > 0, nv_ref[c_prev], 0)
    cs_prev = cs_ref[c_prev]
    nv_prev2 = jnp.where(c > 1, nv_ref[jnp.maximum(c - 2, 0)], 0)
    ost_rows = MOE_CHUNK * OUT_ROWS
    slot = c % 2
    slot_next = (c + 1) % 2

    def gather_row(cs, dst_slot, row):
        tok = jnp.right_shift(order_ref[jnp.minimum(cs + row, nk - 1)], TOP_K_SHIFT)
        src = pl.multiple_of(tok * PACK_ROWS, PACK_ROWS)
        dst = pl.multiple_of(dst_slot * xg_rows + row * PACK_ROWS, PACK_ROWS)
        pltpu.make_async_copy(h_ref.at[pl.ds(src, PACK_ROWS)], xg_ref.at[pl.ds(dst, PACK_ROWS)],
                              sem_g.at[dst_slot]).start()

    def scatter_row(row):
        flat = order_ref[jnp.minimum(cs_prev + row, nk - 1)]
        dest = jnp.where(row < nv_prev, flat, nk + slot_next * MOE_CHUNK + row)
        src = pl.multiple_of(slot_next * ost_rows + row * OUT_ROWS, OUT_ROWS)
        dst = pl.multiple_of(dest * OUT_ROWS, OUT_ROWS)
        pltpu.make_async_copy(ost_ref.at[pl.ds(src, OUT_ROWS)], y_ref.at[pl.ds(dst, OUT_ROWS)],
                              sem_s.at[slot_next]).start()

    def rolled(fn, row0, nrows):
        def body(r0, carry):
            for u in range(DMA_UNROLL):
                fn(row0 + r0 * DMA_UNROLL + u)
            return carry

        lax.fori_loop(0, nrows // DMA_UNROLL, body, 0)

    @pl.when(jnp.logical_and(jnp.logical_and(c == 0, j == 0), nv > 0))
    def _():
        rolled(lambda row: gather_row(cs_ref[0], 0, row), 0, MOE_CHUNK)

    def unpack(sb):
        for s in range(PACK_ROWS):
            w = xg_ref[pl.ds(slot * xg_rows + sb * MOE_SUB * PACK_ROWS + s, MOE_SUB, stride=PACK_ROWS), :]
            lo = pltpu.bitcast(w << 16, F32)
            hi = pltpu.bitcast(w & jnp.uint32(0xFFFF0000), F32)
            xbf_ref[sb * MOE_SUB:(sb + 1) * MOE_SUB, s * LANE:(s + 1) * LANE] = lo.astype(BF16)
            xbf_ref[sb * MOE_SUB:(sb + 1) * MOE_SUB,
                    (s + PACK_ROWS) * LANE:(s + PACK_ROWS + 1) * LANE] = hi.astype(BF16)

    @pl.when(jnp.logical_and(j == 0, nv > 0))
    def _():
        base = pl.multiple_of(slot * xg_rows, xg_rows)
        pltpu.make_async_copy(h_ref.at[pl.ds(0, xg_rows)], xg_ref.at[pl.ds(base, xg_rows)], sem_g.at[slot]).wait()

        acc_ref[...] = jnp.broadcast_to(bd_ref[0], (MOE_CHUNK, D_MODEL))

        @pl.when(c == 0)
        def _():
            ost_ref[...] = jnp.zeros(ost_ref.shape, F32)

        @pl.when(nv == MOE_CHUNK)
        def _():
            for sb in range(nsub):
                unpack(sb)

        @pl.when(nv < MOE_CHUNK)
        def _():
            for sb in range(nsub):
                @pl.when(sb * MOE_SUB < nv)
                def _():
                    unpack(sb)

    def sub_block(sb):
        if sb == 0:
            wdb_ref[...] = wd_ref[0].astype(BF16)
        x = xbf_ref[sb * MOE_SUB:(sb + 1) * MOE_SUB, :]
        gu = _dot(x, wgu_ref[0, 0]) + bgu_ref[0]
        ng = MOE_TF // LANE
        g = jnp.concatenate([gu[:, 2 * b * LANE:(2 * b + 1) * LANE] for b in range(ng)], axis=-1)
        u = jnp.concatenate([gu[:, (2 * b + 1) * LANE:(2 * b + 2) * LANE] for b in range(ng)], axis=-1)
        g = jnp.minimum(g, SWIGLU_LIMIT)
        u = jnp.clip(u, -SWIGLU_LIMIT, SWIGLU_LIMIT)
        act = g * (1.0 / (1.0 + jnp.exp(-(g * SWIGLU_ALPHA)))) * (u + 1.0)
        acc_ref[sb * MOE_SUB:(sb + 1) * MOE_SUB, :] += _dot(act.astype(BF16), wdb_ref[...])

    dma_both = jnp.logical_and(nv_next > 0, nv_prev > 0)
    per_sb = MOE_SUB // nsub
    for sb in range(nsub):
        row0 = j * MOE_SUB + sb * per_sb
        compute = sb * MOE_SUB < nv
        fused = jnp.logical_and(compute, dma_both)

        @pl.when(fused)
        def _():
            for r in range(per_sb):
                gather_row(cs_next, slot_next, row0 + r)
                scatter_row(row0 + r)
            sub_block(sb)

        @pl.when(jnp.logical_not(fused))
        def _():
            @pl.when(nv_next > 0)
            def _():
                for r in range(per_sb):
                    gather_row(cs_next, slot_next, row0 + r)

            @pl.when(nv_prev > 0)
            def _():
                for r in range(per_sb):
                    scatter_row(row0 + r)

            @pl.when(compute)
            def _():
                sub_block(sb)

    def wait_scatter(buf):
        base = pl.multiple_of(buf * ost_rows, ost_rows)
        pltpu.make_async_copy(ost_ref.at[pl.ds(base, ost_rows)], y_ref.at[pl.ds(0, ost_rows)], sem_s.at[buf]).wait()

    @pl.when(j == last_j)
    def _():
        @pl.when(nv_prev2 > 0)
        def _():
            wait_scatter(slot)

        for sb in range(nsub):
            @pl.when(sb * MOE_SUB < nv)
            def _():
                for s in range(OUT_ROWS):
                    dst = pl.ds(slot * ost_rows + sb * MOE_SUB * OUT_ROWS + s, MOE_SUB, stride=OUT_ROWS)
                    ost_ref[dst, :] = acc_ref[sb * MOE_SUB:(sb + 1) * MOE_SUB, s * LANE:(s + 1) * LANE]

        @pl.when(jnp.logical_and(c == last_c, nv_prev > 0))
        def _():
            wait_scatter(slot_next)


def _moe(chunk_expert, chunk_rows, chunk_start, order, h2p, wgu, bgu, w_down, b_down):
    nj = D_FF // MOE_TF
    ngrid = chunk_expert.shape[0]
    nk = order.shape[0]

    def jj(c, j, nv):
        return jnp.where(nv[c] > 0, j, nj - 1)

    return pl.pallas_call(
        _moe_kernel,
        grid_spec=pltpu.PrefetchScalarGridSpec(
            num_scalar_prefetch=4,
            grid=(ngrid, nj),
            in_specs=[
                pl.BlockSpec(memory_space=pl.ANY),
                pl.BlockSpec((1, 1, D_MODEL, 2 * MOE_TF), lambda c, j, ce, nv, cs, od: (ce[c], jj(c, j, nv), 0, 0)),
                pl.BlockSpec((1, 1, 2 * MOE_TF), lambda c, j, ce, nv, cs, od: (ce[c], 0, jj(c, j, nv))),
                pl.BlockSpec((1, MOE_TF, D_MODEL), lambda c, j, ce, nv, cs, od: (ce[c], jj(c, j, nv), 0)),
                pl.BlockSpec((1, 1, D_MODEL), lambda c, j, ce, nv, cs, od: (ce[c], 0, 0))],
            out_specs=pl.BlockSpec(memory_space=pl.ANY),
            scratch_shapes=[pltpu.VMEM((2 * MOE_CHUNK * PACK_ROWS, LANE), U32),
                            pltpu.VMEM((MOE_CHUNK, D_MODEL), BF16),
                            pltpu.VMEM((MOE_CHUNK, D_MODEL), F32),
                            pltpu.VMEM((MOE_TF, D_MODEL), BF16),
                            pltpu.VMEM((2 * MOE_CHUNK * OUT_ROWS, LANE), F32),
                            pltpu.SemaphoreType.DMA((2,)),
                            pltpu.SemaphoreType.DMA((2,))]),
        out_shape=jax.ShapeDtypeStruct(((nk + 2 * MOE_CHUNK) * OUT_ROWS, LANE), F32),
        compiler_params=_cparams(("arbitrary", "arbitrary")),
        name="moe_experts",
    )(chunk_expert, chunk_rows, chunk_start, order, h2p, wgu, bgu, w_down, b_down)


def _combine_kernel(gate_ref, x1_ref, mod_ref, gfin_ref, ye_ref, y_ref):
    gate = gate_ref[...]
    pieces = []
    for s in range(OUT_ROWS):
        acc = None
        for k in range(TOP_K):
            rows = ye_ref[pl.ds(k * OUT_ROWS + s, COMBINE_TOKENS, stride=TOP_K * OUT_ROWS), :]
            term = gate[:, k:k + 1] * rows
            acc = term if acc is None else acc + term
        pieces.append(acc)
    y = jnp.concatenate(pieces, axis=-1)
    gf = mod_ref[0, 5:6, :]
    y_ref[...] = _rms(x1_ref[...] + gf * y, gfin_ref[...])


def _combine(gates, x1, mod, g_final, y_experts, *, tile0, ntiles, mod_row):
    tt = COMBINE_TOKENS
    return pl.pallas_call(
        _combine_kernel,
        grid=(ntiles,),
        in_specs=[pl.BlockSpec((tt, LANE), lambda i: (tile0 + i, 0)),
                  pl.BlockSpec((tt, D_MODEL), lambda i: (tile0 + i, 0)),
                  pl.BlockSpec((1, 6, D_MODEL), lambda i: (mod_row(i), 0, 0)),
                  _const_spec((1, D_MODEL)),
                  pl.BlockSpec((tt * TOP_K * OUT_ROWS, LANE), lambda i: (tile0 + i, 0))],
        out_specs=pl.BlockSpec((tt, D_MODEL), lambda i: (i, 0)),
        out_shape=jax.ShapeDtypeStruct((ntiles * tt, D_MODEL), F32),
        compiler_params=_cparams(("arbitrary",)),
        name="moe_combine_final_norm",
    )(gates, x1, mod, g_final, y_experts)


def _routing(top_idx, nch):
    n = top_idx.shape[0]
    nk = n * TOP_K
    flat_e = top_idx.reshape(nk)
    order = jnp.argsort(flat_e, stable=True).astype(jnp.int32)
    onehot = flat_e[:, None] == jnp.arange(N_EXPERTS, dtype=jnp.int32)[None, :]
    counts = jnp.sum(onehot.astype(jnp.int32), axis=0)
    grp_start = jnp.cumsum(counts) - counts
    chunks_e = (counts + MOE_CHUNK - 1) // MOE_CHUNK
    chunk_end = jnp.cumsum(chunks_e)
    chunk_start = chunk_end - chunks_e

    total = chunk_end[-1]
    cidx = jnp.arange(nch + 1, dtype=jnp.int32)
    active = cidx < total
    ce = jnp.minimum(jnp.searchsorted(chunk_end, cidx, side='right'), N_EXPERTS - 1).astype(jnp.int32)
    local = (cidx - chunk_start[ce]) * MOE_CHUNK
    nv = jnp.where(active, jnp.clip(counts[ce] - local, 0, MOE_CHUNK), 0).astype(jnp.int32)
    last = jnp.maximum(total - 1, 0)
    chunk_expert = jnp.where(active, ce, ce[last]).astype(jnp.int32)
    chunk_first = jnp.where(active, jnp.clip(grp_start[ce] + local, 0, nk - 1), 0).astype(jnp.int32)
    return order, chunk_expert, nv, chunk_first


def _rope_tables(t):
    pos = jnp.arange(t)
    rows = (pos // GRID_W).astype(F32)
    cols = (pos % GRID_W).astype(F32)
    inv = ROPE_THETA ** (-(jnp.arange(ROPE_AXIS // 2, dtype=F32) * 2.0 / ROPE_AXIS))
    ar = rows[:, None] * inv
    ac = cols[:, None] * inv
    ang = jnp.concatenate([ar, ar, ac, ac], axis=-1)
    return jnp.cos(ang), jnp.sin(ang)


def _rot_cols(w):
    half = ROPE_AXIS // 2
    src = np.concatenate([np.arange(half, ROPE_AXIS), np.arange(0, half),
                          np.arange(ROPE_AXIS + half, 2 * ROPE_AXIS), np.arange(ROPE_AXIS, ROPE_AXIS + half)])
    sign = np.concatenate([-np.ones(half), np.ones(half), -np.ones(half), np.ones(half)]).astype(np.float32)
    return w[..., src] * sign


def kernel(x_prompt, x_sample, cache_na_k, cache_na_v, cache_mla_ckv, cache_mla_krope, c, c_ctx, g_attn, g_ffn, g_final, w_mod, b_mod, w_in, w_out, na_rpb, g_q_a, w_q_b, g_kv_a, w_kv_b, w_router, b_router, w_gate_up, b_gate_up, w_down, b_down):
    bp, sp, d = x_prompt.shape
    bd, td, _ = x_sample.shape
    assert d == D_MODEL and w_mod.shape[0] == 1, "one trunk layer of width D_MODEL"
    n_p = bp * sp
    n_s = bd * td
    xp = x_prompt.reshape(n_p, d)
    xs = x_sample.reshape(n_s, d)

    c8 = jnp.zeros((8, d), F32).at[0].set(c_ctx).at[1:1 + bd].set(c)
    mod = _modulation(c8, w_mod[0], b_mod[0].reshape(1, -1)).reshape(8, 6, d)

    w_in0 = w_in[0]
    w_kr = w_in0[:, KR_OFF:KR_OFF + QK_ROPE]
    w_in_p = jnp.concatenate([w_in0, jnp.zeros((d, LANE - QK_ROPE), F32)], axis=1).astype(BF16)
    w_in_s = jnp.concatenate([w_in0, _rot_cols(w_kr)], axis=1).astype(BF16)
    wq = w_q_b[0].reshape(Q_LORA, MLA_HEADS, QK_NOPE + QK_ROPE)
    zpad = jnp.zeros((Q_LORA, MLA_HEADS, Q_PAD - QK_NOPE - QK_ROPE), F32)
    wq_pad = jnp.concatenate([wq, zpad], axis=-1).reshape(Q_LORA, MLA_HEADS * Q_PAD)
    wq_rot = jnp.concatenate([jnp.zeros((Q_LORA, MLA_HEADS, QK_NOPE), F32), _rot_cols(wq[..., QK_NOPE:]), zpad],
                             axis=-1).reshape(Q_LORA, MLA_HEADS * Q_PAD)
    wqb_p = wq_pad.astype(BF16)
    wqb_s = jnp.concatenate([wq_pad, wq_rot], axis=1).astype(BF16)
    w_kvb = w_kv_b[0].astype(BF16)
    cos, sin = _rope_tables(td)
    cosq = jnp.concatenate([jnp.ones((td, QK_NOPE), F32), cos, jnp.ones((td, Q_PAD - QK_NOPE - QK_ROPE), F32)], axis=1)
    sinq = jnp.concatenate([jnp.zeros((td, QK_NOPE), F32), sin, jnp.zeros((td, Q_PAD - QK_NOPE - QK_ROPE), F32)], axis=1)
    cosq = cosq * MLA_QSCALE
    sinq = sinq * MLA_QSCALE
    csk = jnp.concatenate([cos, sin], axis=1)

    tiles_per_seq = td // ROW_TILE
    g_attn2 = g_attn[0].reshape(1, d)
    gq2 = g_q_a[0].reshape(1, Q_LORA)
    gkv2 = g_kv_a[0].reshape(1, KV_LORA)

    naq_p, nak_p, nav_p, q_p, ckv_p, kr_p, krp_p = _pre_attention(
        xp, mod, g_attn2, w_in_p, gq2, wqb_p, gkv2, None, rope=False, mod_row=lambda i: 0)
    ona_p, omla_p = _prompt_attention(naq_p, nak_p, nav_p, q_p, ckv_p, krp_p, w_kvb, sp)

    naq_s, nak_s, nav_s, q_s, ckv_s, krp_s = _pre_attention(
        xs, mod, g_attn2, w_in_s, gq2, wqb_s, gkv2, (cosq, sinq, csk), rope=True,
        mod_row=lambda i: 1 + i // tiles_per_seq)
    past = cache_na_k.shape[2]
    kc = cache_na_k[:, 0].reshape(bd, past, NA_WIDTH)
    vc = cache_na_v[:, 0].reshape(bd, past, NA_WIDTH)
    bias = _na_bias_table(na_rpb[0] * LOG2E, td // GRID_W)
    ona_s = _sample_na(naq_s, nak_s, nav_s, kc, vc, bias, bd, td)
    ckv_all = jnp.concatenate([ckv_s.reshape(bd, td, KV_LORA), cache_mla_ckv[:, 0].astype(BF16)], axis=1)
    krp_c = jnp.concatenate([cache_mla_krope[:, 0], jnp.zeros((bd, past, LANE - QK_ROPE), F32)], axis=-1).astype(BF16)
    krp_all = jnp.concatenate([krp_s.reshape(bd, td, LANE), krp_c], axis=1)
    kf, vv = _kv_expand(ckv_all, krp_all, w_kvb)
    omla_s = _sample_mla(q_s, kf, vv, bd, td)

    npt = n_p // POST_TILE
    post_tiles_per_seq = td // POST_TILE
    x1, h2p, idx128, gate128 = _post_attention(
        xp, xs, ona_p, omla_p, ona_s, omla_s, mod, w_out[0].astype(BF16), g_ffn[0].reshape(1, d),
        w_router[0], b_router[0].reshape(1, N_EXPERTS),
        mod_row=lambda i: jnp.where(i < npt, 0, 1 + jnp.maximum(i - npt, 0) // post_tiles_per_seq))

    n = n_p + n_s
    nch = n * TOP_K // MOE_CHUNK + N_EXPERTS
    order, chunk_expert, chunk_rows, chunk_first = _routing(idx128[:, :TOP_K], nch)
    wgu = _deinterleave_gate_up(w_gate_up[0])
    ng = 2 * D_FF // (2 * LANE)
    bgu = b_gate_up[0].reshape(N_EXPERTS, ng, LANE, 2).transpose(0, 1, 3, 2).reshape(N_EXPERTS, 1, 2 * D_FF)
    y_experts = _moe(chunk_expert, chunk_rows, chunk_first, order, h2p, wgu, bgu, w_down[0],
                     b_down[0].reshape(N_EXPERTS, 1, d))

    tt = COMBINE_TOKENS
    gfin = g_final.reshape(1, d)
    ctiles_seq = td // tt
    y_p = _combine(gate128, x1, mod, gfin, y_experts, tile0=0, ntiles=n_p // tt, mod_row=lambda i: 0)
    y_s = _combine(gate128, x1, mod, gfin, y_experts, tile0=n_p // tt, ntiles=n_s // tt,
                   mod_row=lambda i: 1 + i // ctiles_seq)

    return (y_p.reshape(bp, sp, d), y_s.reshape(bd, td, d),
            nak_p.reshape(bp, 1, sp, NA_HEADS, NA_HEAD_DIM), nav_p.reshape(bp, 1, sp, NA_HEADS, NA_HEAD_DIM),
            ckv_p.reshape(bp, 1, sp, KV_LORA), kr_p.reshape(bp, 1, sp, QK_ROPE))
```

```python
import functools

import numpy as np
import jax
import jax.numpy as jnp
from jax import lax
from jax.experimental import pallas as pl
from jax.experimental.pallas import tpu as pltpu

F32 = jnp.float32
BF16 = jnp.bfloat16
U32 = jnp.uint32

D_MODEL = 2048
GRID_W = 64
NA_HEADS = 8
NA_HEAD_DIM = 128
NA_KR = 8
NA_KC = 16
MLA_HEADS = 8
Q_LORA = 512
KV_LORA = 256
QK_NOPE = 128
QK_ROPE = 64
V_DIM = 128
ROPE_AXIS = QK_ROPE // 2
ROPE_THETA = 10000.0
NA_WIDTH = NA_HEADS * NA_HEAD_DIM
MLA_WIDTH = MLA_HEADS * V_DIM
IN_COLS = 3 * NA_WIDTH + Q_LORA + KV_LORA + QK_ROPE
N_EXPERTS = 32
TOP_K = 4
D_FF = D_MODEL
SWIGLU_ALPHA = 1.702
SWIGLU_LIMIT = 7.0
EPS = 1e-6

LANE = 128
Q_PAD = 2 * LANE
KR_OFF = 3 * NA_WIDTH + Q_LORA + KV_LORA
IN_COLS_PAD = KR_OFF + LANE
VMEM_LIMIT = 56 * 1024 * 1024
NEG_BIG = -1e30

LOG2E = 1.4426950408889634
NA_QSCALE = NA_HEAD_DIM ** -0.5 * LOG2E
MLA_QSCALE = (QK_NOPE + QK_ROPE) ** -0.5 * LOG2E

ROW_TILE = 256
POST_TILE = 2 * ROW_TILE
NA_Q_ROWS = 4
NA_WIN_ROWS = 12
NA_HEADS_PER_STEP = 4
MLA_TQ = 512
MLA_TK = 2304
MOE_CHUNK = 1024
MOE_SUB = 256
MOE_TF = 512
COMBINE_TOKENS = 256
TOP_K_SHIFT = TOP_K.bit_length() - 1
DMA_UNROLL = 8
PACK_ROWS = D_MODEL // (2 * LANE)
OUT_ROWS = D_MODEL // LANE


def _cparams(sem):
    return pltpu.CompilerParams(dimension_semantics=sem, vmem_limit_bytes=VMEM_LIMIT)


def _const_spec(shape):
    nd = len(shape)
    return pl.BlockSpec(shape, lambda *a: (0,) * nd, pipeline_mode=pl.Buffered(1))


def _rms(x, g):
    return x * lax.rsqrt(jnp.mean(x * x, axis=-1, keepdims=True) + EPS) * g


def _dot(a, b):
    return jnp.dot(a, b, preferred_element_type=F32)


def _dot_nt(a, b):
    return lax.dot_general(a, b, (((1,), (1,)), ((), ())), preferred_element_type=F32)


def _mod_kernel(c_ref, w_ref, b_ref, o_ref):
    c = c_ref[...]
    s = c / (1.0 + jnp.exp(-c))
    o_ref[...] = _dot(s.astype(BF16), w_ref[...].astype(BF16)) + b_ref[...]


def _modulation(c8, w_mod, b_mod):
    n = w_mod.shape[1]
    tn = 1024
    return pl.pallas_call(
        _mod_kernel,
        grid=(n // tn,),
        in_specs=[pl.BlockSpec((8, D_MODEL), lambda j: (0, 0)),
                  pl.BlockSpec((D_MODEL, tn), lambda j: (0, j)),
                  pl.BlockSpec((1, tn), lambda j: (0, j))],
        out_specs=pl.BlockSpec((8, tn), lambda j: (0, j)),
        out_shape=jax.ShapeDtypeStruct((8, n), F32),
        compiler_params=_cparams(("arbitrary",)),
        name="modulation",
    )(c8, w_mod, b_mod)


def _pre_kernel(*refs, rope):
    if rope:
        (x_ref, mod_ref, g_ref, win_ref, gq_ref, wqb_ref, gkv_ref, cosq_ref, sinq_ref, csk_ref,
         naq_ref, nak_ref, nav_ref, q_ref, ckv_ref, krp_ref) = refs
    else:
        (x_ref, mod_ref, g_ref, win_ref, gq_ref, wqb_ref, gkv_ref,
         naq_ref, nak_ref, nav_ref, q_ref, ckv_ref, kr_ref, krp_ref) = refs
    x = x_ref[...]
    sa = mod_ref[0, 0:1, :]
    sca = mod_ref[0, 1:2, :]
    h = (_rms(x, g_ref[...]) * (1.0 + sca) + sa).astype(BF16)
    proj = _dot(h, win_ref[...])
    naq_ref[...] = (proj[:, 0:NA_WIDTH] * NA_QSCALE).astype(naq_ref.dtype)
    nak_ref[...] = proj[:, NA_WIDTH:2 * NA_WIDTH].astype(nak_ref.dtype)
    nav_ref[...] = proj[:, 2 * NA_WIDTH:3 * NA_WIDTH].astype(nav_ref.dtype)
    q_a = proj[:, 3 * NA_WIDTH:3 * NA_WIDTH + Q_LORA]
    kv_a = proj[:, 3 * NA_WIDTH + Q_LORA:KR_OFF]
    krx = proj[:, KR_OFF:IN_COLS_PAD]
    qan = _rms(q_a, gq_ref[...]).astype(BF16)
    qq = _dot(qan, wqb_ref[...])
    ckv_ref[...] = _rms(kv_a, gkv_ref[...]).astype(ckv_ref.dtype)
    if rope:
        width = MLA_HEADS * Q_PAD
        cosq = cosq_ref[...]
        sinq = sinq_ref[...]
        for hd in range(MLA_HEADS):
            a = qq[:, hd * Q_PAD:(hd + 1) * Q_PAD]
            b = qq[:, width + hd * Q_PAD:width + (hd + 1) * Q_PAD]
            q_ref[:, hd * Q_PAD:(hd + 1) * Q_PAD] = (a * cosq + b * sinq).astype(BF16)
        y = krx * csk_ref[...]
        y = y + pltpu.roll(y, QK_ROPE, 1)
        lane = lax.broadcasted_iota(jnp.int32, y.shape, 1)
        krp_ref[...] = jnp.where(lane < QK_ROPE, y, 0.0).astype(BF16)
    else:
        q_ref[...] = (qq * MLA_QSCALE).astype(BF16)
        kr_ref[...] = krx[:, 0:QK_ROPE]
        krp_ref[...] = krx.astype(BF16)


def _pre_attention(x, mod, g_attn, w_in, g_q_a, w_qb, g_kv_a, rope_tabs, *, rope, mod_row):
    n = x.shape[0]
    tm = ROW_TILE
    row = lambda i: (i, 0)
    in_specs = [pl.BlockSpec((tm, D_MODEL), row),
                pl.BlockSpec((1, 6, D_MODEL), lambda i: (mod_row(i), 0, 0)),
                _const_spec((1, D_MODEL)),
                _const_spec(w_in.shape),
                _const_spec((1, Q_LORA)),
                _const_spec(w_qb.shape),
                _const_spec((1, KV_LORA))]
    args = [x, mod, g_attn, w_in, g_q_a, w_qb, g_kv_a]
    qw = MLA_HEADS * Q_PAD
    if rope:
        tiles_per_seq = rope_tabs[0].shape[0] // tm
        pos = lambda i: (i % tiles_per_seq, 0)
        in_specs += [pl.BlockSpec((tm, Q_PAD), pos), pl.BlockSpec((tm, Q_PAD), pos),
                     pl.BlockSpec((tm, LANE), pos)]
        args += list(rope_tabs)
        out_shape = [jax.ShapeDtypeStruct((n, NA_WIDTH), BF16)] * 3 + [
            jax.ShapeDtypeStruct((n, qw), BF16),
            jax.ShapeDtypeStruct((n, KV_LORA), BF16),
            jax.ShapeDtypeStruct((n, LANE), BF16)]
        out_specs = [pl.BlockSpec((tm, NA_WIDTH), row)] * 3 + [
            pl.BlockSpec((tm, qw), row), pl.BlockSpec((tm, KV_LORA), row), pl.BlockSpec((tm, LANE), row)]
    else:
        out_shape = [jax.ShapeDtypeStruct((n, NA_WIDTH), BF16),
                     jax.ShapeDtypeStruct((n, NA_WIDTH), F32),
                     jax.ShapeDtypeStruct((n, NA_WIDTH), F32),
                     jax.ShapeDtypeStruct((n, qw), BF16),
                     jax.ShapeDtypeStruct((n, KV_LORA), F32),
                     jax.ShapeDtypeStruct((n, QK_ROPE), F32),
                     jax.ShapeDtypeStruct((n, LANE), BF16)]
        out_specs = [pl.BlockSpec((tm, NA_WIDTH), row)] * 3 + [
            pl.BlockSpec((tm, qw), row), pl.BlockSpec((tm, KV_LORA), row),
            pl.BlockSpec((tm, QK_ROPE), row), pl.BlockSpec((tm, LANE), row)]
    return pl.pallas_call(
        functools.partial(_pre_kernel, rope=rope),
        grid=(n // tm,),
        in_specs=in_specs,
        out_specs=out_specs,
        out_shape=out_shape,
        compiler_params=_cparams(("arbitrary",)),
        name="pre_attention_rope" if rope else "pre_attention",
    )(*args)


def _softmax_pv(s, v):
    m = jnp.max(s, axis=-1, keepdims=True)
    p = jnp.exp2(s - m)
    l = jnp.sum(p, axis=-1, keepdims=True)
    return _dot(p.astype(BF16), v) / l


def _prompt_attn_kernel(naq_ref, nak_ref, nav_ref, q_ref, ckv_ref, krp_ref, wkvb_ref, ona_ref, omla_ref):
    kv = _dot(ckv_ref[...].astype(BF16), wkvb_ref[...])
    krp = krp_ref[...]
    for hd in range(NA_HEADS):
        sl = slice(hd * NA_HEAD_DIM, (hd + 1) * NA_HEAD_DIM)
        s = _dot_nt(naq_ref[:, sl], nak_ref[:, sl].astype(BF16))
        ona_ref[:, sl] = _softmax_pv(s, nav_ref[:, sl].astype(BF16)).astype(BF16)
    kvw = QK_NOPE + V_DIM
    for hd in range(MLA_HEADS):
        kf = jnp.concatenate([kv[:, hd * kvw:hd * kvw + QK_NOPE].astype(BF16), krp], axis=-1)
        s = _dot_nt(q_ref[:, hd * Q_PAD:(hd + 1) * Q_PAD], kf)
        v = kv[:, hd * kvw + QK_NOPE:(hd + 1) * kvw].astype(BF16)
        omla_ref[:, hd * V_DIM:(hd + 1) * V_DIM] = _softmax_pv(s, v).astype(BF16)


def _prompt_attention(naq, nak, nav, q, ckv, krp, w_kvb, seq):
    n = naq.shape[0]
    row = lambda b: (b, 0)
    return pl.pallas_call(
        _prompt_attn_kernel,
        grid=(n // seq,),
        in_specs=[pl.BlockSpec((seq, NA_WIDTH), row)] * 3 + [
            pl.BlockSpec((seq, MLA_HEADS * Q_PAD), row),
            pl.BlockSpec((seq, KV_LORA), row),
            pl.BlockSpec((seq, LANE), row),
            _const_spec(w_kvb.shape)],
        out_specs=[pl.BlockSpec((seq, NA_WIDTH), row), pl.BlockSpec((seq, MLA_WIDTH), row)],
        out_shape=[jax.ShapeDtypeStruct((n, NA_WIDTH), BF16), jax.ShapeDtypeStruct((n, MLA_WIDTH), BF16)],
        compiler_params=_cparams(("arbitrary",)),
        name="prompt_attention",
    )(naq, nak, nav, q, ckv, krp, w_kvb)


def _na_kernel(q_ref, k_ref, v_ref, kc_ref, vc_ref, bias_ref, o_ref, *, rows):
    rt = pl.program_id(2)
    ws = jnp.clip(rt * NA_Q_ROWS - NA_KR // 2, 0, rows - NA_WIN_ROWS)
    start = pl.multiple_of(ws * GRID_W, GRID_W)
    nwin = NA_WIN_ROWS * GRID_W
    for hh in range(NA_HEADS_PER_STEP):
        hs = slice(hh * NA_HEAD_DIM, (hh + 1) * NA_HEAD_DIM)
        q = q_ref[:, hs]
        s_loc = _dot_nt(q, k_ref[pl.ds(start, nwin), hs]) + bias_ref[0, hh]
        s_ctx = _dot_nt(q, kc_ref[0, :, hs].astype(BF16))
        m = jnp.maximum(jnp.max(s_loc, axis=-1, keepdims=True), jnp.max(s_ctx, axis=-1, keepdims=True))
        p_loc = jnp.exp2(s_loc - m)
        p_ctx = jnp.exp2(s_ctx - m)
        l = jnp.sum(p_loc, axis=-1, keepdims=True) + jnp.sum(p_ctx, axis=-1, keepdims=True)
        o = (_dot(p_loc.astype(BF16), v_ref[pl.ds(start, nwin), hs])
             + _dot(p_ctx.astype(BF16), vc_ref[0, :, hs].astype(BF16)))
        o_ref[:, hs] = (o / l).astype(BF16)


def _na_bias_table(rpb, rows):
    nh = rpb.shape[0]
    a = np.arange(NA_Q_ROWS)
    b = np.arange(NA_WIN_ROWS)
    col = np.arange(GRID_W)
    cs = np.clip(col - NA_KC // 2, 0, GRID_W - NA_KC)
    valid_col = (col[None, :] >= cs[:, None]) & (col[None, :] < cs[:, None] + NA_KC)
    padw = GRID_W - NA_KC
    rp = jnp.pad(rpb, ((0, 0), (0, 0), (padw, padw)))
    toep = jnp.stack([rp[:, :, GRID_W - 1 - qc:2 * GRID_W - 1 - qc] for qc in range(GRID_W)], axis=2)
    toep = jnp.where(valid_col[None, None], toep, NEG_BIG)
    masked = jnp.full((nh, GRID_W, GRID_W), NEG_BIG, F32)
    tabs = []
    for r0 in (0, NA_Q_ROWS, rows - NA_Q_ROWS):
        ws = int(np.clip(r0 - NA_KR // 2, 0, rows - NA_WIN_ROWS))
        r = r0 + a
        rs = np.clip(r - NA_KR // 2, 0, rows - NA_KR)
        kr = ws + b
        valid_row = (kr[None, :] >= rs[:, None]) & (kr[None, :] < rs[:, None] + NA_KR)
        dr = kr[None, :] - r[:, None] + NA_KR - 1
        tile_rows = []
        for ai in range(NA_Q_ROWS):
            blocks = [toep[:, int(dr[ai, bi])] if valid_row[ai, bi] else masked for bi in range(NA_WIN_ROWS)]
            tile_rows.append(jnp.concatenate(blocks, axis=-1))
        tabs.append(jnp.concatenate(tile_rows, axis=1))
    return jnp.stack(tabs)


def _sample_na(naq, nak, nav, kc, vc, bias, batch, seq):
    rows = seq // GRID_W
    tq = NA_Q_ROWS * GRID_W
    nt = seq // tq
    last = nt - 1
    hw = NA_HEADS_PER_STEP * NA_HEAD_DIM

    def pat(rt):
        return jnp.where(rt == 0, 0, jnp.where(rt == last, 2, 1))

    return pl.pallas_call(
        functools.partial(_na_kernel, rows=rows),
        grid=(batch, NA_HEADS // NA_HEADS_PER_STEP, nt),
        in_specs=[pl.BlockSpec((tq, hw), lambda b, h, r: (b * nt + r, h)),
                  pl.BlockSpec((seq, hw), lambda b, h, r: (b, h)),
                  pl.BlockSpec((seq, hw), lambda b, h, r: (b, h)),
                  pl.BlockSpec((1, kc.shape[1], hw), lambda b, h, r: (b, 0, h)),
                  pl.BlockSpec((1, vc.shape[1], hw), lambda b, h, r: (b, 0, h)),
                  pl.BlockSpec((1, NA_HEADS_PER_STEP, tq, NA_WIN_ROWS * GRID_W), lambda b, h, r: (pat(r), h, 0, 0))],
        out_specs=pl.BlockSpec((tq, hw), lambda b, h, r: (b * nt + r, h)),
        out_shape=jax.ShapeDtypeStruct((batch * seq, NA_WIDTH), BF16),
        compiler_params=_cparams(("arbitrary", "arbitrary", "arbitrary")),
        name="sample_neighbourhood_attention",
    )(naq, nak, nav, kc, vc, bias)


def _kv_expand_kernel(ckv_ref, krp_ref, w_ref, kf_ref, v_ref):
    kv = _dot(ckv_ref[0], w_ref[...])
    krp = krp_ref[0]
    kvw = QK_NOPE + V_DIM
    for hd in range(MLA_HEADS):
        kf_ref[0, hd, :, 0:QK_NOPE] = kv[:, hd * kvw:hd * kvw + QK_NOPE].astype(BF16)
        kf_ref[0, hd, :, QK_NOPE:Q_PAD] = krp
        v_ref[0, hd] = kv[:, hd * kvw + QK_NOPE:(hd + 1) * kvw].astype(BF16)


def _kv_expand(ckv, krp, w_kvb):
    batch, nkeys, _ = ckv.shape
    tm = 512
    return pl.pallas_call(
        _kv_expand_kernel,
        grid=(batch, nkeys // tm),
        in_specs=[pl.BlockSpec((1, tm, KV_LORA), lambda b, t: (b, t, 0)),
                  pl.BlockSpec((1, tm, LANE), lambda b, t: (b, t, 0)),
                  _const_spec(w_kvb.shape)],
        out_specs=[pl.BlockSpec((1, MLA_HEADS, tm, Q_PAD), lambda b, t: (b, 0, t, 0)),
                   pl.BlockSpec((1, MLA_HEADS, tm, V_DIM), lambda b, t: (b, 0, t, 0))],
        out_shape=[jax.ShapeDtypeStruct((batch, MLA_HEADS, nkeys, Q_PAD), BF16),
                   jax.ShapeDtypeStruct((batch, MLA_HEADS, nkeys, V_DIM), BF16)],
        compiler_params=_cparams(("arbitrary", "arbitrary")),
        name="latent_kv_expand",
    )(ckv, krp, w_kvb)


def _mla_kernel(q_ref, kf_ref, v_ref, o_ref, *, nkeys):
    q = q_ref[...]
    tq = q.shape[0]
    m = jnp.full((tq, 1), NEG_BIG, F32)
    l = jnp.zeros((tq, 1), F32)
    acc = jnp.zeros((tq, V_DIM), F32)
    for c in range(nkeys // MLA_TK):
        ks = slice(c * MLA_TK, (c + 1) * MLA_TK)
        s = _dot_nt(q, kf_ref[0, 0, ks, :])
        m_new = jnp.maximum(m, jnp.max(s, axis=-1, keepdims=True))
        alpha = jnp.exp2(m - m_new)
        p = jnp.exp2(s - m_new)
        l = alpha * l + jnp.sum(p, axis=-1, keepdims=True)
        acc = alpha * acc + _dot(p.astype(BF16), v_ref[0, 0, ks, :])
        m = m_new
    o_ref[...] = (acc / l).astype(BF16)


def _sample_mla(q, kf, v, batch, seq):
    nkeys = kf.shape[2]
    nt = seq // MLA_TQ
    assert nkeys % MLA_TK == 0, "latent key count must be a whole number of softmax passes"
    return pl.pallas_call(
        functools.partial(_mla_kernel, nkeys=nkeys),
        grid=(batch, MLA_HEADS, nt),
        in_specs=[pl.BlockSpec((MLA_TQ, Q_PAD), lambda b, h, t: (b * nt + t, h)),
                  pl.BlockSpec((1, 1, nkeys, Q_PAD), lambda b, h, t: (b, h, 0, 0)),
                  pl.BlockSpec((1, 1, nkeys, V_DIM), lambda b, h, t: (b, h, 0, 0))],
        out_specs=pl.BlockSpec((MLA_TQ, V_DIM), lambda b, h, t: (b * nt + t, h)),
        out_shape=jax.ShapeDtypeStruct((batch * seq, MLA_WIDTH), BF16),
        compiler_params=_cparams(("arbitrary", "arbitrary", "arbitrary")),
        name="sample_latent_attention",
    )(q, kf, v)


def _post_kernel(xp_ref, xs_ref, onp_ref, omp_ref, ons_ref, oms_ref, mod_ref, wout_ref, gffn_ref, wrh_ref, wrl_ref,
                 br_ref, x1_ref, h2p_ref, idx_ref, gate_ref, *, n_prompt_tiles):
    is_prompt = pl.program_id(0) < n_prompt_tiles
    ga = mod_ref[0, 2:3, :]
    sf = mod_ref[0, 3:4, :]
    scf = mod_ref[0, 4:5, :]
    tm = ROW_TILE
    for r0 in range(0, xp_ref.shape[0], tm):
        rs = slice(r0, r0 + tm)
        x = jnp.where(is_prompt, xp_ref[rs, :], xs_ref[rs, :])
        ona = jnp.where(is_prompt, onp_ref[rs, :], ons_ref[rs, :])
        omla = jnp.where(is_prompt, omp_ref[rs, :], oms_ref[rs, :])
        o = _dot(ona, wout_ref[0:NA_WIDTH, :]) + _dot(omla, wout_ref[NA_WIDTH:NA_WIDTH + MLA_WIDTH, :])
        x1 = x + ga * o
        x1_ref[rs, :] = x1
        h2 = _rms(x1, gffn_ref[...]) * (1.0 + scf) + sf
        h_hi = h2.astype(BF16)
        h_lo = (h2 - h_hi.astype(F32)).astype(BF16)
        logits = _dot(h_hi, wrh_ref[...]) + _dot(h_lo, wrh_ref[...]) + _dot(h_hi, wrl_ref[...]) + br_ref[...]
        lane_e = lax.broadcasted_iota(jnp.int32, logits.shape, 1).astype(F32)
        lane_o = lax.broadcasted_iota(jnp.int32, (tm, LANE), 1)
        idx_out = jnp.zeros((tm, LANE), F32)
        gate_out = jnp.zeros((tm, LANE), F32)
        top0 = None
        denom = jnp.zeros((tm, 1), F32)
        cur = logits
        for k in range(TOP_K):
            mx = jnp.max(cur, axis=-1, keepdims=True)
            ix = jnp.min(jnp.where(cur == mx, lane_e, float(N_EXPERTS)), axis=-1, keepdims=True)
            cur = jnp.where(lane_e == ix, -jnp.inf, cur)
            if k == 0:
                top0 = mx
            e = jnp.exp(mx - top0)
            denom = denom + e
            idx_out = jnp.where(lane_o == k, ix, idx_out)
            gate_out = jnp.where(lane_o == k, e, gate_out)
        idx_ref[rs, :] = idx_out.astype(jnp.int32)
        gate_ref[rs, :] = gate_out / denom
        bits = pltpu.bitcast(h_hi.astype(F32), U32)
        for s in range(PACK_ROWS):
            lo = bits[:, s * LANE:(s + 1) * LANE] >> 16
            hi = bits[:, (s + PACK_ROWS) * LANE:(s + PACK_ROWS + 1) * LANE] & jnp.uint32(0xFFFF0000)
            h2p_ref[pl.ds(r0 * PACK_ROWS + s, tm, stride=PACK_ROWS), :] = hi | lo


def _post_attention(xp, xs, onp, omp, ons, oms, mod, w_out, g_ffn, w_router, b_router, *, mod_row):
    tm = POST_TILE
    wr_hi = w_router.astype(BF16)
    wr_lo = (w_router - wr_hi.astype(F32)).astype(BF16)
    npt = xp.shape[0] // tm
    nst = xs.shape[0] // tm
    n = xp.shape[0] + xs.shape[0]
    pidx = lambda i: (jnp.minimum(i, npt - 1), 0)
    sidx = lambda i: (jnp.maximum(i - npt, 0), 0)
    row = lambda i: (i, 0)
    return pl.pallas_call(
        functools.partial(_post_kernel, n_prompt_tiles=npt),
        grid=(npt + nst,),
        in_specs=[pl.BlockSpec((tm, D_MODEL), pidx), pl.BlockSpec((tm, D_MODEL), sidx),
                  pl.BlockSpec((tm, NA_WIDTH), pidx), pl.BlockSpec((tm, MLA_WIDTH), pidx),
                  pl.BlockSpec((tm, NA_WIDTH), sidx), pl.BlockSpec((tm, MLA_WIDTH), sidx),
                  pl.BlockSpec((1, 6, D_MODEL), lambda i: (mod_row(i), 0, 0)),
                  _const_spec(w_out.shape), _const_spec((1, D_MODEL)),
                  _const_spec(w_router.shape), _const_spec(w_router.shape), _const_spec((1, N_EXPERTS))],
        out_specs=[pl.BlockSpec((tm, D_MODEL), row),
                   pl.BlockSpec((tm * PACK_ROWS, LANE), row),
                   pl.BlockSpec((tm, LANE), row),
                   pl.BlockSpec((tm, LANE), row)],
        out_shape=[jax.ShapeDtypeStruct((n, D_MODEL), F32),
                   jax.ShapeDtypeStruct((n * PACK_ROWS, LANE), U32),
                   jax.ShapeDtypeStruct((n, LANE), jnp.int32),
                   jax.ShapeDtypeStruct((n, LANE), F32)],
        compiler_params=_cparams(("arbitrary",)),
        name="post_attention_router",
    )(xp, xs, onp, omp, ons, oms, mod, w_out, g_ffn, wr_hi, wr_lo, b_router)


def _deinterleave_kernel(w_ref, p_ref, o_ref):
    p = p_ref[...]
    grp = p.shape[0]
    per_tile = o_ref.shape[3] // grp
    for c in range(w_ref.shape[2] // grp):
        w = w_ref[0, :, c * grp:(c + 1) * grp].astype(BF16)
        col = (c % per_tile) * grp
        o_ref[0, c // per_tile, :, col:col + grp] = _dot(w, p).astype(BF16)


def _deinterleave_gate_up(w_gate_up):
    ne, d, n2 = w_gate_up.shape
    nj = n2 // (2 * MOE_TF)
    grp = 2 * LANE
    dst = np.arange(grp)
    src = np.where(dst < LANE, 2 * dst, 2 * (dst - LANE) + 1)
    perm = np.zeros((grp, grp), np.float32)
    perm[src, dst] = 1.0
    tk = 512
    return pl.pallas_call(
        _deinterleave_kernel,
        grid=(ne, d // tk),
        in_specs=[pl.BlockSpec((1, tk, n2), lambda e, k: (e, k, 0)), _const_spec((grp, grp))],
        out_specs=pl.BlockSpec((1, nj, tk, 2 * MOE_TF), lambda e, k: (e, 0, k, 0)),
        out_shape=jax.ShapeDtypeStruct((ne, nj, d, 2 * MOE_TF), BF16),
        compiler_params=_cparams(("arbitrary", "arbitrary")),
        name="deinterleave_gate_up",
    )(w_gate_up, jnp.asarray(perm, BF16))


def _moe_kernel(ce_ref, nv_ref, cs_ref, order_ref, h_ref, wgu_ref, bgu_ref, wd_ref, bd_ref, y_ref,
                xg_ref, xbf_ref, acc_ref, wdb_ref, ost_ref, sem_g, sem_s):
    c = pl.program_id(0)
    j = pl.program_id(1)
    last_c = pl.num_programs(0) - 1
    last_j = pl.num_programs(1) - 1
    nk = order_ref.shape[0]
    nsub = MOE_CHUNK // MOE_SUB
    xg_rows = MOE_CHUNK * PACK_ROWS

    nv = nv_ref[c]
    c_next = jnp.minimum(c + 1, last_c)
    nv_next = jnp.where(c < last_c, nv_ref[c_next], 0)
    cs_next = cs_ref[c_next]
    c_prev = jnp.maximum(c - 1, 0)
    nv_prev = jnp.where(c > 0, nv_ref[c_prev], 0)
    cs_prev = cs_ref[c_prev]
    nv_prev2 = jnp.where(c > 1, nv_ref[jnp.maximum(c - 2, 0)], 0)
    ost_rows = MOE_CHUNK * OUT_ROWS
    slot = c % 2
    slot_next = (c + 1) % 2

    def gather_row(cs, dst_slot, row):
        tok = jnp.right_shift(order_ref[jnp.minimum(cs + row, nk - 1)], TOP_K_SHIFT)
        src = pl.multiple_of(tok * PACK_ROWS, PACK_ROWS)
        dst = pl.multiple_of(dst_slot * xg_rows + row * PACK_ROWS, PACK_ROWS)
        pltpu.make_async_copy(h_ref.at[pl.ds(src, PACK_ROWS)], xg_ref.at[pl.ds(dst, PACK_ROWS)],
                              sem_g.at[dst_slot]).start()

    def scatter_row(row):
        flat = order_ref[jnp.minimum(cs_prev + row, nk - 1)]
        dest = jnp.where(row < nv_prev, flat, nk + slot_next * MOE_CHUNK + row)
        src = pl.multiple_of(slot_next * ost_rows + row * OUT_ROWS, OUT_ROWS)
        dst = pl.multiple_of(dest * OUT_ROWS, OUT_ROWS)
        pltpu.make_async_copy(ost_ref.at[pl.ds(src, OUT_ROWS)], y_ref.at[pl.ds(dst, OUT_ROWS)],
                              sem_s.at[slot_next]).start()

    def rolled(fn, row0, nrows):
        def body(r0, carry):
            for u in range(DMA_UNROLL):
                fn(row0 + r0 * DMA_UNROLL + u)
            return carry

        lax.fori_loop(0, nrows // DMA_UNROLL, body, 0)

    @pl.when(jnp.logical_and(jnp.logical_and(c == 0, j == 0), nv > 0))
    def _():
        rolled(lambda row: gather_row(cs_ref[0], 0, row), 0, MOE_CHUNK)

    def unpack(sb):
        for s in range(PACK_ROWS):
            w = xg_ref[pl.ds(slot * xg_rows + sb * MOE_SUB * PACK_ROWS + s, MOE_SUB, stride=PACK_ROWS), :]
            lo = pltpu.bitcast(w << 16, F32)
            hi = pltpu.bitcast(w & jnp.uint32(0xFFFF0000), F32)
            xbf_ref[sb * MOE_SUB:(sb + 1) * MOE_SUB, s * LANE:(s + 1) * LANE] = lo.astype(BF16)
            xbf_ref[sb * MOE_SUB:(sb + 1) * MOE_SUB,
                    (s + PACK_ROWS) * LANE:(s + PACK_ROWS + 1) * LANE] = hi.astype(BF16)

    @pl.when(jnp.logical_and(j == 0, nv > 0))
    def _():
        base = pl.multiple_of(slot * xg_rows, xg_rows)
        pltpu.make_async_copy(h_ref.at[pl.ds(0, xg_rows)], xg_ref.at[pl.ds(base, xg_rows)], sem_g.at[slot]).wait()

        acc_ref[...] = jnp.broadcast_to(bd_ref[0], (MOE_CHUNK, D_MODEL))

        @pl.when(c == 0)
        def _():
            ost_ref[...] = jnp.zeros(ost_ref.shape, F32)

        @pl.when(nv == MOE_CHUNK)
        def _():
            for sb in range(nsub):
                unpack(sb)

        @pl.when(nv < MOE_CHUNK)
        def _():
            for sb in range(nsub):
                @pl.when(sb * MOE_SUB < nv)
                def _():
                    unpack(sb)

    def sub_block(sb):
        if sb == 0:
            wdb_ref[...] = wd_ref[0].astype(BF16)
        x = xbf_ref[sb * MOE_SUB:(sb + 1) * MOE_SUB, :]
        gu = _dot(x, wgu_ref[0, 0]) + bgu_ref[0]
        ng = MOE_TF // LANE
        g = jnp.concatenate([gu[:, 2 * b * LANE:(2 * b + 1) * LANE] for b in range(ng)], axis=-1)
        u = jnp.concatenate([gu[:, (2 * b + 1) * LANE:(2 * b + 2) * LANE] for b in range(ng)], axis=-1)
        g = jnp.minimum(g, SWIGLU_LIMIT)
        u = jnp.clip(u, -SWIGLU_LIMIT, SWIGLU_LIMIT)
        act = g * (1.0 / (1.0 + jnp.exp(-(g * SWIGLU_ALPHA)))) * (u + 1.0)
        acc_ref[sb * MOE_SUB:(sb + 1) * MOE_SUB, :] += _dot(act.astype(BF16), wdb_ref[...])

    dma_both = jnp.logical_and(nv_next > 0, nv_prev > 0)
    per_sb = MOE_SUB // nsub

    def wait_scatter(buf):
        base = pl.multiple_of(buf * ost_rows, ost_rows)
        pltpu.make_async_copy(ost_ref.at[pl.ds(base, ost_rows)], y_ref.at[pl.ds(0, ost_rows)], sem_s.at[buf]).wait()

    def stage(sb):
        for s in range(OUT_ROWS):
            dst = pl.ds(slot * ost_rows + sb * MOE_SUB * OUT_ROWS + s, MOE_SUB, stride=OUT_ROWS)
            ost_ref[dst, :] = acc_ref[sb * MOE_SUB:(sb + 1) * MOE_SUB, s * LANE:(s + 1) * LANE]

    @pl.when(jnp.logical_and(j == 0, nv_prev2 > 0))
    def _():
        wait_scatter(slot)

    for sb in range(nsub):
        row0 = j * MOE_SUB + sb * per_sb
        compute = sb * MOE_SUB < nv
        fused = jnp.logical_and(compute, dma_both)

        @pl.when(fused)
        def _():
            for r in range(per_sb):
                gather_row(cs_next, slot_next, row0 + r)
                scatter_row(row0 + r)
            sub_block(sb)
            if sb > 0:
                stage(sb - 1)

        @pl.when(jnp.logical_not(fused))
        def _():
            @pl.when(nv_next > 0)
            def _():
                for r in range(per_sb):
                    gather_row(cs_next, slot_next, row0 + r)

            @pl.when(nv_prev > 0)
            def _():
                for r in range(per_sb):
                    scatter_row(row0 + r)

            @pl.when(compute)
            def _():
                sub_block(sb)

    @pl.when(j == last_j)
    def _():
        for sb in range(nsub):
            staged_by_next = jnp.logical_and(dma_both, (sb + 1) * MOE_SUB < nv) if sb + 1 < nsub else False

            @pl.when(jnp.logical_and(sb * MOE_SUB < nv, jnp.logical_not(staged_by_next)))
            def _():
                stage(sb)

        @pl.when(jnp.logical_and(c == last_c, nv_prev > 0))
        def _():
            wait_scatter(slot_next)


def _moe(chunk_expert, chunk_rows, chunk_start, order, h2p, wgu, bgu, w_down, b_down):
    nj = D_FF // MOE_TF
    ngrid = chunk_expert.shape[0]
    nk = order.shape[0]

    def jj(c, j, nv):
        return jnp.where(nv[c] > 0, j, nj - 1)

    return pl.pallas_call(
        _moe_kernel,
        grid_spec=pltpu.PrefetchScalarGridSpec(
            num_scalar_prefetch=4,
            grid=(ngrid, nj),
            in_specs=[
                pl.BlockSpec(memory_space=pl.ANY),
                pl.BlockSpec((1, 1, D_MODEL, 2 * MOE_TF), lambda c, j, ce, nv, cs, od: (ce[c], jj(c, j, nv), 0, 0)),
                pl.BlockSpec((1, 1, 2 * MOE_TF), lambda c, j, ce, nv, cs, od: (ce[c], 0, jj(c, j, nv))),
                pl.BlockSpec((1, MOE_TF, D_MODEL), lambda c, j, ce, nv, cs, od: (ce[c], jj(c, j, nv), 0)),
                pl.BlockSpec((1, 1, D_MODEL), lambda c, j, ce, nv, cs, od: (ce[c], 0, 0))],
            out_specs=pl.BlockSpec(memory_space=pl.ANY),
            scratch_shapes=[pltpu.VMEM((2 * MOE_CHUNK * PACK_ROWS, LANE), U32),
                            pltpu.VMEM((MOE_CHUNK, D_MODEL), BF16),
                            pltpu.VMEM((MOE_CHUNK, D_MODEL), F32),
                            pltpu.VMEM((MOE_TF, D_MODEL), BF16),
                            pltpu.VMEM((2 * MOE_CHUNK * OUT_ROWS, LANE), F32),
                            pltpu.SemaphoreType.DMA((2,)),
                            pltpu.SemaphoreType.DMA((2,))]),
        out_shape=jax.ShapeDtypeStruct(((nk + 2 * MOE_CHUNK) * OUT_ROWS, LANE), F32),
        compiler_params=_cparams(("arbitrary", "arbitrary")),
        name="moe_experts",
    )(chunk_expert, chunk_rows, chunk_start, order, h2p, wgu, bgu, w_down, b_down)


def _combine_kernel(gate_ref, x1_ref, mod_ref, gfin_ref, ye_ref, y_ref):
    gate = gate_ref[...]
    pieces = []
    for s in range(OUT_ROWS):
        acc = None
        for k in range(TOP_K):
            rows = ye_ref[pl.ds(k * OUT_ROWS + s, COMBINE_TOKENS, stride=TOP_K * OUT_ROWS), :]
            term = gate[:, k:k + 1] * rows
            acc = term if acc is None else acc + term
        pieces.append(acc)
    y = jnp.concatenate(pieces, axis=-1)
    gf = mod_ref[0, 5:6, :]
    y_ref[...] = _rms(x1_ref[...] + gf * y, gfin_ref[...])


def _combine(gates, x1, mod, g_final, y_experts, *, tile0, ntiles, mod_row):
    tt = COMBINE_TOKENS
    return pl.pallas_call(
        _combine_kernel,
        grid=(ntiles,),
        in_specs=[pl.BlockSpec((tt, LANE), lambda i: (tile0 + i, 0)),
                  pl.BlockSpec((tt, D_MODEL), lambda i: (tile0 + i, 0)),
                  pl.BlockSpec((1, 6, D_MODEL), lambda i: (mod_row(i), 0, 0)),
                  _const_spec((1, D_MODEL)),
                  pl.BlockSpec((tt * TOP_K * OUT_ROWS, LANE), lambda i: (tile0 + i, 0))],
        out_specs=pl.BlockSpec((tt, D_MODEL), lambda i: (i, 0)),
        out_shape=jax.ShapeDtypeStruct((ntiles * tt, D_MODEL), F32),
        compiler_params=_cparams(("arbitrary",)),
        name="moe_combine_final_norm",
    )(gates, x1, mod, g_final, y_experts)


def _routing(top_idx, nch):
    n = top_idx.shape[0]
    nk = n * TOP_K
    flat_e = top_idx.reshape(nk)
    order = jnp.argsort(flat_e, stable=True).astype(jnp.int32)
    onehot = flat_e[:, None] == jnp.arange(N_EXPERTS, dtype=jnp.int32)[None, :]
    counts = jnp.sum(onehot.astype(jnp.int32), axis=0)
    grp_start = jnp.cumsum(counts) - counts
    chunks_e = (counts + MOE_CHUNK - 1) // MOE_CHUNK
    chunk_end = jnp.cumsum(chunks_e)
    chunk_start = chunk_end - chunks_e

    total = chunk_end[-1]
    cidx = jnp.arange(nch + 1, dtype=jnp.int32)
    active = cidx < total
    ce = jnp.minimum(jnp.searchsorted(chunk_end, cidx, side='right'), N_EXPERTS - 1).astype(jnp.int32)
    local = (cidx - chunk_start[ce]) * MOE_CHUNK
    nv = jnp.where(active, jnp.clip(counts[ce] - local, 0, MOE_CHUNK), 0).astype(jnp.int32)
    last = jnp.maximum(total - 1, 0)
    chunk_expert = jnp.where(active, ce, ce[last]).astype(jnp.int32)
    chunk_first = jnp.where(active, jnp.clip(grp_start[ce] + local, 0, nk - 1), 0).astype(jnp.int32)
    return order, chunk_expert, nv, chunk_first


def _rope_tables(t):
    pos = jnp.arange(t)
    rows = (pos // GRID_W).astype(F32)
    cols = (pos % GRID_W).astype(F32)
    inv = ROPE_THETA ** (-(jnp.arange(ROPE_AXIS // 2, dtype=F32) * 2.0 / ROPE_AXIS))
    ar = rows[:, None] * inv
    ac = cols[:, None] * inv
    ang = jnp.concatenate([ar, ar, ac, ac], axis=-1)
    return jnp.cos(ang), jnp.sin(ang)


def _rot_cols(w):
    half = ROPE_AXIS // 2
    src = np.concatenate([np.arange(half, ROPE_AXIS), np.arange(0, half),
                          np.arange(ROPE_AXIS + half, 2 * ROPE_AXIS), np.arange(ROPE_AXIS, ROPE_AXIS + half)])
    sign = np.concatenate([-np.ones(half), np.ones(half), -np.ones(half), np.ones(half)]).astype(np.float32)
    return w[..., src] * sign


def kernel(x_prompt, x_sample, cache_na_k, cache_na_v, cache_mla_ckv, cache_mla_krope, c, c_ctx, g_attn, g_ffn, g_final, w_mod, b_mod, w_in, w_out, na_rpb, g_q_a, w_q_b, g_kv_a, w_kv_b, w_router, b_router, w_gate_up, b_gate_up, w_down, b_down):
    bp, sp, d = x_prompt.shape
    bd, td, _ = x_sample.shape
    assert d == D_MODEL and w_mod.shape[0] == 1, "one trunk layer of width D_MODEL"
    n_p = bp * sp
    n_s = bd * td
    xp = x_prompt.reshape(n_p, d)
    xs = x_sample.reshape(n_s, d)

    c8 = jnp.zeros((8, d), F32).at[0].set(c_ctx).at[1:1 + bd].set(c)
    mod = _modulation(c8, w_mod[0], b_mod[0].reshape(1, -1)).reshape(8, 6, d)

    w_in0 = w_in[0]
    w_kr = w_in0[:, KR_OFF:KR_OFF + QK_ROPE]
    w_in_p = jnp.concatenate([w_in0, jnp.zeros((d, LANE - QK_ROPE), F32)], axis=1).astype(BF16)
    w_in_s = jnp.concatenate([w_in0, _rot_cols(w_kr)], axis=1).astype(BF16)
    wq = w_q_b[0].reshape(Q_LORA, MLA_HEADS, QK_NOPE + QK_ROPE)
    zpad = jnp.zeros((Q_LORA, MLA_HEADS, Q_PAD - QK_NOPE - QK_ROPE), F32)
    wq_pad = jnp.concatenate([wq, zpad], axis=-1).reshape(Q_LORA, MLA_HEADS * Q_PAD)
    wq_rot = jnp.concatenate([jnp.zeros((Q_LORA, MLA_HEADS, QK_NOPE), F32), _rot_cols(wq[..., QK_NOPE:]), zpad],
                             axis=-1).reshape(Q_LORA, MLA_HEADS * Q_PAD)
    wqb_p = wq_pad.astype(BF16)
    wqb_s = jnp.concatenate([wq_pad, wq_rot], axis=1).astype(BF16)
    w_kvb = w_kv_b[0].astype(BF16)
    cos, sin = _rope_tables(td)
    cosq = jnp.concatenate([jnp.ones((td, QK_NOPE), F32), cos, jnp.ones((td, Q_PAD - QK_NOPE - QK_ROPE), F32)], axis=1)
    sinq = jnp.concatenate([jnp.zeros((td, QK_NOPE), F32), sin, jnp.zeros((td, Q_PAD - QK_NOPE - QK_ROPE), F32)], axis=1)
    cosq = cosq * MLA_QSCALE
    sinq = sinq * MLA_QSCALE
    csk = jnp.concatenate([cos, sin], axis=1)

    tiles_per_seq = td // ROW_TILE
    g_attn2 = g_attn[0].reshape(1, d)
    gq2 = g_q_a[0].reshape(1, Q_LORA)
    gkv2 = g_kv_a[0].reshape(1, KV_LORA)

    naq_p, nak_p, nav_p, q_p, ckv_p, kr_p, krp_p = _pre_attention(
        xp, mod, g_attn2, w_in_p, gq2, wqb_p, gkv2, None, rope=False, mod_row=lambda i: 0)
    ona_p, omla_p = _prompt_attention(naq_p, nak_p, nav_p, q_p, ckv_p, krp_p, w_kvb, sp)

    naq_s, nak_s, nav_s, q_s, ckv_s, krp_s = _pre_attention(
        xs, mod, g_attn2, w_in_s, gq2, wqb_s, gkv2, (cosq, sinq, csk), rope=True,
        mod_row=lambda i: 1 + i // tiles_per_seq)
    past = cache_na_k.shape[2]
    kc = cache_na_k[:, 0].reshape(bd, past, NA_WIDTH)
    vc = cache_na_v[:, 0].reshape(bd, past, NA_WIDTH)
    bias = _na_bias_table(na_rpb[0] * LOG2E, td // GRID_W)
    ona_s = _sample_na(naq_s, nak_s, nav_s, kc, vc, bias, bd, td)
    ckv_all = jnp.concatenate([ckv_s.reshape(bd, td, KV_LORA), cache_mla_ckv[:, 0].astype(BF16)], axis=1)
    krp_c = jnp.concatenate([cache_mla_krope[:, 0], jnp.zeros((bd, past, LANE - QK_ROPE), F32)], axis=-1).astype(BF16)
    krp_all = jnp.concatenate([krp_s.reshape(bd, td, LANE), krp_c], axis=1)
    kf, vv = _kv_expand(ckv_all, krp_all, w_kvb)
    omla_s = _sample_mla(q_s, kf, vv, bd, td)

    npt = n_p // POST_TILE
    post_tiles_per_seq = td // POST_TILE
    x1, h2p, idx128, gate128 = _post_attention(
        xp, xs, ona_p, omla_p, ona_s, omla_s, mod, w_out[0].astype(BF16), g_ffn[0].reshape(1, d),
        w_router[0], b_router[0].reshape(1, N_EXPERTS),
        mod_row=lambda i: jnp.where(i < npt, 0, 1 + jnp.maximum(i - npt, 0) // post_tiles_per_seq))

    n = n_p + n_s
    nch = n * TOP_K // MOE_CHUNK + N_EXPERTS
    order, chunk_expert, chunk_rows, chunk_first = _routing(idx128[:, :TOP_K], nch)
    wgu = _deinterleave_gate_up(w_gate_up[0])
    ng = 2 * D_FF // (2 * LANE)
    bgu = b_gate_up[0].reshape(N_EXPERTS, ng, LANE, 2).transpose(0, 1, 3, 2).reshape(N_EXPERTS, 1, 2 * D_FF)
    y_experts = _moe(chunk_expert, chunk_rows, chunk_first, order, h2p, wgu, bgu, w_down[0],
                     b_down[0].reshape(N_EXPERTS, 1, d))

    tt = COMBINE_TOKENS
    gfin = g_final.reshape(1, d)
    ctiles_seq = td // tt
    y_p = _combine(gate128, x1, mod, gfin, y_experts, tile0=0, ntiles=n_p // tt, mod_row=lambda i: 0)
    y_s = _combine(gate128, x1, mod, gfin, y_experts, tile0=n_p // tt, ntiles=n_s // tt,
                   mod_row=lambda i: 1 + i // ctiles_seq)

    return (y_p.reshape(bp, sp, d), y_s.reshape(bd, td, d),
            nak_p.reshape(bp, 1, sp, NA_HEADS, NA_HEAD_DIM), nav_p.reshape(bp, 1, sp, NA_HEADS, NA_HEAD_DIM),
            ckv_p.reshape(bp, 1, sp, KV_LORA), kr_p.reshape(bp, 1, sp, QK_ROPE))
```
